```python
import math
import jax, jax.numpy as jnp
from jax import lax
import numpy as np

D_MODEL = 2048
BATCH = 32
SEQ = 256
DEPTH = 2
DEC_BATCH = 2
DEC_SEQ = 1024
PAST_LEN = 256

GRID_W = 64
HEAD_DIM = 128
N_Q_HEADS = D_MODEL // HEAD_DIM
N_KV_HEADS = 4
Q_PER_KV = N_Q_HEADS // N_KV_HEADS
ATTN_WIDTH = N_Q_HEADS * HEAD_DIM
KV_WIDTH = N_KV_HEADS * HEAD_DIM
WINDOW = 128
BLOCK = 128
ROPE_BASE = 10000.0
SSD_WIDTH = 2 * D_MODEL
SSD_HEADDIM = 64
N_SSD_HEADS = SSD_WIDTH // SSD_HEADDIM
D_STATE = 128
N_SSD_GROUPS = 8
HEADS_PER_GROUP = N_SSD_HEADS // N_SSD_GROUPS
SSD_CHUNK = 128
D_CONV = 5
BC_WIDTH = N_SSD_GROUPS * D_STATE
CONV_WIDTH = SSD_WIDTH + 2 * BC_WIDTH
EPS = 1e-6
IN_SPLITS = (ATTN_WIDTH, KV_WIDTH, KV_WIDTH, ATTN_WIDTH, CONV_WIDTH, SSD_WIDTH, N_SSD_HEADS, N_SSD_HEADS, D_MODEL, D_MODEL)
IN_WIDTH = 19584

kernel_name = 'bidir_ssd_swa_hybrid_dit_step'


def split_cols(x, sizes):
    offsets = np.cumsum(np.array(sizes))[:-1].tolist()
    return jnp.split(x, offsets, axis=-1)


def rmsnorm(x, g):
    xf = x.astype(jnp.float32)
    y = xf * lax.rsqrt(jnp.mean(xf * xf, axis=-1, keepdims=True) + EPS)
    return (y * g.astype(jnp.float32)).astype(x.dtype)


def axial_rope(x):
    length = x.shape[1]
    rows = length // GRID_W
    t_row = jnp.repeat(jnp.arange(rows), GRID_W)
    t_col = jnp.tile(jnp.arange(GRID_W), rows)
    sec = HEAD_DIM // 2
    half = sec // 2
    freqs = ROPE_BASE ** (-jnp.arange(half, dtype=jnp.float32) / half)

    def rot(xs, pos):
        ang = pos.astype(jnp.float32)[:, None] * freqs[None, :]
        cos = jnp.cos(ang)[None, :, None, :]
        sin = jnp.sin(ang)[None, :, None, :]
        x1 = xs[..., :half].astype(jnp.float32)
        x2 = xs[..., half:].astype(jnp.float32)
        return jnp.concatenate([x1 * cos - x2 * sin, x2 * cos + x1 * sin], axis=-1)

    out = jnp.concatenate([rot(x[..., :sec], t_row), rot(x[..., sec:], t_col)], axis=-1)
    return out.astype(x.dtype)


def centred_conv(u, w, b):
    out = lax.conv_general_dilated(
        u, w[:, None, :].astype(u.dtype), window_strides=(1,),
        padding=[(D_CONV // 2, D_CONV // 2)],
        dimension_numbers=('NWC', 'WIO', 'NWC'),
        feature_group_count=u.shape[-1])
    return out + b.astype(u.dtype)


def mixer_inputs(x, cond, norm_g, w_mod, b_mod, w_in, conv_w, conv_b):
    shift, scale, gate = jnp.split(jax.nn.silu(cond) @ w_mod + b_mod, 3, axis=-1)
    h = rmsnorm(x, norm_g) * (1 + scale) + shift
    q, k, v, z_a, xbc, z_s, dt_f, dt_b, g_a, g_s = split_cols(h @ w_in, IN_SPLITS)
    xbc = jax.nn.silu(centred_conv(xbc, conv_w, conv_b))
    xs, b_ssm, c_ssm = split_cols(xbc, (SSD_WIDTH, BC_WIDTH, BC_WIDTH))
    bsz, length = x.shape[:2]
    q = q.reshape(bsz, length, N_Q_HEADS, HEAD_DIM)
    k = k.reshape(bsz, length, N_KV_HEADS, HEAD_DIM)
    v = v.reshape(bsz, length, N_KV_HEADS, HEAD_DIM)
    return q, k, v, z_a, xs, b_ssm, c_ssm, z_s, dt_f, dt_b, g_a, g_s, gate


def context_attention(q, k, v, sink):
    bsz, lc = q.shape[:2]
    nq = lc // BLOCK
    scale = HEAD_DIM ** -0.5
    qb = q.reshape(bsz, nq, BLOCK, N_KV_HEADS, Q_PER_KV, HEAD_DIM).transpose(1, 0, 2, 3, 4, 5)
    sink_g = sink.astype(jnp.float32).reshape(1, N_KV_HEADS, Q_PER_KV, 1, 1)

    def one_block(qblk):
        s = jnp.einsum('bqgrd,bkgd->bgrqk', qblk, k).astype(jnp.float32) * scale
        sink_col = jnp.broadcast_to(sink_g, s.shape[:-1] + (1,))
        p = jax.nn.softmax(jnp.concatenate([sink_col, s], axis=-1), axis=-1)[..., 1:]
        return jnp.einsum('bgrqk,bkgd->bqgrd', p.astype(v.dtype), v)

    o = lax.map(one_block, qb)
    return o.transpose(1, 0, 2, 3, 4, 5).reshape(bsz, lc, ATTN_WIDTH)


def latent_attention(q, k, v, kc, vc, sink):
    bsz, length = q.shape[:2]
    nb = length // BLOCK
    lc = kc.shape[1]
    scale = HEAD_DIM ** -0.5
    qb = q.reshape(bsz, nb, BLOCK, N_KV_HEADS, Q_PER_KV, HEAD_DIM)
    pad = ((0, 0), (BLOCK, BLOCK), (0, 0), (0, 0))
    kp = jnp.pad(k, pad).reshape(bsz, nb + 2, BLOCK, N_KV_HEADS, HEAD_DIM)
    vp = jnp.pad(v, pad).reshape(bsz, nb + 2, BLOCK, N_KV_HEADS, HEAD_DIM)
    kb = jnp.concatenate([kp[:, :-2], kp[:, 1:-1], kp[:, 2:]], axis=2)
    vb = jnp.concatenate([vp[:, :-2], vp[:, 1:-1], vp[:, 2:]], axis=2)
    qpos = jnp.arange(nb)[:, None] * BLOCK + jnp.arange(BLOCK)[None, :]
    kpos = jnp.arange(nb)[:, None] * BLOCK - BLOCK + jnp.arange(3 * BLOCK)[None, :]
    mask = ((jnp.abs(qpos[:, :, None] - kpos[:, None, :]) <= WINDOW)
            & (kpos[:, None, :] >= 0) & (kpos[:, None, :] < length))
    s_lat = jnp.einsum('bnqgrd,bnkgd->bngrqk', qb, kb).astype(jnp.float32) * scale
    s_lat = jnp.where(mask[None, :, None, None], s_lat, -jnp.inf)
    s_ctx = jnp.einsum('bnqgrd,bkgd->bngrqk', qb, kc).astype(jnp.float32) * scale
    sink_col = jnp.broadcast_to(sink.astype(jnp.float32).reshape(1, 1, N_KV_HEADS, Q_PER_KV, 1, 1),
                                s_ctx.shape[:-1] + (1,))
    p = jax.nn.softmax(jnp.concatenate([sink_col, s_ctx, s_lat], axis=-1), axis=-1)
    p_ctx = p[..., 1:1 + lc].astype(v.dtype)
    p_lat = p[..., 1 + lc:].astype(v.dtype)
    o = (jnp.einsum('bngrqk,bkgd->bnqgrd', p_ctx, vc)
         + jnp.einsum('bngrqk,bnkgd->bnqgrd', p_lat, vb))
    return o.reshape(bsz, length, ATTN_WIDTH)


def ssd_scan(x, dt, a, bm, cm, s0):
    f32 = jnp.float32
    bsz, length = x.shape[:2]
    nc = length // SSD_CHUNK
    g, r, q = N_SSD_GROUPS, HEADS_PER_GROUP, SSD_CHUNK
    xc = x.astype(f32).reshape(bsz, nc, q, g, r, SSD_HEADDIM)
    dtc = dt.reshape(bsz, nc, q, g, r)
    bc = bm.astype(f32).reshape(bsz, nc, q, g, D_STATE)
    cc = cm.astype(f32).reshape(bsz, nc, q, g, D_STATE)
    acs = jnp.cumsum(dtc * a.reshape(g, r), axis=2)
    acs_t = jnp.moveaxis(acs, 2, -1)
    seg = acs_t[..., :, None] - acs_t[..., None, :]
    lower = jnp.tril(jnp.ones((q, q), dtype=bool))
    decay_in = jnp.exp(jnp.where(lower, seg, -jnp.inf))
    cb = jnp.einsum('bcign,bcjgn->bcgij', cc, bc)
    xdt = xc * dtc[..., None]
    y_diag = jnp.einsum('bcgij,bcgrij,bcjgrp->bcigrp', cb, decay_in, xdt)
    acs_last = acs[:, :, -1]
    decay_to_end = jnp.exp(acs_last[:, :, None] - acs)
    chunk_states = jnp.einsum('bcjgn,bcjgr,bcjgrp->bcgrpn', bc, decay_to_end, xdt)
    chunk_decay = jnp.exp(acs_last)

    def step(s, inp):
        dec, st = inp
        return s * dec[..., None, None] + st, s

    s_init = s0.astype(f32).reshape(bsz, g, r, SSD_HEADDIM, D_STATE)
    s_final, s_starts = lax.scan(step, s_init, (jnp.moveaxis(chunk_decay, 1, 0),
                                                 jnp.moveaxis(chunk_states, 1, 0)))
    s_starts = jnp.moveaxis(s_starts, 0, 1)
    y_off = jnp.einsum('bcign,bcgrpn,bcigr->bcigrp', cc, s_starts, jnp.exp(acs))
    y = (y_diag + y_off).reshape(bsz, length, N_SSD_HEADS, SSD_HEADDIM)
    return y.astype(x.dtype), s_final.reshape(bsz, N_SSD_HEADS, SSD_HEADDIM, D_STATE).astype(x.dtype)


def ssd_branch(xs, b_ssm, c_ssm, z_s, dt_f, dt_b, s0_f, s0_b,
               a_log_f, a_log_b, dt_bias_f, dt_bias_b, d_skip, ssd_norm_g):
    bsz, length = xs.shape[:2]
    f32 = jnp.float32
    x4 = xs.reshape(bsz, length, N_SSD_HEADS, SSD_HEADDIM)
    bm = b_ssm.reshape(bsz, length, N_SSD_GROUPS, D_STATE)
    cm = c_ssm.reshape(bsz, length, N_SSD_GROUPS, D_STATE)
    dtf = jax.nn.softplus(dt_f.astype(f32) + dt_bias_f.astype(f32))
    dtb = jax.nn.softplus(dt_b.astype(f32) + dt_bias_b.astype(f32))
    a_f = -jnp.exp(a_log_f.astype(f32))
    a_b = -jnp.exp(a_log_b.astype(f32))
    y_f, s_f = ssd_scan(x4, dtf, a_f, bm, cm, s0_f)
    y_b, s_b = ssd_scan(jnp.flip(x4, 1), jnp.flip(dtb, 1), a_b, jnp.flip(bm, 1), jnp.flip(cm, 1), s0_b)
    y = y_f + jnp.flip(y_b, 1) + d_skip[:, None].astype(x4.dtype) * x4
    y = y.reshape(bsz, length, SSD_WIDTH) * jax.nn.silu(z_s)
    yg = rmsnorm(y.reshape(bsz, length, N_SSD_GROUPS, SSD_WIDTH // N_SSD_GROUPS),
                 ssd_norm_g.reshape(N_SSD_GROUPS, SSD_WIDTH // N_SSD_GROUPS))
    return yg.reshape(bsz, length, SSD_WIDTH), s_f, s_b


def merge(x, o_a, z_a, y_s, g_a, g_s, gate, w_pa, w_ps, w_out):
    branch_a = (o_a * jax.nn.silu(z_a)) @ w_pa
    branch_s = y_s @ w_ps
    merged = jax.nn.sigmoid(g_a) * branch_a + jax.nn.sigmoid(g_s) * branch_s
    return x + gate * (merged @ w_out)


def setup_inputs(seed: int = 0) -> dict:
    key = jax.random.key(seed)
    ks = jax.random.split(key, 26)
    f32 = jnp.float32

    def nrm(k, shape, scale):
        return jax.random.normal(k, shape, f32) * scale

    def dt_bias_init(k):
        dt = jnp.exp(jax.random.uniform(k, (DEPTH, N_SSD_HEADS), f32, math.log(1e-3), math.log(1e-1)))
        return dt + jnp.log(-jnp.expm1(-dt))

    return {
        'x_prompt': nrm(ks[0], (BATCH, SEQ, D_MODEL), 1.0),
        'x_sample': nrm(ks[1], (DEC_BATCH, DEC_SEQ, D_MODEL), 1.0),
        'c': nrm(ks[2], (DEC_BATCH, D_MODEL), 1.0),
        'cache_k': nrm(ks[3], (DEC_BATCH, DEPTH, PAST_LEN, N_KV_HEADS, HEAD_DIM), 1.0),
        'cache_v': nrm(ks[4], (DEC_BATCH, DEPTH, PAST_LEN, N_KV_HEADS, HEAD_DIM), 1.0),
        'state_ssm_fwd': nrm(ks[5], (DEC_BATCH, DEPTH, N_SSD_HEADS, SSD_HEADDIM, D_STATE), 0.1),
        'state_ssm_bwd': nrm(ks[6], (DEC_BATCH, DEPTH, N_SSD_HEADS, SSD_HEADDIM, D_STATE), 0.1),
        'c_ctx': nrm(ks[7], (D_MODEL,), 1.0),
        'norm_g': 1.0 + nrm(ks[8], (DEPTH, D_MODEL), 0.02),
        'w_mod': nrm(ks[9], (DEPTH, D_MODEL, 3 * D_MODEL), 0.5 * D_MODEL ** -0.5),
        'b_mod': nrm(ks[10], (DEPTH, 3 * D_MODEL), 0.02),
        'w_in': nrm(ks[11], (DEPTH, D_MODEL, IN_WIDTH), D_MODEL ** -0.5),
        'conv_w': nrm(ks[12], (DEPTH, D_CONV, CONV_WIDTH), D_CONV ** -0.5),
        'conv_b': nrm(ks[13], (DEPTH, CONV_WIDTH), 0.02),
        'attn_sink': nrm(ks[14], (DEPTH, N_Q_HEADS), 1.0),
        'a_log_fwd': jnp.log(jax.random.uniform(ks[15], (DEPTH, N_SSD_HEADS), f32, 1.0, 16.0)),
        'a_log_bwd': jnp.log(jax.random.uniform(ks[16], (DEPTH, N_SSD_HEADS), f32, 1.0, 16.0)),
        'dt_bias_fwd': dt_bias_init(ks[17]),
        'dt_bias_bwd': dt_bias_init(ks[18]),
        'd_skip': 1.0 + nrm(ks[19], (DEPTH, N_SSD_HEADS), 0.02),
        'ssd_norm_g': 1.0 + nrm(ks[20], (DEPTH, SSD_WIDTH), 0.02),
        'w_pa': nrm(ks[21], (DEPTH, ATTN_WIDTH, D_MODEL), ATTN_WIDTH ** -0.5),
        'w_ps': nrm(ks[22], (DEPTH, SSD_WIDTH, D_MODEL), SSD_WIDTH ** -0.5),
        'w_out': nrm(ks[23], (DEPTH, D_MODEL, D_MODEL), D_MODEL ** -0.5),
        'final_norm_g': 1.0 + nrm(ks[24], (D_MODEL,), 0.02),
    }


def reference(x_prompt, x_sample, c, cache_k, cache_v, state_ssm_fwd, state_ssm_bwd, c_ctx,
              norm_g, w_mod, b_mod, w_in, conv_w, conv_b, attn_sink, a_log_fwd, a_log_bwd,
              dt_bias_fwd, dt_bias_bwd, d_skip, ssd_norm_g, w_pa, w_ps, w_out, final_norm_g):
    xp = x_prompt
    xl = x_sample
    bsz_p = x_prompt.shape[0]
    new_k, new_v, new_sf, new_sb = [], [], [], []
    for l in range(DEPTH):
        q, k, v, z_a, xs, b_ssm, c_ssm, z_s, dt_f, dt_b, g_a, g_s, gate = mixer_inputs(
            xp, c_ctx[None, None, :], norm_g[l], w_mod[l], b_mod[l], w_in[l], conv_w[l], conv_b[l])
        o_a = context_attention(q, k, v, attn_sink[l])
        zero_state = jnp.zeros((bsz_p, N_SSD_HEADS, SSD_HEADDIM, D_STATE), xp.dtype)
        y_s, s_f, s_b = ssd_branch(xs, b_ssm, c_ssm, z_s, dt_f, dt_b, zero_state, zero_state,
                                   a_log_fwd[l], a_log_bwd[l], dt_bias_fwd[l], dt_bias_bwd[l],
                                   d_skip[l], ssd_norm_g[l])
        xp = merge(xp, o_a, z_a, y_s, g_a, g_s, gate, w_pa[l], w_ps[l], w_out[l])
        new_k.append(k)
        new_v.append(v)
        new_sf.append(s_f)
        new_sb.append(s_b)
        q, k, v, z_a, xs, b_ssm, c_ssm, z_s, dt_f, dt_b, g_a, g_s, gate = mixer_inputs(
            xl, c[:, None, :], norm_g[l], w_mod[l], b_mod[l], w_in[l], conv_w[l], conv_b[l])
        o_a = latent_attention(axial_rope(q), axial_rope(k), v, cache_k[:, l], cache_v[:, l], attn_sink[l])
        y_s, _, _ = ssd_branch(xs, b_ssm, c_ssm, z_s, dt_f, dt_b, state_ssm_fwd[:, l], state_ssm_bwd[:, l],
                               a_log_fwd[l], a_log_bwd[l], dt_bias_fwd[l], dt_bias_bwd[l],
                               d_skip[l], ssd_norm_g[l])
        xl = merge(xl, o_a, z_a, y_s, g_a, g_s, gate, w_pa[l], w_ps[l], w_out[l])
    y_prompt = rmsnorm(xp, final_norm_g)
    y_sample = rmsnorm(xl, final_norm_g)
    new_cache_k = jnp.stack(new_k, axis=1)
    new_cache_v = jnp.stack(new_v, axis=1)
    new_state_ssm_fwd = jnp.stack(new_sf, axis=1)
    new_state_ssm_bwd = jnp.stack(new_sb, axis=1)
    return (y_prompt, y_sample, new_cache_k, new_cache_v, new_state_ssm_fwd, new_state_ssm_bwd)
```

```python
import functools
import math

import numpy as np
import jax
import jax.numpy as jnp
from jax import lax
from jax.experimental import pallas as pl
from jax.experimental.pallas import tpu as pltpu

F32 = jnp.float32
BF16 = jnp.bfloat16

D_MODEL = 2048
HEAD_DIM = 128
N_Q_HEADS = 16
N_KV_HEADS = 4
Q_PER_KV = 4
ATTN_WIDTH = 2048
KV_WIDTH = 512
WINDOW = 128
GRID_W = 64
ROPE_BASE = 10000.0
SSD_WIDTH = 4096
SSD_HEADDIM = 64
N_SSD_HEADS = 64
D_STATE = 128
N_SSD_GROUPS = 8
HEADS_PER_GROUP = 8
GROUP_WIDTH = SSD_WIDTH // N_SSD_GROUPS
CHUNK = 128
D_CONV = 5
BC_WIDTH = 1024
CONV_WIDTH = 6144
EPS = 1e-6
MOD_WIDTH = 3 * D_MODEL

COL_Q = 0
COL_K = 2048
COL_V = 2560
COL_ZA = 3072
COL_XBC = 5120
COL_ZS = 11264
COL_GA = 15360
COL_GS = 17408
ACT_WIDTH = 19456
W_IN_DT = 15360
DT_WIDTH = 2 * N_SSD_HEADS

LANES = 128
SUBLANES = 8
VMEM_LIMIT = 56 * 1024 * 1024

COND_ROWS = 8


def _cparams(sem):
    return pltpu.CompilerParams(dimension_semantics=sem, vmem_limit_bytes=VMEM_LIMIT)


def _dot(a, b):
    return jnp.dot(a, b, preferred_element_type=F32)


def _dot_nt(a, b):
    return lax.dot_general(a, b, (((1,), (1,)), ((), ())), preferred_element_type=F32)


def _split2(x):
    hi = x.astype(BF16)
    lo = (x - hi.astype(F32)).astype(BF16)
    return hi, lo


def _split3(x):
    p1 = x.astype(BF16)
    r1 = x - p1.astype(F32)
    p2 = r1.astype(BF16)
    p3 = (r1 - p2.astype(F32)).astype(BF16)
    return p1, p2, p3


def _dot3(a, b):
    ah, al = _split2(a)
    bh, bl = _split2(b)
    return _dot(ah, bh) + _dot(al, bh) + _dot(ah, bl)


def _sigmoid(x):
    return 1.0 / (1.0 + jnp.exp(-x))


def _silu(x):
    return x * _sigmoid(x)


def _softplus(x):
    return jnp.maximum(x, 0.0) + jnp.log1p(jnp.exp(-jnp.abs(x)))


def _mod_kernel(cond_ref, w_ref, b_ref, o_ref):
    o_ref[...] = _dot3(_silu(cond_ref[...]), w_ref[...]) + b_ref[...]


def _modulation(cond, w_mod, b_mod):
    depth = w_mod.shape[0]
    tn = 512
    return pl.pallas_call(
        _mod_kernel,
        grid=(depth, MOD_WIDTH // tn),
        in_specs=[
            pl.BlockSpec((COND_ROWS, D_MODEL), lambda l, j: (0, 0)),
            pl.BlockSpec((None, D_MODEL, tn), lambda l, j: (l, 0, j)),
            pl.BlockSpec((None, 1, tn), lambda l, j: (l, 0, j)),
        ],
        out_specs=pl.BlockSpec((None, COND_ROWS, tn), lambda l, j: (l, 0, j)),
        out_shape=jax.ShapeDtypeStruct((depth, COND_ROWS, MOD_WIDTH), F32),
        compiler_params=_cparams(("arbitrary", "arbitrary")),
        name="modulation",
    )(cond, w_mod, b_mod.reshape(depth, 1, MOD_WIDTH))


def _prep_kernel(x_ref, shift_ref, scale_ref, g_ref, wdt_ref, wdtt_ref, bias_ref, biast_ref,
                 alog_ref, alogt_ref, h_ref, col_ref, row_ref, *, tm):
    x = x_ref[...]
    ms = jnp.mean(x * x, axis=-1, keepdims=True)
    h = (x * lax.rsqrt(ms + EPS) * g_ref[...]) * (1.0 + scale_ref[...]) + shift_ref[...]
    h_ref[...] = h.astype(BF16)

    hh, hl = _split2(h)
    wh, wl = _split2(wdt_ref[...])
    raw = _dot(hh, wh) + _dot(hl, wh) + _dot(hh, wl)
    wth, wtl = _split2(wdtt_ref[...])
    rawt = _dot_nt(wth, hh) + _dot_nt(wth, hl) + _dot_nt(wtl, hh)

    dt = _softplus(raw + bias_ref[...])
    dta = dt * (-jnp.exp(alog_ref[...]))
    dtt = _softplus(rawt + biast_ref[...])
    dtat = dtt * (-jnp.exp(alogt_ref[...]))

    ii = lax.broadcasted_iota(jnp.int32, (CHUNK, CHUNK), 0)
    kk = lax.broadcasted_iota(jnp.int32, (CHUNK, CHUNK), 1)
    lt = jnp.where(kk <= ii, 1.0, 0.0).astype(BF16)
    ut = jnp.where(kk >= ii, 1.0, 0.0).astype(BF16)
    fwd_lane = (kk % 16) < HEADS_PER_GROUP
    fwd_row = (ii % 16) < HEADS_PER_GROUP

    for c in range(tm // CHUNK):
        rows = slice(c * CHUNK, (c + 1) * CHUNK)
        p1, p2, p3 = _split3(dta[rows, :])
        pre = _dot(lt, p1) + _dot(lt, p2) + _dot(lt, p3)
        suf = _dot(ut, p1) + _dot(ut, p2) + _dot(ut, p3)
        acs = jnp.where(fwd_lane, pre, suf)
        edge = jnp.where(fwd_lane, acs[CHUNK - 1:CHUNK, :], acs[0:1, :])
        e = jnp.exp(acs)
        w1 = dt[rows, :] * jnp.exp(edge - acs)
        for g in range(N_SSD_GROUPS):
            r0 = pltpu.roll(acs, (0 - 16 * g) % LANES, 1)
            r1 = pltpu.roll(e, (16 - 16 * g) % LANES, 1)
            r2 = pltpu.roll(w1, (32 - 16 * g) % LANES, 1)
            packed = jnp.where(kk < 16, r0, jnp.where(kk < 32, r1, jnp.where(kk < 48, r2, 0.0)))
            col_ref[g, rows, :] = packed
        q1, q2, q3 = _split3(dtat[:, rows])
        pre_t = _dot(q1, ut) + _dot(q2, ut) + _dot(q3, ut)
        suf_t = _dot(q1, lt) + _dot(q2, lt) + _dot(q3, lt)
        row_ref[c, 0] = jnp.where(fwd_row, pre_t, suf_t)
        row_ref[c, 1] = dtt[:, rows]


def _prep(x, mod, norm_g, wdt, wdtt, bias, alog, *, tm, group_of):
    m = x.shape[0]
    kern = functools.partial(_prep_kernel, tm=tm)
    vec = lambda i: (0, 0)
    return pl.pallas_call(
        kern,
        grid=(m // tm,),
        in_specs=[
            pl.BlockSpec((tm, D_MODEL), lambda i: (i, 0)),
            pl.BlockSpec((None, 1, D_MODEL), lambda i: (group_of(i, tm), 0, 0)),
            pl.BlockSpec((None, 1, D_MODEL), lambda i: (group_of(i, tm), 0, 1)),
            pl.BlockSpec((1, D_MODEL), vec),
            pl.BlockSpec((D_MODEL, DT_WIDTH), vec),
            pl.BlockSpec((DT_WIDTH, D_MODEL), vec),
            pl.BlockSpec((1, DT_WIDTH), vec),
            pl.BlockSpec((DT_WIDTH, 1), vec),
            pl.BlockSpec((1, DT_WIDTH), vec),
            pl.BlockSpec((DT_WIDTH, 1), vec),
        ],
        out_specs=[
            pl.BlockSpec((tm, D_MODEL), lambda i: (i, 0)),
            pl.BlockSpec((N_SSD_GROUPS, tm, LANES), lambda i: (0, i, 0)),
            pl.BlockSpec((tm // CHUNK, 2, DT_WIDTH, CHUNK), lambda i: (i, 0, 0, 0)),
        ],
        out_shape=[
            jax.ShapeDtypeStruct((m, D_MODEL), BF16),
            jax.ShapeDtypeStruct((N_SSD_GROUPS, m, LANES), F32),
            jax.ShapeDtypeStruct((m // CHUNK, 2, DT_WIDTH, CHUNK), F32),
        ],
        compiler_params=_cparams(("arbitrary",)),
        name="prep",
    )(x, mod, mod, norm_g.reshape(1, D_MODEL), wdt, wdtt, bias.reshape(1, DT_WIDTH),
      bias.reshape(DT_WIDTH, 1), alog.reshape(1, DT_WIDTH), alog.reshape(DT_WIDTH, 1))


INPROJ_TN = 1024
KV_BLOCK = COL_K // INPROJ_TN


def _inproj_kernel(h_ref, w_ref, act_ref, kv_ref):
    acc = _dot(h_ref[...], w_ref[...])
    act_ref[...] = acc.astype(BF16)

    @pl.when(pl.program_id(1) == KV_BLOCK)
    def _():
        kv_ref[...] = acc


def _inproj(h, w, *, tm):
    m = h.shape[0]
    tn = INPROJ_TN
    return pl.pallas_call(
        _inproj_kernel,
        grid=(m // tm, ACT_WIDTH // tn),
        in_specs=[
            pl.BlockSpec((tm, D_MODEL), lambda i, j: (i, 0)),
            pl.BlockSpec((D_MODEL, tn), lambda i, j: (0, j)),
        ],
        out_specs=[
            pl.BlockSpec((tm, tn), lambda i, j: (i, j)),
            pl.BlockSpec((tm, tn), lambda i, j: (i, 0)),
        ],
        out_shape=[
            jax.ShapeDtypeStruct((m, ACT_WIDTH), BF16),
            jax.ShapeDtypeStruct((m, 2 * KV_WIDTH), F32),
        ],
        compiler_params=_cparams(("arbitrary", "arbitrary")),
        name="inproj",
    )(h, w)


PAD = SUBLANES


def _ssd_kernel(*refs, L, has_init, emit_state):
    refs = list(refs)
    (x_ref, b_ref, c_ref, z_ref, col_ref, row_ref, cwx_ref, cwb_ref, cwc_ref,
     cbx_ref, cbb_ref, cbc_ref, dsk_ref, ng_ref) = refs[:14]
    pos = 14
    if has_init:
        s0f_ref, s0b_ref, _yprev_ref = refs[pos:pos + 3]
        pos += 3
    y_ref = refs[pos]
    pos += 1
    if emit_state:
        sf_ref, sb_ref = refs[pos:pos + 2]
        pos += 2
    pad_s, xc_s, bc_s, cc_s, yacc_s, sft_s, sbt_s = refs[pos:]
    nc = L // CHUNK

    def conv_part(src_ref, w_ref, bias_ref, dst_s, width):
        zeros = jnp.zeros((PAD, width), F32)
        pad_s[0:PAD, 0:width] = zeros
        pad_s[L + PAD:L + 2 * PAD, 0:width] = zeros
        for c in range(nc):
            pad_s[PAD + c * CHUNK:PAD + (c + 1) * CHUNK, 0:width] = (
                src_ref[c * CHUNK:(c + 1) * CHUNK, :].astype(F32))
        for c in range(nc):
            for s in range(width // LANES):
                ls = slice(s * LANES, (s + 1) * LANES)
                acc = jnp.broadcast_to(bias_ref[:, ls], (CHUNK, LANES))
                for k in range(D_CONV):
                    start = PAD + c * CHUNK + k - D_CONV // 2
                    acc = acc + w_ref[k:k + 1, ls] * pad_s[start:start + CHUNK, ls]
                dst_s[c * CHUNK:(c + 1) * CHUNK, ls] = _silu(acc)

    conv_part(x_ref, cwx_ref, cbx_ref, xc_s, GROUP_WIDTH)
    conv_part(b_ref, cwb_ref, cbb_ref, bc_s, D_STATE)
    conv_part(c_ref, cwc_ref, cbc_ref, cc_s, D_STATE)

    if has_init:
        sft_s[...] = s0f_ref[0].T
        sbt_s[...] = s0b_ref[0].T
    else:
        sft_s[...] = jnp.zeros_like(sft_s)
        sbt_s[...] = jnp.zeros_like(sbt_s)

    ii = lax.broadcasted_iota(jnp.int32, (CHUNK, CHUNK), 0)
    jj = lax.broadcasted_iota(jnp.int32, (CHUNK, CHUNK), 1)
    lower = jj <= ii
    upper = jj >= ii
    left = jj < SSD_HEADDIM
    neg_inf = -jnp.inf

    def expand(col, off):
        parts = []
        for k in range(HEADS_PER_GROUP // 2):
            a = jnp.broadcast_to(col[:, off + 2 * k:off + 2 * k + 1], (CHUNK, LANES))
            b = jnp.broadcast_to(col[:, off + 2 * k + 1:off + 2 * k + 2], (CHUNK, LANES))
            parts.append(jnp.where(left, a, b))
        return jnp.concatenate(parts, axis=1)

    def fwd_chunk(c, carry):
        r0 = pl.multiple_of(c * CHUNK, CHUNK)
        rows = pl.ds(r0, CHUNK)
        xq = xc_s[rows, :]
        bq = bc_s[rows, :]
        cq = cc_s[rows, :].astype(BF16)
        col = col_ref[0, rows, :]
        rowa = row_ref[c, 0]
        rowd = row_ref[c, 1]
        cb = _dot_nt(cq, bq.astype(BF16))
        ef = expand(col, 16)
        w1f = expand(col, 32)
        y_off = _dot(cq, sft_s[...].astype(BF16)) * ef
        y_parts = []
        for k in range(HEADS_PER_GROUP // 2):
            ms = []
            for r in (2 * k, 2 * k + 1):
                rb = HEADS_PER_GROUP + r
                af_c = jnp.broadcast_to(col[:, r:r + 1], (CHUNK, CHUNK))
                ab_c = jnp.broadcast_to(col[:, rb:rb + 1], (CHUNK, CHUNK))
                lf = jnp.exp(jnp.where(lower, af_c - rowa[r:r + 1, :], neg_inf)) * rowd[r:r + 1, :]
                lb = jnp.exp(jnp.where(upper, ab_c - rowa[rb:rb + 1, :], neg_inf)) * rowd[rb:rb + 1, :]
                ms.append((cb * (lf + lb)).astype(BF16))
            lhs = jnp.concatenate(ms, axis=1)
            xp = xq[:, k * LANES:(k + 1) * LANES]
            rhs = jnp.concatenate([jnp.where(left, xp, 0.0), jnp.where(left, 0.0, xp)],
                                  axis=0).astype(BF16)
            y_parts.append(_dot(lhs, rhs))
        yacc_s[rows, :] = jnp.concatenate(y_parts, axis=1) + y_off
        decay = ef[CHUNK - 1:CHUNK, :]
        sft_s[...] = sft_s[...] * decay + _dot(bq.T.astype(BF16), (xq * w1f).astype(BF16))
        return carry

    lax.fori_loop(0, nc, fwd_chunk, 0)

    def bwd_chunk(t, carry):
        c = nc - 1 - t
        r0 = pl.multiple_of(c * CHUNK, CHUNK)
        rows = pl.ds(r0, CHUNK)
        xq = xc_s[rows, :]
        bq = bc_s[rows, :]
        cq = cc_s[rows, :].astype(BF16)
        col = col_ref[0, rows, :]
        eb = expand(col, 24)
        w1b = expand(col, 40)
        y = yacc_s[rows, :] + _dot(cq, sbt_s[...].astype(BF16)) * eb + dsk_ref[...] * xq
        y = y * _silu(z_ref[rows, :].astype(F32))
        ms = jnp.mean(y * y, axis=-1, keepdims=True)
        y_ref[rows, :] = (y * lax.rsqrt(ms + EPS) * ng_ref[...]).astype(BF16)
        decay = eb[0:1, :]
        sbt_s[...] = sbt_s[...] * decay + _dot(bq.T.astype(BF16), (xq * w1b).astype(BF16))
        return carry

    lax.fori_loop(0, nc, bwd_chunk, 0)

    if emit_state:
        sf_ref[0] = sft_s[...].T
        sb_ref[0] = sbt_s[...].T


def _ssd(act, colp, rowp, conv_w, conv_b, dsk, ng, *, L, nseq, row_block0, s0f=None, s0b=None,
         y_prev=None):
    m = act.shape[0]
    has_init = s0f is not None
    emit_state = not has_init
    nc = L // CHUNK
    gw = GROUP_WIDTH
    cw = conv_w.reshape(D_CONV, CONV_WIDTH)
    cbias = conv_b.reshape(1, CONV_WIDTH)
    rb = row_block0
    in_specs = [
        pl.BlockSpec((L, gw), lambda b, g: (rb + b, COL_XBC // gw + g)),
        pl.BlockSpec((L, D_STATE), lambda b, g: (rb + b, (COL_XBC + SSD_WIDTH) // D_STATE + g)),
        pl.BlockSpec((L, D_STATE),
                     lambda b, g: (rb + b, (COL_XBC + SSD_WIDTH + BC_WIDTH) // D_STATE + g)),
        pl.BlockSpec((L, gw), lambda b, g: (rb + b, COL_ZS // gw + g)),
        pl.BlockSpec((1, L, LANES), lambda b, g: (g, rb + b, 0)),
        pl.BlockSpec((nc, 2, 16, CHUNK), lambda b, g: (rb + b, 0, g, 0)),
        pl.BlockSpec((D_CONV, gw), lambda b, g: (0, g)),
        pl.BlockSpec((D_CONV, D_STATE), lambda b, g: (0, SSD_WIDTH // D_STATE + g)),
        pl.BlockSpec((D_CONV, D_STATE), lambda b, g: (0, (SSD_WIDTH + BC_WIDTH) // D_STATE + g)),
        pl.BlockSpec((1, gw), lambda b, g: (0, g)),
        pl.BlockSpec((1, D_STATE), lambda b, g: (0, SSD_WIDTH // D_STATE + g)),
        pl.BlockSpec((1, D_STATE), lambda b, g: (0, (SSD_WIDTH + BC_WIDTH) // D_STATE + g)),
        pl.BlockSpec((1, gw), lambda b, g: (0, g)),
        pl.BlockSpec((1, gw), lambda b, g: (0, g)),
    ]
    args = [act, act, act, act, colp, rowp, cw, cw, cw, cbias, cbias, cbias, dsk, ng]
    aliases = {}
    if has_init:
        in_specs += [
            pl.BlockSpec((1, gw, D_STATE), lambda b, g: (b, g, 0)),
            pl.BlockSpec((1, gw, D_STATE), lambda b, g: (b, g, 0)),
            pl.BlockSpec(memory_space=pl.ANY),
        ]
        args += [s0f, s0b, y_prev]
        aliases = {len(args) - 1: 0}
    out_specs = [pl.BlockSpec((L, gw), lambda b, g: (rb + b, g))]
    out_shape = [jax.ShapeDtypeStruct((m, SSD_WIDTH), BF16)]
    if emit_state:
        out_specs += [pl.BlockSpec((1, gw, D_STATE), lambda b, g: (b, g, 0))] * 2
        out_shape += [jax.ShapeDtypeStruct((nseq, SSD_WIDTH, D_STATE), F32)] * 2
    scratch = [
        pltpu.VMEM((L + 2 * PAD, gw), F32),
        pltpu.VMEM((L, gw), F32),
        pltpu.VMEM((L, D_STATE), F32),
        pltpu.VMEM((L, D_STATE), F32),
        pltpu.VMEM((L, gw), F32),
        pltpu.VMEM((D_STATE, gw), F32),
        pltpu.VMEM((D_STATE, gw), F32),
    ]
    kern = functools.partial(_ssd_kernel, L=L, has_init=has_init, emit_state=emit_state)
    return pl.pallas_call(
        kern,
        grid=(nseq, N_SSD_GROUPS),
        in_specs=in_specs,
        out_specs=out_specs,
        out_shape=out_shape,
        scratch_shapes=scratch,
        input_output_aliases=aliases,
        compiler_params=_cparams(("arbitrary", "arbitrary")),
        name="ssd_latent" if has_init else "ssd_context",
    )(*args)


ATTN_SCALE = HEAD_DIM ** -0.5


def _ctx_attn_kernel(sink_ref, q_ref, k_ref, v_ref, z_ref, o_ref):
    g = pl.program_id(1)
    k = k_ref[...]
    v = v_ref[...]
    for r in range(Q_PER_KV):
        ls = slice(r * HEAD_DIM, (r + 1) * HEAD_DIM)
        sink = sink_ref[g * Q_PER_KV + r]
        s = _dot_nt(q_ref[:, ls], k) * ATTN_SCALE
        m = jnp.maximum(jnp.max(s, axis=-1, keepdims=True), sink)
        p = jnp.exp(s - m)
        denom = jnp.sum(p, axis=-1, keepdims=True) + jnp.exp(sink - m)
        o = _dot(p.astype(BF16), v) / denom
        o_ref[:, ls] = (o * _silu(z_ref[:, ls].astype(F32))).astype(BF16)


def _ctx_attention(act, sink, *, L, nseq):
    m = act.shape[0]
    gw = Q_PER_KV * HEAD_DIM
    return pl.pallas_call(
        _ctx_attn_kernel,
        grid=(nseq, N_KV_HEADS),
        in_specs=[
            pl.BlockSpec(memory_space=pltpu.SMEM),
            pl.BlockSpec((L, gw), lambda b, g: (b, COL_Q // gw + g)),
            pl.BlockSpec((L, HEAD_DIM), lambda b, g: (b, COL_K // HEAD_DIM + g)),
            pl.BlockSpec((L, HEAD_DIM), lambda b, g: (b, COL_V // HEAD_DIM + g)),
            pl.BlockSpec((L, gw), lambda b, g: (b, COL_ZA // gw + g)),
        ],
        out_specs=pl.BlockSpec((L, gw), lambda b, g: (b, g)),
        out_shape=jax.ShapeDtypeStruct((m, ATTN_WIDTH), BF16),
        compiler_params=_cparams(("arbitrary", "arbitrary")),
        name="attn_context",
    )(sink, act, act, act, act)


def _rope_tables(length):
    sec = HEAD_DIM // 2
    half = sec // 2
    d = np.arange(HEAD_DIM)
    e = d % sec
    freqs = ROPE_BASE ** (-(e % half).astype(np.float64) / half)
    t = np.arange(length)
    pos = np.where((d // sec)[None, :] == 0, (t // GRID_W)[:, None], (t % GRID_W)[:, None])
    ang = pos.astype(np.float64) * freqs[None, :]
    sign = np.where(e < half, -1.0, 1.0)[None, :]
    return (jnp.asarray(np.cos(ang), F32), jnp.asarray(np.sin(ang) * sign, F32))


def _rope(x, cos, sin_signed, first_half):
    partner = jnp.where(first_half, pltpu.roll(x, LANES - HEAD_DIM // 4, 1),
                        pltpu.roll(x, HEAD_DIM // 4, 1))
    return x * cos + partner * sin_signed


def _lat_attn_kernel(sink_ref, q_ref, k_ref, v_ref, z_ref, kc_ref, vc_ref, cos_ref, sin_ref,
                     _oprev_ref, o_ref, keys_s, vals_s, *, L, lc):
    g = pl.program_id(1)
    nb = L // CHUNK
    nk = lc + L
    lane = lax.broadcasted_iota(jnp.int32, (CHUNK, HEAD_DIM), 1)
    first_half = (lane % (HEAD_DIM // 2)) < (HEAD_DIM // 4)

    keys_s[0:lc, :] = kc_ref[...].astype(BF16)
    vals_s[0:lc, :] = vc_ref[...].astype(BF16)
    for n in range(nb):
        rows = slice(n * CHUNK, (n + 1) * CHUNK)
        kr = _rope(k_ref[rows, :].astype(F32), cos_ref[rows, :], sin_ref[rows, :], first_half)
        keys_s[lc + n * CHUNK:lc + (n + 1) * CHUNK, :] = kr.astype(BF16)
        vals_s[lc + n * CHUNK:lc + (n + 1) * CHUNK, :] = v_ref[rows, :]

    qi = lax.broadcasted_iota(jnp.int32, (CHUNK, nk), 0)
    ki = lax.broadcasted_iota(jnp.int32, (CHUNK, nk), 1)

    def block(n, carry):
        r0 = pl.multiple_of(n * CHUNK, CHUNK)
        rows = pl.ds(r0, CHUNK)
        cos = cos_ref[rows, :]
        sin = sin_ref[rows, :]
        kpos = ki - lc
        valid = (ki < lc) | (jnp.abs(qi + r0 - kpos) <= WINDOW)
        keys = keys_s[...]
        vals = vals_s[...]
        for r in range(Q_PER_KV):
            ls = slice(r * HEAD_DIM, (r + 1) * HEAD_DIM)
            sink = sink_ref[g * Q_PER_KV + r]
            q = _rope(q_ref[rows, ls].astype(F32), cos, sin, first_half).astype(BF16)
            s = jnp.where(valid, _dot_nt(q, keys) * ATTN_SCALE, -jnp.inf)
            m = jnp.maximum(jnp.max(s, axis=-1, keepdims=True), sink)
            p = jnp.exp(s - m)
            denom = jnp.sum(p, axis=-1, keepdims=True) + jnp.exp(sink - m)
            o = _dot(p.astype(BF16), vals) / denom
            o_ref[rows, ls] = (o * _silu(z_ref[rows, ls].astype(F32))).astype(BF16)
        return carry

    lax.fori_loop(0, nb, block, 0)


def _lat_attention(act, sink, cache_k, cache_v, o_prev, *, L, nseq, row_block0, layer):
    m = act.shape[0]
    gw = Q_PER_KV * HEAD_DIM
    lc = cache_k.shape[2]
    rb = row_block0
    cos, sin = _rope_tables(L)
    kc = cache_k.reshape(cache_k.shape[0], cache_k.shape[1], lc, KV_WIDTH)
    vc = cache_v.reshape(cache_v.shape[0], cache_v.shape[1], lc, KV_WIDTH)
    kern = functools.partial(_lat_attn_kernel, L=L, lc=lc)
    tab = lambda b, g: (0, 0)
    return pl.pallas_call(
        kern,
        grid=(nseq, N_KV_HEADS),
        in_specs=[
            pl.BlockSpec(memory_space=pltpu.SMEM),
            pl.BlockSpec((L, gw), lambda b, g: (rb + b, COL_Q // gw + g)),
            pl.BlockSpec((L, HEAD_DIM), lambda b, g: (rb + b, COL_K // HEAD_DIM + g)),
            pl.BlockSpec((L, HEAD_DIM), lambda b, g: (rb + b, COL_V // HEAD_DIM + g)),
            pl.BlockSpec((L, gw), lambda b, g: (rb + b, COL_ZA // gw + g)),
            pl.BlockSpec((None, None, lc, HEAD_DIM), lambda b, g: (b, layer, 0, g)),
            pl.BlockSpec((None, None, lc, HEAD_DIM), lambda b, g: (b, layer, 0, g)),
            pl.BlockSpec((L, HEAD_DIM), tab),
            pl.BlockSpec((L, HEAD_DIM), tab),
            pl.BlockSpec(memory_space=pl.ANY),
        ],
        out_specs=pl.BlockSpec((L, gw), lambda b, g: (rb + b, g)),
        out_shape=jax.ShapeDtypeStruct((m, ATTN_WIDTH), BF16),
        scratch_shapes=[pltpu.VMEM((lc + L, HEAD_DIM), BF16), pltpu.VMEM((lc + L, HEAD_DIM), BF16)],
        input_output_aliases={9: 0},
        compiler_params=_cparams(("arbitrary", "arbitrary")),
        name="attn_latent",
    )(sink, act, act, act, act, kc, vc, cos, sin, o_prev)


def _branch_kernel(oa_ref, ys_ref, wpa_ref, wps_ref, ga_ref, gs_ref, o_ref):
    a = _dot(oa_ref[...], wpa_ref[...])
    s = _dot(ys_ref[...], wps_ref[...])
    merged = _sigmoid(ga_ref[...].astype(F32)) * a + _sigmoid(gs_ref[...].astype(F32)) * s
    o_ref[...] = merged.astype(BF16)


def _branches(oa, ys, w_pa, w_ps, act, *, tm):
    m = oa.shape[0]
    tn = 512
    return pl.pallas_call(
        _branch_kernel,
        grid=(m // tm, D_MODEL // tn),
        in_specs=[
            pl.BlockSpec((tm, ATTN_WIDTH), lambda i, j: (i, 0)),
            pl.BlockSpec((tm, SSD_WIDTH), lambda i, j: (i, 0)),
            pl.BlockSpec((ATTN_WIDTH, tn), lambda i, j: (0, j)),
            pl.BlockSpec((SSD_WIDTH, tn), lambda i, j: (0, j)),
            pl.BlockSpec((tm, tn), lambda i, j: (i, COL_GA // tn + j)),
            pl.BlockSpec((tm, tn), lambda i, j: (i, COL_GS // tn + j)),
        ],
        out_specs=pl.BlockSpec((tm, tn), lambda i, j: (i, j)),
        out_shape=jax.ShapeDtypeStruct((m, D_MODEL), BF16),
        compiler_params=_cparams(("arbitrary", "arbitrary")),
        name="branches",
    )(oa, ys, w_pa, w_ps, act, act)


def _out_kernel(*refs, final):
    if final:
        mg_ref, w_ref, x_ref, gate_ref, fg_ref, o_ref = refs
    else:
        mg_ref, w_ref, x_ref, gate_ref, o_ref = refs
    y = x_ref[...] + gate_ref[...] * _dot(mg_ref[...], w_ref[...])
    if final:
        ms = jnp.mean(y * y, axis=-1, keepdims=True)
        y = y * lax.rsqrt(ms + EPS) * fg_ref[...]
    o_ref[...] = y


def _out_proj(merged, w_out, x, mod, final_g, *, tm, group_of):
    m = x.shape[0]
    final = final_g is not None
    in_specs = [
        pl.BlockSpec((tm, D_MODEL), lambda i: (i, 0)),
        pl.BlockSpec((D_MODEL, D_MODEL), lambda i: (0, 0)),
        pl.BlockSpec((tm, D_MODEL), lambda i: (i, 0)),
        pl.BlockSpec((None, 1, D_MODEL), lambda i: (group_of(i, tm), 0, 2)),
    ]
    args = [merged, w_out, x, mod]
    if final:
        in_specs.append(pl.BlockSpec((1, D_MODEL), lambda i: (0, 0)))
        args.append(final_g.reshape(1, D_MODEL))
    return pl.pallas_call(
        functools.partial(_out_kernel, final=final),
        grid=(m // tm,),
        in_specs=in_specs,
        out_specs=pl.BlockSpec((tm, D_MODEL), lambda i: (i, 0)),
        out_shape=jax.ShapeDtypeStruct((m, D_MODEL), F32),
        compiler_params=_cparams(("arbitrary",)),
        name="out_proj_final" if final else "out_proj",
    )(*args)


def _dt_permutation():
    perm = np.zeros(DT_WIDTH, np.int32)
    for g in range(N_SSD_GROUPS):
        for d in range(2):
            for r in range(HEADS_PER_GROUP):
                perm[g * 16 + d * HEADS_PER_GROUP + r] = d * N_SSD_HEADS + g * HEADS_PER_GROUP + r
    return perm


def kernel(x_prompt, x_sample, c, cache_k, cache_v, state_ssm_fwd, state_ssm_bwd, c_ctx, norm_g, w_mod, b_mod, w_in, conv_w, conv_b, attn_sink, a_log_fwd, a_log_bwd, dt_bias_fwd, dt_bias_bwd, d_skip, ssd_norm_g, w_pa, w_ps, w_out, final_norm_g):
    bc, lc, _ = x_prompt.shape
    bl, ll, _ = x_sample.shape
    depth = w_in.shape[0]
    n_ctx = bc * lc
    m = n_ctx + bl * ll
    assert n_ctx % ll == 0 and ll % lc == 0 and lc % CHUNK == 0
    assert 1 + bl <= COND_ROWS

    def group_of(i, tm):
        return jnp.maximum(i * tm - n_ctx + ll, 0) // ll

    tm_big = math.gcd(1024, math.gcd(n_ctx, ll))
    tm_small = math.gcd(512, tm_big)

    x = jnp.concatenate([x_prompt.reshape(n_ctx, D_MODEL), x_sample.reshape(bl * ll, D_MODEL)], axis=0)
    cond = jnp.zeros((COND_ROWS, D_MODEL), F32).at[0].set(c_ctx).at[1:1 + bl].set(c)
    mod_all = _modulation(cond, w_mod, b_mod)
    perm = _dt_permutation()

    new_k, new_v, new_sf, new_sb = [], [], [], []
    for l in range(depth):
        mod = mod_all[l].reshape(COND_ROWS, 1, MOD_WIDTH)
        wdt = w_in[l][:, W_IN_DT:W_IN_DT + DT_WIDTH][:, perm]
        bias = jnp.concatenate([dt_bias_fwd[l], dt_bias_bwd[l]])[perm]
        alog = jnp.concatenate([a_log_fwd[l], a_log_bwd[l]])[perm]
        h, colp, rowp = _prep(x, mod, norm_g[l], wdt, wdt.T, bias, alog, tm=tm_small,
                              group_of=group_of)
        w_main = jnp.concatenate([w_in[l][:, :W_IN_DT], w_in[l][:, W_IN_DT + DT_WIDTH:]],
                                 axis=1).astype(BF16)
        act, kv = _inproj(h, w_main, tm=tm_big)

        sink = attn_sink[l]
        oa = _ctx_attention(act, sink, L=lc, nseq=bc)
        oa = _lat_attention(act, sink, cache_k, cache_v, oa, L=ll, nseq=bl,
                            row_block0=n_ctx // ll, layer=l)

        dsk = jnp.repeat(d_skip[l], SSD_HEADDIM).reshape(1, SSD_WIDTH)
        ng = ssd_norm_g[l].reshape(1, SSD_WIDTH)
        ys, sf, sb = _ssd(act, colp, rowp, conv_w[l], conv_b[l], dsk, ng, L=lc, nseq=bc,
                          row_block0=0)
        (ys,) = _ssd(act, colp, rowp, conv_w[l], conv_b[l], dsk, ng, L=ll, nseq=bl,
                     row_block0=n_ctx // ll,
                     s0f=state_ssm_fwd[:, l].reshape(bl, SSD_WIDTH, D_STATE),
                     s0b=state_ssm_bwd[:, l].reshape(bl, SSD_WIDTH, D_STATE), y_prev=ys)

        merged = _branches(oa, ys, w_pa[l].astype(BF16), w_ps[l].astype(BF16), act, tm=tm_big)
        x = _out_proj(merged, w_out[l].astype(BF16), x, mod,
                      final_norm_g if l == depth - 1 else None, tm=tm_small, group_of=group_of)

        new_k.append(kv[:n_ctx, :KV_WIDTH].reshape(bc, lc, N_KV_HEADS, HEAD_DIM))
        new_v.append(kv[:n_ctx, KV_WIDTH:].reshape(bc, lc, N_KV_HEADS, HEAD_DIM))
        new_sf.append(sf.reshape(bc, N_SSD_HEADS, SSD_HEADDIM, D_STATE))
        new_sb.append(sb.reshape(bc, N_SSD_HEADS, SSD_HEADDIM, D_STATE))

    y_prompt = x[:n_ctx].reshape(bc, lc, D_MODEL)
    y_sample = x[n_ctx:].reshape(bl, ll, D_MODEL)
    return (y_prompt, y_sample, jnp.stack(new_k, axis=1), jnp.stack(new_v, axis=1),
            jnp.stack(new_sf, axis=1), jnp.stack(new_sb, axis=1))
```

```python
import functools
import math

import numpy as np
import jax
import jax.numpy as jnp
from jax import lax
from jax.experimental import pallas as pl
from jax.experimental.pallas import tpu as pltpu

F32 = jnp.float32
BF16 = jnp.bfloat16

D_MODEL = 2048
HEAD_DIM = 128
N_Q_HEADS = 16
N_KV_HEADS = 4
Q_PER_KV = 4
ATTN_WIDTH = 2048
KV_WIDTH = 512
WINDOW = 128
GRID_W = 64
ROPE_BASE = 10000.0
SSD_WIDTH = 4096
SSD_HEADDIM = 64
N_SSD_HEADS = 64
D_STATE = 128
N_SSD_GROUPS = 8
HEADS_PER_GROUP = 8
GROUP_WIDTH = SSD_WIDTH // N_SSD_GROUPS
CHUNK = 128
D_CONV = 5
BC_WIDTH = 1024
CONV_WIDTH = 6144
EPS = 1e-6
MOD_WIDTH = 3 * D_MODEL

COL_Q = 0
COL_K = 2048
COL_V = 2560
COL_ZA = 3072
COL_XBC = 5120
COL_ZS = 11264
COL_GA = 15360
COL_GS = 17408
ACT_WIDTH = 19456
W_IN_DT = 15360
DT_WIDTH = 2 * N_SSD_HEADS

LANES = 128
SUBLANES = 8
VMEM_LIMIT = 56 * 1024 * 1024

COND_ROWS = 8


def _cparams(sem):
    return pltpu.CompilerParams(dimension_semantics=sem, vmem_limit_bytes=VMEM_LIMIT)


def _dot(a, b):
    return jnp.dot(a, b, preferred_element_type=F32)


def _dot_nt(a, b):
    return lax.dot_general(a, b, (((1,), (1,)), ((), ())), preferred_element_type=F32)


def _split2(x):
    hi = x.astype(BF16)
    lo = (x - hi.astype(F32)).astype(BF16)
    return hi, lo


def _split3(x):
    p1 = x.astype(BF16)
    r1 = x - p1.astype(F32)
    p2 = r1.astype(BF16)
    p3 = (r1 - p2.astype(F32)).astype(BF16)
    return p1, p2, p3


def _dot3(a, b):
    ah, al = _split2(a)
    bh, bl = _split2(b)
    return _dot(ah, bh) + _dot(al, bh) + _dot(ah, bl)


def _sigmoid(x):
    return 1.0 / (1.0 + jnp.exp(-x))


def _silu(x):
    return x * _sigmoid(x)


def _softplus(x):
    return jnp.maximum(x, 0.0) + jnp.log1p(jnp.exp(-jnp.abs(x)))


def _mod_kernel(cond_ref, w_ref, b_ref, o_ref):
    res = _dot3(_silu(cond_ref[...]), w_ref[...]) + b_ref[...]
    for r in range(COND_ROWS):
        o_ref[r] = res[r:r + 1, :]


def _modulation(cond, w_mod, b_mod):
    depth = w_mod.shape[0]
    tn = 512
    return pl.pallas_call(
        _mod_kernel,
        grid=(depth, MOD_WIDTH // tn),
        in_specs=[
            pl.BlockSpec((COND_ROWS, D_MODEL), lambda l, j: (0, 0)),
            pl.BlockSpec((None, D_MODEL, tn), lambda l, j: (l, 0, j)),
            pl.BlockSpec((None, 1, tn), lambda l, j: (l, 0, j)),
        ],
        out_specs=pl.BlockSpec((None, COND_ROWS, 1, tn), lambda l, j: (l, 0, 0, j)),
        out_shape=jax.ShapeDtypeStruct((depth, COND_ROWS, 1, MOD_WIDTH), F32),
        compiler_params=_cparams(("arbitrary", "arbitrary")),
        name="modulation",
    )(cond, w_mod, b_mod.reshape(depth, 1, MOD_WIDTH))


def _prep_kernel(xa_ref, xb_ref, shift_ref, scale_ref, g_ref, wdt_ref, wdtt_ref, bias_ref, biast_ref,
                 alog_ref, alogt_ref, h_ref, col_ref, row_ref, *, tm, na):
    x = jnp.where(pl.program_id(0) < na, xa_ref[...], xb_ref[...])
    ms = jnp.mean(x * x, axis=-1, keepdims=True)
    h = (x * lax.rsqrt(ms + EPS) * g_ref[...]) * (1.0 + scale_ref[...]) + shift_ref[...]
    h_ref[...] = h.astype(BF16)

    hh, hl = _split2(h)
    wh, wl = _split2(wdt_ref[...])
    raw = _dot(hh, wh) + _dot(hl, wh) + _dot(hh, wl)
    wth, wtl = _split2(wdtt_ref[...])
    rawt = _dot_nt(wth, hh) + _dot_nt(wth, hl) + _dot_nt(wtl, hh)

    dt = _softplus(raw + bias_ref[...])
    dta = dt * (-jnp.exp(alog_ref[...]))
    dtt = _softplus(rawt + biast_ref[...])
    dtat = dtt * (-jnp.exp(alogt_ref[...]))

    ii = lax.broadcasted_iota(jnp.int32, (CHUNK, CHUNK), 0)
    kk = lax.broadcasted_iota(jnp.int32, (CHUNK, CHUNK), 1)
    lt = jnp.where(kk <= ii, 1.0, 0.0).astype(BF16)
    ut = jnp.where(kk >= ii, 1.0, 0.0).astype(BF16)
    fwd_lane = (kk % 16) < HEADS_PER_GROUP
    fwd_row = (ii % 16) < HEADS_PER_GROUP

    for c in range(tm // CHUNK):
        rows = slice(c * CHUNK, (c + 1) * CHUNK)
        p1, p2, p3 = _split3(dta[rows, :])
        pre = _dot(lt, p1) + _dot(lt, p2) + _dot(lt, p3)
        suf = _dot(ut, p1) + _dot(ut, p2) + _dot(ut, p3)
        acs = jnp.where(fwd_lane, pre, suf)
        edge = jnp.where(fwd_lane, acs[CHUNK - 1:CHUNK, :], acs[0:1, :])
        e = jnp.exp(acs)
        w1 = dt[rows, :] * jnp.exp(edge - acs)
        for g in range(N_SSD_GROUPS):
            r0 = pltpu.roll(acs, (0 - 16 * g) % LANES, 1)
            r1 = pltpu.roll(e, (16 - 16 * g) % LANES, 1)
            r2 = pltpu.roll(w1, (32 - 16 * g) % LANES, 1)
            packed = jnp.where(kk < 16, r0, jnp.where(kk < 32, r1, jnp.where(kk < 48, r2, 0.0)))
            col_ref[g, rows, :] = packed
        q1, q2, q3 = _split3(dtat[:, rows])
        pre_t = _dot(q1, ut) + _dot(q2, ut) + _dot(q3, ut)
        suf_t = _dot(q1, lt) + _dot(q2, lt) + _dot(q3, lt)
        row_ref[c, 0] = jnp.where(fwd_row, pre_t, suf_t)
        row_ref[c, 1] = dtt[:, rows]


def _row_split_specs(tm, na, xb_offset):
    return [
        pl.BlockSpec((tm, D_MODEL), lambda i: (jnp.minimum(i, na - 1), 0)),
        pl.BlockSpec((tm, D_MODEL), lambda i: (xb_offset + jnp.maximum(i - na, 0), 0)),
    ]


def _prep(xa, xb, xb_offset, m, mod, norm_g, wdt, wdtt, bias, alog, *, layer, tm, na, group_of):
    l = layer
    kern = functools.partial(_prep_kernel, tm=tm, na=na)
    par2 = lambda i: (l, 0, 0)
    return pl.pallas_call(
        kern,
        grid=(m // tm,),
        in_specs=_row_split_specs(tm, na, xb_offset) + [
            pl.BlockSpec((None, None, 1, D_MODEL), lambda i: (l, group_of(i, tm), 0, 0)),
            pl.BlockSpec((None, None, 1, D_MODEL), lambda i: (l, group_of(i, tm), 0, 1)),
            pl.BlockSpec((None, 1, D_MODEL), par2),
            pl.BlockSpec((None, D_MODEL, DT_WIDTH), par2),
            pl.BlockSpec((None, DT_WIDTH, D_MODEL), par2),
            pl.BlockSpec((None, 1, DT_WIDTH), par2),
            pl.BlockSpec((None, DT_WIDTH, 1), par2),
            pl.BlockSpec((None, 1, DT_WIDTH), par2),
            pl.BlockSpec((None, DT_WIDTH, 1), par2),
        ],
        out_specs=[
            pl.BlockSpec((tm, D_MODEL), lambda i: (i, 0)),
            pl.BlockSpec((N_SSD_GROUPS, tm, LANES), lambda i: (0, i, 0)),
            pl.BlockSpec((tm // CHUNK, 2, DT_WIDTH, CHUNK), lambda i: (i, 0, 0, 0)),
        ],
        out_shape=[
            jax.ShapeDtypeStruct((m, D_MODEL), BF16),
            jax.ShapeDtypeStruct((N_SSD_GROUPS, m, LANES), F32),
            jax.ShapeDtypeStruct((m // CHUNK, 2, DT_WIDTH, CHUNK), F32),
        ],
        compiler_params=_cparams(("arbitrary",)),
        name="prep",
    )(xa, xb, mod, mod, norm_g, wdt, wdtt, bias[:, None, :], bias[:, :, None], alog[:, None, :],
      alog[:, :, None])


INPROJ_TN = 1024
KV_BLOCK = COL_K // INPROJ_TN


def _inproj_kernel(*refs, na, lc, aliased):
    if aliased:
        h_ref, w_ref, _kprev, _vprev, act_ref, k_ref, v_ref, wbf_s = refs
    else:
        h_ref, w_ref, act_ref, k_ref, v_ref, wbf_s = refs
    j = pl.program_id(0)
    i = pl.program_id(1)

    @pl.when(i == 0)
    def _():
        wbf_s[...] = w_ref[...].astype(BF16)

    acc = _dot(h_ref[...], wbf_s[...])
    act_ref[...] = acc.astype(BF16)

    @pl.when((j == KV_BLOCK) & (i < na))
    def _():
        for s in range(k_ref.shape[0]):
            k_ref[s] = acc[s * lc:(s + 1) * lc, :KV_WIDTH]
            v_ref[s] = acc[s * lc:(s + 1) * lc, KV_WIDTH:]


def _inproj(h, w_in, k_prev, v_prev, *, layer, tm, na, lc, bc):
    m = h.shape[0]
    depth = w_in.shape[0]
    tn = INPROJ_TN
    l = layer
    spb = tm // lc
    aliased = k_prev is not None

    def w_col(j):
        skip = jnp.where(j >= W_IN_DT // tn, DT_WIDTH // LANES, 0)
        return (j * (tn // LANES) + skip) * LANES

    def kv_idx(j, i):
        return jnp.where(j < KV_BLOCK, 0, jnp.where(j == KV_BLOCK, jnp.minimum(i, na - 1), na - 1))

    in_specs = [
        pl.BlockSpec((tm, D_MODEL), lambda j, i: (i, 0)),
        pl.BlockSpec((None, pl.Element(D_MODEL), pl.Element(tn)), lambda j, i: (l, 0, w_col(j))),
    ]
    args = [h, w_in]
    aliases = {}
    if aliased:
        in_specs += [pl.BlockSpec(memory_space=pl.ANY)] * 2
        args += [k_prev, v_prev]
        aliases = {2: 1, 3: 2}
    kv_spec = pl.BlockSpec((spb, None, lc, KV_WIDTH), lambda j, i: (kv_idx(j, i), l, 0, 0))
    return pl.pallas_call(
        functools.partial(_inproj_kernel, na=na, lc=lc, aliased=aliased),
        grid=(ACT_WIDTH // tn, m // tm),
        in_specs=in_specs,
        out_specs=[pl.BlockSpec((tm, tn), lambda j, i: (i, j)), kv_spec, kv_spec],
        out_shape=[
            jax.ShapeDtypeStruct((m, ACT_WIDTH), BF16),
            jax.ShapeDtypeStruct((bc, depth, lc, KV_WIDTH), F32),
            jax.ShapeDtypeStruct((bc, depth, lc, KV_WIDTH), F32),
        ],
        scratch_shapes=[pltpu.VMEM((D_MODEL, tn), BF16)],
        input_output_aliases=aliases,
        compiler_params=_cparams(("arbitrary", "arbitrary")),
        name="inproj",
    )(*args)


PAD = SUBLANES


def _ssd_kernel(*refs, L, has_init, emit_state, n_alias):
    refs = list(refs)
    (x_ref, b_ref, c_ref, z_ref, col_ref, row_ref, cwx_ref, cwb_ref, cwc_ref,
     cbx_ref, cbb_ref, cbc_ref, dsk_ref, ng_ref) = refs[:14]
    pos = 14
    if has_init:
        s0f_ref, s0b_ref = refs[pos:pos + 2]
        pos += 2
    pos += n_alias
    y_ref = refs[pos]
    pos += 1
    if emit_state:
        sf_ref, sb_ref = refs[pos:pos + 2]
        pos += 2
    pad_s, xc_s, bc_s, cc_s, yacc_s, sft_s, sbt_s = refs[pos:]
    nc = L // CHUNK

    def conv_part(src_ref, w_ref, bias_ref, dst_s, width):
        zeros = jnp.zeros((PAD, width), F32)
        pad_s[0:PAD, 0:width] = zeros
        pad_s[L + PAD:L + 2 * PAD, 0:width] = zeros
        for c in range(nc):
            pad_s[PAD + c * CHUNK:PAD + (c + 1) * CHUNK, 0:width] = (
                src_ref[c * CHUNK:(c + 1) * CHUNK, :].astype(F32))
        for c in range(nc):
            for s in range(width // LANES):
                ls = slice(s * LANES, (s + 1) * LANES)
                acc = jnp.broadcast_to(bias_ref[:, ls], (CHUNK, LANES))
                for k in range(D_CONV):
                    start = PAD + c * CHUNK + k - D_CONV // 2
                    acc = acc + w_ref[k:k + 1, ls] * pad_s[start:start + CHUNK, ls]
                dst_s[c * CHUNK:(c + 1) * CHUNK, ls] = _silu(acc)

    conv_part(x_ref, cwx_ref, cbx_ref, xc_s, GROUP_WIDTH)
    conv_part(b_ref, cwb_ref, cbb_ref, bc_s, D_STATE)
    conv_part(c_ref, cwc_ref, cbc_ref, cc_s, D_STATE)

    if has_init:
        sft_s[...] = s0f_ref[0].T
        sbt_s[...] = s0b_ref[0].T
    else:
        sft_s[...] = jnp.zeros_like(sft_s)
        sbt_s[...] = jnp.zeros_like(sbt_s)

    ii = lax.broadcasted_iota(jnp.int32, (CHUNK, CHUNK), 0)
    jj = lax.broadcasted_iota(jnp.int32, (CHUNK, CHUNK), 1)
    lower = jj <= ii
    upper = jj >= ii
    left = jj < SSD_HEADDIM
    neg_inf = -jnp.inf

    def expand(col, off):
        parts = []
        for k in range(HEADS_PER_GROUP // 2):
            a = jnp.broadcast_to(col[:, off + 2 * k:off + 2 * k + 1], (CHUNK, LANES))
            b = jnp.broadcast_to(col[:, off + 2 * k + 1:off + 2 * k + 2], (CHUNK, LANES))
            parts.append(jnp.where(left, a, b))
        return jnp.concatenate(parts, axis=1)

    def fwd_chunk(c, carry):
        r0 = pl.multiple_of(c * CHUNK, CHUNK)
        rows = pl.ds(r0, CHUNK)
        xq = xc_s[rows, :]
        bq = bc_s[rows, :]
        cq = cc_s[rows, :].astype(BF16)
        col = col_ref[0, rows, :]
        rowa = row_ref[c, 0]
        rowd = row_ref[c, 1]
        cb = _dot_nt(cq, bq.astype(BF16))
        ef = expand(col, 16)
        w1f = expand(col, 32)
        y_off = _dot(cq, sft_s[...].astype(BF16)) * ef
        y_parts = []
        for k in range(HEADS_PER_GROUP // 2):
            ms = []
            for r in (2 * k, 2 * k + 1):
                rb = HEADS_PER_GROUP + r
                af_c = jnp.broadcast_to(col[:, r:r + 1], (CHUNK, CHUNK))
                ab_c = jnp.broadcast_to(col[:, rb:rb + 1], (CHUNK, CHUNK))
                lf = jnp.exp(jnp.where(lower, af_c - rowa[r:r + 1, :], neg_inf)) * rowd[r:r + 1, :]
                lb = jnp.exp(jnp.where(upper, ab_c - rowa[rb:rb + 1, :], neg_inf)) * rowd[rb:rb + 1, :]
                ms.append((cb * (lf + lb)).astype(BF16))
            lhs = jnp.concatenate(ms, axis=1)
            xp = xq[:, k * LANES:(k + 1) * LANES]
            rhs = jnp.concatenate([jnp.where(left, xp, 0.0), jnp.where(left, 0.0, xp)],
                                  axis=0).astype(BF16)
            y_parts.append(_dot(lhs, rhs))
        yacc_s[rows, :] = jnp.concatenate(y_parts, axis=1) + y_off
        decay = ef[CHUNK - 1:CHUNK, :]
        sft_s[...] = sft_s[...] * decay + _dot(bq.T.astype(BF16), (xq * w1f).astype(BF16))
        return carry

    lax.fori_loop(0, nc, fwd_chunk, 0)

    def bwd_chunk(t, carry):
        c = nc - 1 - t
        r0 = pl.multiple_of(c * CHUNK, CHUNK)
        rows = pl.ds(r0, CHUNK)
        xq = xc_s[rows, :]
        bq = bc_s[rows, :]
        cq = cc_s[rows, :].astype(BF16)
        col = col_ref[0, rows, :]
        eb = expand(col, 24)
        w1b = expand(col, 40)
        y = yacc_s[rows, :] + _dot(cq, sbt_s[...].astype(BF16)) * eb + dsk_ref[...] * xq
        y = y * _silu(z_ref[rows, :].astype(F32))
        ms = jnp.mean(y * y, axis=-1, keepdims=True)
        y_ref[rows, :] = (y * lax.rsqrt(ms + EPS) * ng_ref[...]).astype(BF16)
        decay = eb[0:1, :]
        sbt_s[...] = sbt_s[...] * decay + _dot(bq.T.astype(BF16), (xq * w1b).astype(BF16))
        return carry

    lax.fori_loop(0, nc, bwd_chunk, 0)

    if emit_state:
        sf_ref[0] = sft_s[...].T
        sb_ref[0] = sbt_s[...].T


def _ssd(act, colp, rowp, conv_w, conv_b, dsk, ng, *, layer, L, nseq, row_block0, s0f=None,
         s0b=None, y_prev=None, state_prev=None):
    m = act.shape[0]
    depth = conv_w.shape[0]
    l = layer
    has_init = s0f is not None
    emit_state = not has_init
    nc = L // CHUNK
    gw = GROUP_WIDTH
    rb = row_block0
    off_b = SSD_WIDTH // D_STATE
    off_c = (SSD_WIDTH + BC_WIDTH) // D_STATE
    in_specs = [
        pl.BlockSpec((L, gw), lambda b, g: (rb + b, COL_XBC // gw + g)),
        pl.BlockSpec((L, D_STATE), lambda b, g: (rb + b, COL_XBC // D_STATE + off_b + g)),
        pl.BlockSpec((L, D_STATE), lambda b, g: (rb + b, COL_XBC // D_STATE + off_c + g)),
        pl.BlockSpec((L, gw), lambda b, g: (rb + b, COL_ZS // gw + g)),
        pl.BlockSpec((1, L, LANES), lambda b, g: (g, rb + b, 0)),
        pl.BlockSpec((nc, 2, 16, CHUNK), lambda b, g: (rb + b, 0, g, 0)),
        pl.BlockSpec((None, D_CONV, gw), lambda b, g: (l, 0, g)),
        pl.BlockSpec((None, D_CONV, D_STATE), lambda b, g: (l, 0, off_b + g)),
        pl.BlockSpec((None, D_CONV, D_STATE), lambda b, g: (l, 0, off_c + g)),
        pl.BlockSpec((None, 1, gw), lambda b, g: (l, 0, g)),
        pl.BlockSpec((None, 1, D_STATE), lambda b, g: (l, 0, off_b + g)),
        pl.BlockSpec((None, 1, D_STATE), lambda b, g: (l, 0, off_c + g)),
        pl.BlockSpec((None, 1, gw), lambda b, g: (l, 0, g)),
        pl.BlockSpec((None, 1, gw), lambda b, g: (l, 0, g)),
    ]
    args = [act, act, act, act, colp, rowp, conv_w, conv_w, conv_w, conv_b, conv_b, conv_b, dsk, ng]
    aliases = {}
    n_alias = 0
    state_spec = pl.BlockSpec((1, None, gw, D_STATE), lambda b, g: (b, l, g, 0))
    if has_init:
        in_specs += [state_spec, state_spec, pl.BlockSpec(memory_space=pl.ANY)]
        args += [s0f, s0b, y_prev]
        aliases = {len(args) - 1: 0}
        n_alias = 1
    elif state_prev is not None:
        in_specs += [pl.BlockSpec(memory_space=pl.ANY)] * 2
        args += list(state_prev)
        aliases = {len(args) - 2: 1, len(args) - 1: 2}
        n_alias = 2
    out_specs = [pl.BlockSpec((L, gw), lambda b, g: (rb + b, g))]
    out_shape = [jax.ShapeDtypeStruct((m, SSD_WIDTH), BF16)]
    if emit_state:
        out_specs += [state_spec] * 2
        out_shape += [jax.ShapeDtypeStruct((nseq, depth, SSD_WIDTH, D_STATE), F32)] * 2
    scratch = [
        pltpu.VMEM((L + 2 * PAD, gw), F32),
        pltpu.VMEM((L, gw), F32),
        pltpu.VMEM((L, D_STATE), F32),
        pltpu.VMEM((L, D_STATE), F32),
        pltpu.VMEM((L, gw), F32),
        pltpu.VMEM((D_STATE, gw), F32),
        pltpu.VMEM((D_STATE, gw), F32),
    ]
    kern = functools.partial(_ssd_kernel, L=L, has_init=has_init, emit_state=emit_state,
                             n_alias=n_alias)
    return pl.pallas_call(
        kern,
        grid=(nseq, N_SSD_GROUPS),
        in_specs=in_specs,
        out_specs=out_specs,
        out_shape=out_shape,
        scratch_shapes=scratch,
        input_output_aliases=aliases,
        compiler_params=_cparams(("arbitrary", "arbitrary")),
        name="ssd_latent" if has_init else "ssd_context",
    )(*args)


ATTN_SCALE = HEAD_DIM ** -0.5


def _ctx_attn_kernel(sink_ref, q_ref, k_ref, v_ref, z_ref, o_ref):
    g = pl.program_id(1)
    k = k_ref[...]
    v = v_ref[...]
    for r in range(Q_PER_KV):
        ls = slice(r * HEAD_DIM, (r + 1) * HEAD_DIM)
        sink = sink_ref[g * Q_PER_KV + r]
        s = _dot_nt(q_ref[:, ls], k) * ATTN_SCALE
        m = jnp.maximum(jnp.max(s, axis=-1, keepdims=True), sink)
        p = jnp.exp(s - m)
        denom = jnp.sum(p, axis=-1, keepdims=True) + jnp.exp(sink - m)
        o = _dot(p.astype(BF16), v) / denom
        o_ref[:, ls] = (o * _silu(z_ref[:, ls].astype(F32))).astype(BF16)


def _ctx_attention(act, sink, *, L, nseq):
    m = act.shape[0]
    gw = Q_PER_KV * HEAD_DIM
    return pl.pallas_call(
        _ctx_attn_kernel,
        grid=(nseq, N_KV_HEADS),
        in_specs=[
            pl.BlockSpec(memory_space=pltpu.SMEM),
            pl.BlockSpec((L, gw), lambda b, g: (b, COL_Q // gw + g)),
            pl.BlockSpec((L, HEAD_DIM), lambda b, g: (b, COL_K // HEAD_DIM + g)),
            pl.BlockSpec((L, HEAD_DIM), lambda b, g: (b, COL_V // HEAD_DIM + g)),
            pl.BlockSpec((L, gw), lambda b, g: (b, COL_ZA // gw + g)),
        ],
        out_specs=pl.BlockSpec((L, gw), lambda b, g: (b, g)),
        out_shape=jax.ShapeDtypeStruct((m, ATTN_WIDTH), BF16),
        compiler_params=_cparams(("arbitrary", "arbitrary")),
        name="attn_context",
    )(sink, act, act, act, act)


def _rope_tables(length):
    sec = HEAD_DIM // 2
    half = sec // 2
    d = np.arange(HEAD_DIM)
    e = d % sec
    freqs = ROPE_BASE ** (-(e % half).astype(np.float64) / half)
    t = np.arange(length)
    pos = np.where((d // sec)[None, :] == 0, (t // GRID_W)[:, None], (t % GRID_W)[:, None])
    ang = pos.astype(np.float64) * freqs[None, :]
    sign = np.where(e < half, -1.0, 1.0)[None, :]
    return (jnp.asarray(np.cos(ang), F32), jnp.asarray(np.sin(ang) * sign, F32))


def _rope(x, cos, sin_signed, first_half):
    partner = jnp.where(first_half, pltpu.roll(x, LANES - HEAD_DIM // 4, 1),
                        pltpu.roll(x, HEAD_DIM // 4, 1))
    return x * cos + partner * sin_signed


def _lat_attn_kernel(sink_ref, q_ref, k_ref, v_ref, z_ref, kc_ref, vc_ref, cos_ref, sin_ref,
                     _oprev_ref, o_ref, keys_s, vals_s, *, L, lc):
    g = pl.program_id(1)
    nb = L // CHUNK
    nk = lc + L
    lane = lax.broadcasted_iota(jnp.int32, (CHUNK, HEAD_DIM), 1)
    first_half = (lane % (HEAD_DIM // 2)) < (HEAD_DIM // 4)

    keys_s[0:lc, :] = kc_ref[...].astype(BF16)
    vals_s[0:lc, :] = vc_ref[...].astype(BF16)
    for n in range(nb):
        rows = slice(n * CHUNK, (n + 1) * CHUNK)
        kr = _rope(k_ref[rows, :].astype(F32), cos_ref[rows, :], sin_ref[rows, :], first_half)
        keys_s[lc + n * CHUNK:lc + (n + 1) * CHUNK, :] = kr.astype(BF16)
        vals_s[lc + n * CHUNK:lc + (n + 1) * CHUNK, :] = v_ref[rows, :]

    qi = lax.broadcasted_iota(jnp.int32, (CHUNK, nk), 0)
    ki = lax.broadcasted_iota(jnp.int32, (CHUNK, nk), 1)

    def block(n, carry):
        r0 = pl.multiple_of(n * CHUNK, CHUNK)
        rows = pl.ds(r0, CHUNK)
        cos = cos_ref[rows, :]
        sin = sin_ref[rows, :]
        kpos = ki - lc
        valid = (ki < lc) | (jnp.abs(qi + r0 - kpos) <= WINDOW)
        keys = keys_s[...]
        vals = vals_s[...]
        for r in range(Q_PER_KV):
            ls = slice(r * HEAD_DIM, (r + 1) * HEAD_DIM)
            sink = sink_ref[g * Q_PER_KV + r]
            q = _rope(q_ref[rows, ls].astype(F32), cos, sin, first_half).astype(BF16)
            s = jnp.where(valid, _dot_nt(q, keys) * ATTN_SCALE, -jnp.inf)
            m = jnp.maximum(jnp.max(s, axis=-1, keepdims=True), sink)
            p = jnp.exp(s - m)
            denom = jnp.sum(p, axis=-1, keepdims=True) + jnp.exp(sink - m)
            o = _dot(p.astype(BF16), vals) / denom
            o_ref[rows, ls] = (o * _silu(z_ref[rows, ls].astype(F32))).astype(BF16)
        return carry

    lax.fori_loop(0, nb, block, 0)


def _lat_attention(act, sink, cache_k, cache_v, o_prev, *, L, nseq, row_block0, layer):
    m = act.shape[0]
    gw = Q_PER_KV * HEAD_DIM
    lc = cache_k.shape[2]
    rb = row_block0
    cos, sin = _rope_tables(L)
    kc = cache_k.reshape(cache_k.shape[0], cache_k.shape[1], lc, KV_WIDTH)
    vc = cache_v.reshape(cache_v.shape[0], cache_v.shape[1], lc, KV_WIDTH)
    kern = functools.partial(_lat_attn_kernel, L=L, lc=lc)
    tab = lambda b, g: (0, 0)
    return pl.pallas_call(
        kern,
        grid=(nseq, N_KV_HEADS),
        in_specs=[
            pl.BlockSpec(memory_space=pltpu.SMEM),
            pl.BlockSpec((L, gw), lambda b, g: (rb + b, COL_Q // gw + g)),
            pl.BlockSpec((L, HEAD_DIM), lambda b, g: (rb + b, COL_K // HEAD_DIM + g)),
            pl.BlockSpec((L, HEAD_DIM), lambda b, g: (rb + b, COL_V // HEAD_DIM + g)),
            pl.BlockSpec((L, gw), lambda b, g: (rb + b, COL_ZA // gw + g)),
            pl.BlockSpec((None, None, lc, HEAD_DIM), lambda b, g: (b, layer, 0, g)),
            pl.BlockSpec((None, None, lc, HEAD_DIM), lambda b, g: (b, layer, 0, g)),
            pl.BlockSpec((L, HEAD_DIM), tab),
            pl.BlockSpec((L, HEAD_DIM), tab),
            pl.BlockSpec(memory_space=pl.ANY),
        ],
        out_specs=pl.BlockSpec((L, gw), lambda b, g: (rb + b, g)),
        out_shape=jax.ShapeDtypeStruct((m, ATTN_WIDTH), BF16),
        scratch_shapes=[pltpu.VMEM((lc + L, HEAD_DIM), BF16), pltpu.VMEM((lc + L, HEAD_DIM), BF16)],
        input_output_aliases={9: 0},
        compiler_params=_cparams(("arbitrary", "arbitrary")),
        name="attn_latent",
    )(sink, act, act, act, act, kc, vc, cos, sin, o_prev)


def _branch_kernel(oa_ref, ys_ref, wpa_ref, wps_ref, ga_ref, gs_ref, o_ref):
    a = _dot(oa_ref[...], wpa_ref[...])
    s = _dot(ys_ref[...], wps_ref[...])
    merged = _sigmoid(ga_ref[...].astype(F32)) * a + _sigmoid(gs_ref[...].astype(F32)) * s
    o_ref[...] = merged.astype(BF16)


def _branches(oa, ys, w_pa, w_ps, act, *, layer, tm):
    m = oa.shape[0]
    tn = 512
    l = layer
    return pl.pallas_call(
        _branch_kernel,
        grid=(m // tm, D_MODEL // tn),
        in_specs=[
            pl.BlockSpec((tm, ATTN_WIDTH), lambda i, j: (i, 0)),
            pl.BlockSpec((tm, SSD_WIDTH), lambda i, j: (i, 0)),
            pl.BlockSpec((None, ATTN_WIDTH, tn), lambda i, j: (l, 0, j)),
            pl.BlockSpec((None, SSD_WIDTH, tn), lambda i, j: (l, 0, j)),
            pl.BlockSpec((tm, tn), lambda i, j: (i, COL_GA // tn + j)),
            pl.BlockSpec((tm, tn), lambda i, j: (i, COL_GS // tn + j)),
        ],
        out_specs=pl.BlockSpec((tm, tn), lambda i, j: (i, j)),
        out_shape=jax.ShapeDtypeStruct((m, D_MODEL), BF16),
        compiler_params=_cparams(("arbitrary", "arbitrary")),
        name="branches",
    )(oa, ys, w_pa, w_ps, act, act)


def _out_kernel(*refs, final, na):
    if final:
        mg_ref, w_ref, xa_ref, xb_ref, gate_ref, fg_ref, ya_ref, yb_ref = refs
    else:
        mg_ref, w_ref, xa_ref, xb_ref, gate_ref, o_ref = refs
    is_ctx = pl.program_id(0) < na
    x = jnp.where(is_ctx, xa_ref[...], xb_ref[...])
    y = x + gate_ref[...] * _dot(mg_ref[...], w_ref[...])
    if not final:
        o_ref[...] = y
        return
    ms = jnp.mean(y * y, axis=-1, keepdims=True)
    y = y * lax.rsqrt(ms + EPS) * fg_ref[...]

    @pl.when(is_ctx)
    def _():
        ya_ref[...] = y

    @pl.when(jnp.logical_not(is_ctx))
    def _():
        yb_ref[...] = y


def _out_proj(merged, w_out, xa, xb, xb_offset, mod, final_g, *, layer, tm, na, group_of):
    m = merged.shape[0]
    l = layer
    final = final_g is not None
    in_specs = [
        pl.BlockSpec((tm, D_MODEL), lambda i: (i, 0)),
        pl.BlockSpec((None, D_MODEL, D_MODEL), lambda i: (l, 0, 0)),
    ] + _row_split_specs(tm, na, xb_offset) + [
        pl.BlockSpec((None, None, 1, D_MODEL), lambda i: (l, group_of(i, tm), 0, 2)),
    ]
    args = [merged, w_out, xa, xb, mod]
    if final:
        in_specs.append(pl.BlockSpec((1, D_MODEL), lambda i: (0, 0)))
        args.append(final_g.reshape(1, D_MODEL))
        out_specs = [
            pl.BlockSpec((tm, D_MODEL), lambda i: (jnp.minimum(i, na - 1), 0)),
            pl.BlockSpec((tm, D_MODEL), lambda i: (jnp.maximum(i - na, 0), 0)),
        ]
        out_shape = [jax.ShapeDtypeStruct((na * tm, D_MODEL), F32),
                     jax.ShapeDtypeStruct((m - na * tm, D_MODEL), F32)]
    else:
        out_specs = pl.BlockSpec((tm, D_MODEL), lambda i: (i, 0))
        out_shape = jax.ShapeDtypeStruct((m, D_MODEL), F32)
    return pl.pallas_call(
        functools.partial(_out_kernel, final=final, na=na),
        grid=(m // tm,),
        in_specs=in_specs,
        out_specs=out_specs,
        out_shape=out_shape,
        compiler_params=_cparams(("arbitrary",)),
        name="out_proj_final" if final else "out_proj",
    )(*args)


def _dt_permutation():
    perm = np.zeros(DT_WIDTH, np.int32)
    for g in range(N_SSD_GROUPS):
        for d in range(2):
            for r in range(HEADS_PER_GROUP):
                perm[g * 16 + d * HEADS_PER_GROUP + r] = d * N_SSD_HEADS + g * HEADS_PER_GROUP + r
    return perm


def kernel(x_prompt, x_sample, c, cache_k, cache_v, state_ssm_fwd, state_ssm_bwd, c_ctx, norm_g, w_mod, b_mod, w_in, conv_w, conv_b, attn_sink, a_log_fwd, a_log_bwd, dt_bias_fwd, dt_bias_bwd, d_skip, ssd_norm_g, w_pa, w_ps, w_out, final_norm_g):
    bc, lc, _ = x_prompt.shape
    bl, ll, _ = x_sample.shape
    depth = w_in.shape[0]
    n_ctx = bc * lc
    m = n_ctx + bl * ll
    assert n_ctx % ll == 0 and ll % lc == 0 and lc % CHUNK == 0
    assert 1 + bl <= COND_ROWS

    def group_of(i, tm):
        return jnp.maximum(i * tm - n_ctx + ll, 0) // ll

    tm_big = math.gcd(1024, math.gcd(n_ctx, ll))
    tm_small = math.gcd(512, tm_big)

    cond = jnp.zeros((COND_ROWS, D_MODEL), F32).at[0].set(c_ctx).at[1:1 + bl].set(c)
    mod = _modulation(cond, w_mod, b_mod)

    perm = _dt_permutation()
    wdt = w_in[:, :, W_IN_DT:W_IN_DT + DT_WIDTH][:, :, perm]
    wdtt = jnp.swapaxes(wdt, 1, 2)
    bias = jnp.concatenate([dt_bias_fwd, dt_bias_bwd], axis=1)[:, perm]
    alog = jnp.concatenate([a_log_fwd, a_log_bwd], axis=1)[:, perm]
    norm_g3 = norm_g.reshape(depth, 1, D_MODEL)
    conv_b3 = conv_b.reshape(depth, 1, CONV_WIDTH)
    dsk = jnp.repeat(d_skip, SSD_HEADDIM, axis=1).reshape(depth, 1, SSD_WIDTH)
    ng = ssd_norm_g.reshape(depth, 1, SSD_WIDTH)
    w_pa_bf = w_pa.astype(BF16)
    w_ps_bf = w_ps.astype(BF16)
    w_out_bf = w_out.astype(BF16)
    s0f = state_ssm_fwd.reshape(bl, depth, SSD_WIDTH, D_STATE)
    s0b = state_ssm_bwd.reshape(bl, depth, SSD_WIDTH, D_STATE)

    na_small = n_ctx // tm_small
    na_big = n_ctx // tm_big
    xa, xb, xb_off = x_prompt.reshape(n_ctx, D_MODEL), x_sample.reshape(bl * ll, D_MODEL), 0
    k_new = v_new = states = None
    for l in range(depth):
        h, colp, rowp = _prep(xa, xb, xb_off, m, mod, norm_g3, wdt, wdtt, bias, alog, layer=l,
                              tm=tm_small, na=na_small, group_of=group_of)
        act, k_new, v_new = _inproj(h, w_in, k_new, v_new, layer=l, tm=tm_big, na=na_big, lc=lc,
                                    bc=bc)

        sink = attn_sink[l]
        oa = _ctx_attention(act, sink, L=lc, nseq=bc)
        oa = _lat_attention(act, sink, cache_k, cache_v, oa, L=ll, nseq=bl,
                            row_block0=n_ctx // ll, layer=l)

        ys, sf, sb = _ssd(act, colp, rowp, conv_w, conv_b3, dsk, ng, layer=l, L=lc, nseq=bc,
                          row_block0=0, state_prev=states)
        states = (sf, sb)
        (ys,) = _ssd(act, colp, rowp, conv_w, conv_b3, dsk, ng, layer=l, L=ll, nseq=bl,
                     row_block0=n_ctx // ll, s0f=s0f, s0b=s0b, y_prev=ys)

        merged = _branches(oa, ys, w_pa_bf, w_ps_bf, act, layer=l, tm=tm_big)
        last = l == depth - 1
        res = _out_proj(merged, w_out_bf, xa, xb, xb_off, mod, final_norm_g if last else None,
                        layer=l, tm=tm_small, na=na_small, group_of=group_of)
        if not last:
            xa, xb, xb_off = res, res, na_small

    y_prompt = res[0].reshape(bc, lc, D_MODEL)
    y_sample = res[1].reshape(bl, ll, D_MODEL)
    shape_kv = (bc, depth, lc, N_KV_HEADS, HEAD_DIM)
    shape_st = (bc, depth, N_SSD_HEADS, SSD_HEADDIM, D_STATE)
    return (y_prompt, y_sample, k_new.reshape(shape_kv), v_new.reshape(shape_kv),
            states[0].reshape(shape_st), states[1].reshape(shape_st))
```

```python
import functools
import math

import numpy as np
import jax
import jax.numpy as jnp
from jax import lax
from jax.experimental import pallas as pl
from jax.experimental.pallas import tpu as pltpu

F32 = jnp.float32
BF16 = jnp.bfloat16

D_MODEL = 2048
HEAD_DIM = 128
N_Q_HEADS = 16
N_KV_HEADS = 4
Q_PER_KV = 4
ATTN_WIDTH = 2048
KV_WIDTH = 512
WINDOW = 128
GRID_W = 64
ROPE_BASE = 10000.0
SSD_WIDTH = 4096
SSD_HEADDIM = 64
N_SSD_HEADS = 64
D_STATE = 128
N_SSD_GROUPS = 8
HEADS_PER_GROUP = 8
GROUP_WIDTH = SSD_WIDTH // N_SSD_GROUPS
CHUNK = 128
D_CONV = 5
BC_WIDTH = 1024
CONV_WIDTH = 6144
EPS = 1e-6
MOD_WIDTH = 3 * D_MODEL

COL_Q = 0
COL_K = 2048
COL_V = 2560
COL_ZA = 3072
COL_XBC = 5120
COL_ZS = 11264
COL_GA = 15360
COL_GS = 17408
ACT_WIDTH = 19456
W_IN_DT = 15360
DT_WIDTH = 2 * N_SSD_HEADS

LANES = 128
SUBLANES = 8
VMEM_LIMIT = 56 * 1024 * 1024

COND_ROWS = 8


def _cparams(sem):
    return pltpu.CompilerParams(dimension_semantics=sem, vmem_limit_bytes=VMEM_LIMIT)


def _dot(a, b):
    return jnp.dot(a, b, preferred_element_type=F32)


def _dot_nt(a, b):
    return lax.dot_general(a, b, (((1,), (1,)), ((), ())), preferred_element_type=F32)


def _split2(x):
    hi = x.astype(BF16)
    lo = (x - hi.astype(F32)).astype(BF16)
    return hi, lo


def _split3(x):
    p1 = x.astype(BF16)
    r1 = x - p1.astype(F32)
    p2 = r1.astype(BF16)
    p3 = (r1 - p2.astype(F32)).astype(BF16)
    return p1, p2, p3


def _dot3(a, b):
    ah, al = _split2(a)
    bh, bl = _split2(b)
    return _dot(ah, bh) + _dot(al, bh) + _dot(ah, bl)


def _sigmoid(x):
    return 1.0 / (1.0 + jnp.exp(-x))


def _silu(x):
    return x * _sigmoid(x)


def _softplus(x):
    return jnp.maximum(x, 0.0) + jnp.log1p(jnp.exp(-jnp.abs(x)))


def _mod_kernel(cond_ref, w_ref, b_ref, o_ref):
    res = _dot3(_silu(cond_ref[...]), w_ref[...]) + b_ref[...]
    for r in range(COND_ROWS):
        o_ref[r] = res[r:r + 1, :]


def _modulation(cond, w_mod, b_mod):
    depth = w_mod.shape[0]
    tn = 512
    return pl.pallas_call(
        _mod_kernel,
        grid=(depth, MOD_WIDTH // tn),
        in_specs=[
            pl.BlockSpec((COND_ROWS, D_MODEL), lambda l, j: (0, 0)),
            pl.BlockSpec((None, D_MODEL, tn), lambda l, j: (l, 0, j)),
            pl.BlockSpec((None, 1, tn), lambda l, j: (l, 0, j)),
        ],
        out_specs=pl.BlockSpec((None, COND_ROWS, 1, tn), lambda l, j: (l, 0, 0, j)),
        out_shape=jax.ShapeDtypeStruct((depth, COND_ROWS, 1, MOD_WIDTH), F32),
        compiler_params=_cparams(("arbitrary", "arbitrary")),
        name="modulation",
    )(cond, w_mod, b_mod.reshape(depth, 1, MOD_WIDTH))


def _prep_kernel(xa_ref, xb_ref, shift_ref, scale_ref, g_ref, wdt_ref, wdtt_ref, bias_ref, biast_ref,
                 alog_ref, alogt_ref, sel1_ref, sel2_ref, h_ref, p1_ref, p2_ref, row_ref, *, tm, na):
    x = jnp.where(pl.program_id(0) < na, xa_ref[...], xb_ref[...])
    ms = jnp.mean(x * x, axis=-1, keepdims=True)
    h = (x * lax.rsqrt(ms + EPS) * g_ref[...]) * (1.0 + scale_ref[...]) + shift_ref[...]
    h_ref[...] = h.astype(BF16)

    hh, hl = _split2(h)
    wh, wl = _split2(wdt_ref[...])
    raw = _dot(hh, wh) + _dot(hl, wh) + _dot(hh, wl)
    wth, wtl = _split2(wdtt_ref[...])
    rawt = _dot_nt(wth, hh) + _dot_nt(wth, hl) + _dot_nt(wtl, hh)

    dt = _softplus(raw + bias_ref[...])
    dta = dt * (-jnp.exp(alog_ref[...]))
    dtt = _softplus(rawt + biast_ref[...])
    dtat = dtt * (-jnp.exp(alogt_ref[...]))

    ii = lax.broadcasted_iota(jnp.int32, (CHUNK, CHUNK), 0)
    kk = lax.broadcasted_iota(jnp.int32, (CHUNK, CHUNK), 1)
    lt = jnp.where(kk <= ii, 1.0, 0.0).astype(BF16)
    ut = jnp.where(kk >= ii, 1.0, 0.0).astype(BF16)
    fwd_lane = (kk % 16) < HEADS_PER_GROUP
    fwd_row = (ii % 16) < HEADS_PER_GROUP

    for c in range(tm // CHUNK):
        rows = slice(c * CHUNK, (c + 1) * CHUNK)
        p1, p2, p3 = _split3(dta[rows, :])
        pre = _dot(lt, p1) + _dot(lt, p2) + _dot(lt, p3)
        suf = _dot(ut, p1) + _dot(ut, p2) + _dot(ut, p3)
        acs = jnp.where(fwd_lane, pre, suf)
        edge = jnp.where(fwd_lane, acs[CHUNK - 1:CHUNK, :], acs[0:1, :])
        e = jnp.exp(acs)
        w1 = dt[rows, :] * jnp.exp(edge - acs)
        p1 = _dot(jnp.concatenate(_split3(acs), axis=1), sel1_ref[...])
        p1_ref[rows, :] = p1.astype(BF16)
        p2 = _dot(jnp.concatenate(_split3(e) + _split3(w1), axis=1), sel2_ref[...])
        p2_ref[rows, :] = p2.astype(BF16)
        q1, q2, q3 = _split3(dtat[:, rows])
        pre_t = _dot(q1, ut) + _dot(q2, ut) + _dot(q3, ut)
        suf_t = _dot(q1, lt) + _dot(q2, lt) + _dot(q3, lt)
        row_ref[c, 0] = jnp.where(fwd_row, pre_t, suf_t) - jnp.log(dtt[:, rows])
        row_ref[c, 1] = dtt[:, rows]


def _row_split_specs(tm, na, xb_offset):
    return [
        pl.BlockSpec((tm, D_MODEL), lambda i: (jnp.minimum(i, na - 1), 0)),
        pl.BlockSpec((tm, D_MODEL), lambda i: (xb_offset + jnp.maximum(i - na, 0), 0)),
    ]


def _prep(xa, xb, xb_offset, m, mod, norm_g, wdt, wdtt, bias, alog, consts, *, layer, tm, na,
          group_of):
    l = layer
    kern = functools.partial(_prep_kernel, tm=tm, na=na)
    par2 = lambda i: (l, 0, 0)
    whole = lambda i: (0, 0)
    pw = N_SSD_GROUPS * LANES
    return pl.pallas_call(
        kern,
        grid=(m // tm,),
        in_specs=_row_split_specs(tm, na, xb_offset) + [
            pl.BlockSpec((None, None, 1, D_MODEL), lambda i: (l, group_of(i, tm), 0, 0)),
            pl.BlockSpec((None, None, 1, D_MODEL), lambda i: (l, group_of(i, tm), 0, 1)),
            pl.BlockSpec((None, 1, D_MODEL), par2),
            pl.BlockSpec((None, D_MODEL, DT_WIDTH), par2),
            pl.BlockSpec((None, DT_WIDTH, D_MODEL), par2),
            pl.BlockSpec((None, 1, DT_WIDTH), par2),
            pl.BlockSpec((None, DT_WIDTH, 1), par2),
            pl.BlockSpec((None, 1, DT_WIDTH), par2),
            pl.BlockSpec((None, DT_WIDTH, 1), par2),
            pl.BlockSpec((N_PIECES * LANES, pw), whole),
            pl.BlockSpec((2 * N_PIECES * LANES, pw), whole),
        ],
        out_specs=[
            pl.BlockSpec((tm, D_MODEL), lambda i: (i, 0)),
            pl.BlockSpec((tm, pw), lambda i: (i, 0)),
            pl.BlockSpec((tm, pw), lambda i: (i, 0)),
            pl.BlockSpec((tm // CHUNK, 2, DT_WIDTH, CHUNK), lambda i: (i, 0, 0, 0)),
        ],
        out_shape=[
            jax.ShapeDtypeStruct((m, D_MODEL), BF16),
            jax.ShapeDtypeStruct((m, pw), BF16),
            jax.ShapeDtypeStruct((m, pw), BF16),
            jax.ShapeDtypeStruct((m // CHUNK, 2, DT_WIDTH, CHUNK), F32),
        ],
        compiler_params=_cparams(("arbitrary",)),
        name="prep",
    )(xa, xb, mod, mod, norm_g, wdt, wdtt, bias[:, None, :], bias[:, :, None], alog[:, None, :],
      alog[:, :, None], consts["sel1"], consts["sel2"])


N_PIECES = 3
N_HD = 2 * HEADS_PER_GROUP
PIECE_LANES = N_PIECES * N_HD
N_EXPAND = 4


def _ssd_constants():
    pw = N_SSD_GROUPS * LANES
    sel1 = np.zeros((N_PIECES * LANES, pw), np.float32)
    sel2 = np.zeros((2 * N_PIECES * LANES, pw), np.float32)
    for g in range(N_SSD_GROUPS):
        for hd in range(N_HD):
            lam = g * N_HD + hd
            for p in range(N_PIECES):
                sel1[p * LANES + lam, g * LANES + N_PIECES * hd + p] = 1.0
                for q in range(2):
                    sel2[(q * N_PIECES + p) * LANES + lam,
                         g * LANES + q * PIECE_LANES + N_PIECES * hd + p] = 1.0
    cbc = np.zeros((LANES, N_HD * LANES), np.float32)
    for hd in range(N_HD):
        cbc[N_PIECES * hd:N_PIECES * (hd + 1), hd * LANES:(hd + 1) * LANES] = 1.0
    eexp = np.zeros((N_EXPAND, LANES, GROUP_WIDTH), np.float32)
    for e, (q, d) in enumerate(((0, 0), (1, 0), (0, 1), (1, 1))):
        for r in range(HEADS_PER_GROUP):
            row0 = q * PIECE_LANES + N_PIECES * (d * HEADS_PER_GROUP + r)
            eexp[e, row0:row0 + N_PIECES, r * SSD_HEADDIM:(r + 1) * SSD_HEADDIM] = 1.0
    return {k: jnp.asarray(v, BF16) for k, v in
            (("sel1", sel1), ("sel2", sel2), ("cbc", cbc), ("eexp", eexp))}


INPROJ_TN = 1024
KV_BLOCK = COL_K // INPROJ_TN


def _inproj_kernel(*refs, na, lc, aliased):
    if aliased:
        h_ref, w_ref, _kprev, _vprev, act_ref, k_ref, v_ref, wbf_s = refs
    else:
        h_ref, w_ref, act_ref, k_ref, v_ref, wbf_s = refs
    j = pl.program_id(0)
    i = pl.program_id(1)

    @pl.when(i == 0)
    def _():
        wbf_s[...] = w_ref[...].astype(BF16)

    acc = _dot(h_ref[...], wbf_s[...])
    act_ref[...] = acc.astype(BF16)

    @pl.when((j == KV_BLOCK) & (i < na))
    def _():
        for s in range(k_ref.shape[0]):
            k_ref[s] = acc[s * lc:(s + 1) * lc, :KV_WIDTH]
            v_ref[s] = acc[s * lc:(s + 1) * lc, KV_WIDTH:]


def _inproj(h, w_in, k_prev, v_prev, *, layer, tm, na, lc, bc):
    m = h.shape[0]
    depth = w_in.shape[0]
    tn = INPROJ_TN
    l = layer
    spb = tm // lc
    aliased = k_prev is not None

    def w_col(j):
        skip = jnp.where(j >= W_IN_DT // tn, DT_WIDTH // LANES, 0)
        return (j * (tn // LANES) + skip) * LANES

    def kv_idx(j, i):
        return jnp.where(j < KV_BLOCK, 0, jnp.where(j == KV_BLOCK, jnp.minimum(i, na - 1), na - 1))

    in_specs = [
        pl.BlockSpec((tm, D_MODEL), lambda j, i: (i, 0)),
        pl.BlockSpec((None, pl.Element(D_MODEL), pl.Element(tn)), lambda j, i: (l, 0, w_col(j))),
    ]
    args = [h, w_in]
    aliases = {}
    if aliased:
        in_specs += [pl.BlockSpec(memory_space=pl.ANY)] * 2
        args += [k_prev, v_prev]
        aliases = {2: 1, 3: 2}
    kv_spec = pl.BlockSpec((spb, None, lc, KV_WIDTH), lambda j, i: (kv_idx(j, i), l, 0, 0))
    return pl.pallas_call(
        functools.partial(_inproj_kernel, na=na, lc=lc, aliased=aliased),
        grid=(ACT_WIDTH // tn, m // tm),
        in_specs=in_specs,
        out_specs=[pl.BlockSpec((tm, tn), lambda j, i: (i, j)), kv_spec, kv_spec],
        out_shape=[
            jax.ShapeDtypeStruct((m, ACT_WIDTH), BF16),
            jax.ShapeDtypeStruct((bc, depth, lc, KV_WIDTH), F32),
            jax.ShapeDtypeStruct((bc, depth, lc, KV_WIDTH), F32),
        ],
        scratch_shapes=[pltpu.VMEM((D_MODEL, tn), BF16)],
        input_output_aliases=aliases,
        compiler_params=_cparams(("arbitrary", "arbitrary")),
        name="inproj",
    )(*args)


PAD = SUBLANES


def _ssd_kernel(*refs, L, has_init, emit_state, n_alias):
    refs = list(refs)
    (x_ref, b_ref, c_ref, z_ref, p1_ref, p2_ref, row_ref, cbc_ref, eexp_ref, cwx_ref, cwb_ref,
     cwc_ref, cbx_ref, cbb_ref, cbias_c_ref, dsk_ref, ng_ref) = refs[:17]
    pos = 17
    if has_init:
        s0f_ref, s0b_ref = refs[pos:pos + 2]
        pos += 2
    pos += n_alias
    y_ref = refs[pos]
    pos += 1
    if emit_state:
        sf_ref, sb_ref = refs[pos:pos + 2]
        pos += 2
    pad_s, xc_s, bc_s, cc_s, bt_s, cum_s, exp_s, yacc_s, sft_s, sbt_s = refs[pos:]
    nc = L // CHUNK

    def conv_part(src_ref, w_ref, bias_ref, dst_s, width):
        zeros = jnp.zeros((PAD, width), F32)
        pad_s[0:PAD, 0:width] = zeros
        pad_s[L + PAD:L + 2 * PAD, 0:width] = zeros
        for c in range(nc):
            pad_s[PAD + c * CHUNK:PAD + (c + 1) * CHUNK, 0:width] = (
                src_ref[c * CHUNK:(c + 1) * CHUNK, :].astype(F32))
        for c in range(nc):
            for s in range(width // LANES):
                ls = slice(s * LANES, (s + 1) * LANES)
                acc = jnp.broadcast_to(bias_ref[:, ls], (CHUNK, LANES))
                for k in range(D_CONV):
                    start = PAD + c * CHUNK + k - D_CONV // 2
                    acc = acc + w_ref[k:k + 1, ls] * pad_s[start:start + CHUNK, ls]
                dst_s[c * CHUNK:(c + 1) * CHUNK, ls] = _silu(acc)

    conv_part(x_ref, cwx_ref, cbx_ref, xc_s, GROUP_WIDTH)
    conv_part(b_ref, cwb_ref, cbb_ref, bc_s, D_STATE)
    conv_part(c_ref, cwc_ref, cbias_c_ref, cc_s, D_STATE)

    if has_init:
        sft_s[...] = s0f_ref[0].T
        sbt_s[...] = s0b_ref[0].T
    else:
        sft_s[...] = jnp.zeros_like(sft_s)
        sbt_s[...] = jnp.zeros_like(sbt_s)

    ii = lax.broadcasted_iota(jnp.int32, (CHUNK, CHUNK), 0)
    jj = lax.broadcasted_iota(jnp.int32, (CHUNK, CHUNK), 1)
    lower = jj <= ii
    diag = jj == ii
    left = jj < SSD_HEADDIM

    tile = 2 * LANES
    for t in range(N_HD * LANES // tile):
        ls = slice(t * tile, (t + 1) * tile)
        cum_s[:, ls] = _dot(p1_ref[...], cbc_ref[:, ls])
    for e in range(N_EXPAND):
        for t in range(GROUP_WIDTH // tile):
            ls = slice(t * tile, (t + 1) * tile)
            exp_s[e, :, ls] = _dot(p2_ref[...], eexp_ref[e, :, ls])

    def fwd_chunk(c, carry):
        r0 = pl.multiple_of(c * CHUNK, CHUNK)
        rows = pl.ds(r0, CHUNK)
        xq = xc_s[rows, :]
        bq = bc_s[rows, :]
        cq = cc_s[rows, :].astype(BF16)
        rowa = row_ref[c, 0]
        rowd = row_ref[c, 1]
        cb = _dot_nt(cq, bq.astype(BF16))
        y_off = _dot(cq, sft_s[...].astype(BF16)) * exp_s[0, rows, :]
        y_parts = []
        for k in range(HEADS_PER_GROUP // 2):
            ms = []
            for r in (2 * k, 2 * k + 1):
                rb = HEADS_PER_GROUP + r
                seg_f = cum_s[rows, r * LANES:(r + 1) * LANES] - rowa[r:r + 1, :]
                seg_b = cum_s[rows, rb * LANES:(rb + 1) * LANES] - rowa[rb:rb + 1, :]
                dm = jnp.exp(jnp.where(lower, seg_f, seg_b))
                dm = dm + jnp.where(diag, rowd[rb:rb + 1, :], 0.0)
                ms.append((cb * dm).astype(BF16))
            lhs = jnp.concatenate(ms, axis=1)
            xp = xq[:, k * LANES:(k + 1) * LANES]
            rhs = jnp.concatenate([jnp.where(left, xp, 0.0), jnp.where(left, 0.0, xp)],
                                  axis=0).astype(BF16)
            y_parts.append(_dot(lhs, rhs))
        yacc_s[rows, :] = jnp.concatenate(y_parts, axis=1) + y_off
        decay = exp_s[0, pl.ds(r0 + CHUNK - 1, 1), :]
        bt = bq.T.astype(BF16)
        bt_s[c] = bt
        sft_s[...] = sft_s[...] * decay + _dot(bt, (xq * exp_s[1, rows, :]).astype(BF16))
        return carry

    lax.fori_loop(0, nc, fwd_chunk, 0)

    def bwd_chunk(t, carry):
        c = nc - 1 - t
        r0 = pl.multiple_of(c * CHUNK, CHUNK)
        rows = pl.ds(r0, CHUNK)
        xq = xc_s[rows, :]
        cq = cc_s[rows, :].astype(BF16)
        y = (yacc_s[rows, :] + _dot(cq, sbt_s[...].astype(BF16)) * exp_s[2, rows, :]
             + dsk_ref[...] * xq)
        y = y * _silu(z_ref[rows, :].astype(F32))
        ms = jnp.mean(y * y, axis=-1, keepdims=True)
        y_ref[rows, :] = (y * lax.rsqrt(ms + EPS) * ng_ref[...]).astype(BF16)
        decay = exp_s[2, pl.ds(r0, 1), :]
        sbt_s[...] = sbt_s[...] * decay + _dot(bt_s[c], (xq * exp_s[3, rows, :]).astype(BF16))
        return carry

    lax.fori_loop(0, nc, bwd_chunk, 0)

    if emit_state:
        sf_ref[0] = sft_s[...].T
        sb_ref[0] = sbt_s[...].T


def _ssd(act, p1, p2, rowp, consts, conv_w, conv_b, dsk, ng, *, layer, L, nseq, row_block0,
         s0f=None, s0b=None, y_prev=None, state_prev=None):
    m = act.shape[0]
    depth = conv_w.shape[0]
    l = layer
    has_init = s0f is not None
    emit_state = not has_init
    nc = L // CHUNK
    gw = GROUP_WIDTH
    rb = row_block0
    off_b = SSD_WIDTH // D_STATE
    off_c = (SSD_WIDTH + BC_WIDTH) // D_STATE
    in_specs = [
        pl.BlockSpec((L, gw), lambda b, g: (rb + b, COL_XBC // gw + g)),
        pl.BlockSpec((L, D_STATE), lambda b, g: (rb + b, COL_XBC // D_STATE + off_b + g)),
        pl.BlockSpec((L, D_STATE), lambda b, g: (rb + b, COL_XBC // D_STATE + off_c + g)),
        pl.BlockSpec((L, gw), lambda b, g: (rb + b, COL_ZS // gw + g)),
        pl.BlockSpec((L, LANES), lambda b, g: (rb + b, g)),
        pl.BlockSpec((L, LANES), lambda b, g: (rb + b, g)),
        pl.BlockSpec((nc, 2, N_HD, CHUNK), lambda b, g: (rb + b, 0, g, 0)),
        pl.BlockSpec((LANES, N_HD * LANES), lambda b, g: (0, 0)),
        pl.BlockSpec((N_EXPAND, LANES, gw), lambda b, g: (0, 0, 0)),
        pl.BlockSpec((None, D_CONV, gw), lambda b, g: (l, 0, g)),
        pl.BlockSpec((None, D_CONV, D_STATE), lambda b, g: (l, 0, off_b + g)),
        pl.BlockSpec((None, D_CONV, D_STATE), lambda b, g: (l, 0, off_c + g)),
        pl.BlockSpec((None, 1, gw), lambda b, g: (l, 0, g)),
        pl.BlockSpec((None, 1, D_STATE), lambda b, g: (l, 0, off_b + g)),
        pl.BlockSpec((None, 1, D_STATE), lambda b, g: (l, 0, off_c + g)),
        pl.BlockSpec((None, 1, gw), lambda b, g: (l, 0, g)),
        pl.BlockSpec((None, 1, gw), lambda b, g: (l, 0, g)),
    ]
    args = [act, act, act, act, p1, p2, rowp, consts["cbc"], consts["eexp"], conv_w, conv_w, conv_w,
            conv_b, conv_b, conv_b, dsk, ng]
    aliases = {}
    n_alias = 0
    state_spec = pl.BlockSpec((1, None, gw, D_STATE), lambda b, g: (b, l, g, 0))
    if has_init:
        in_specs += [state_spec, state_spec, pl.BlockSpec(memory_space=pl.ANY)]
        args += [s0f, s0b, y_prev]
        aliases = {len(args) - 1: 0}
        n_alias = 1
    elif state_prev is not None:
        in_specs += [pl.BlockSpec(memory_space=pl.ANY)] * 2
        args += list(state_prev)
        aliases = {len(args) - 2: 1, len(args) - 1: 2}
        n_alias = 2
    out_specs = [pl.BlockSpec((L, gw), lambda b, g: (rb + b, g))]
    out_shape = [jax.ShapeDtypeStruct((m, SSD_WIDTH), BF16)]
    if emit_state:
        out_specs += [state_spec] * 2
        out_shape += [jax.ShapeDtypeStruct((nseq, depth, SSD_WIDTH, D_STATE), F32)] * 2
    scratch = [
        pltpu.VMEM((L + 2 * PAD, gw), F32),
        pltpu.VMEM((L, gw), F32),
        pltpu.VMEM((L, D_STATE), F32),
        pltpu.VMEM((L, D_STATE), F32),
        pltpu.VMEM((nc, D_STATE, CHUNK), BF16),
        pltpu.VMEM((L, N_HD * LANES), F32),
        pltpu.VMEM((N_EXPAND, L, gw), F32),
        pltpu.VMEM((L, gw), F32),
        pltpu.VMEM((D_STATE, gw), F32),
        pltpu.VMEM((D_STATE, gw), F32),
    ]
    kern = functools.partial(_ssd_kernel, L=L, has_init=has_init, emit_state=emit_state,
                             n_alias=n_alias)
    return pl.pallas_call(
        kern,
        grid=(nseq, N_SSD_GROUPS),
        in_specs=in_specs,
        out_specs=out_specs,
        out_shape=out_shape,
        scratch_shapes=scratch,
        input_output_aliases=aliases,
        compiler_params=_cparams(("arbitrary", "arbitrary")),
        name="ssd_latent" if has_init else "ssd_context",
    )(*args)


ATTN_SCALE = HEAD_DIM ** -0.5


def _ctx_attn_kernel(sink_ref, q_ref, k_ref, v_ref, z_ref, o_ref):
    g = pl.program_id(1)
    k = k_ref[...]
    v = v_ref[...]
    for r in range(Q_PER_KV):
        ls = slice(r * HEAD_DIM, (r + 1) * HEAD_DIM)
        sink = sink_ref[g * Q_PER_KV + r]
        s = _dot_nt(q_ref[:, ls], k) * ATTN_SCALE
        m = jnp.maximum(jnp.max(s, axis=-1, keepdims=True), sink)
        p = jnp.exp(s - m)
        denom = jnp.sum(p, axis=-1, keepdims=True) + jnp.exp(sink - m)
        o = _dot(p.astype(BF16), v) / denom
        o_ref[:, ls] = (o * _silu(z_ref[:, ls].astype(F32))).astype(BF16)


def _ctx_attention(act, sink, *, L, nseq):
    m = act.shape[0]
    gw = Q_PER_KV * HEAD_DIM
    return pl.pallas_call(
        _ctx_attn_kernel,
        grid=(nseq, N_KV_HEADS),
        in_specs=[
            pl.BlockSpec(memory_space=pltpu.SMEM),
            pl.BlockSpec((L, gw), lambda b, g: (b, COL_Q // gw + g)),
            pl.BlockSpec((L, HEAD_DIM), lambda b, g: (b, COL_K // HEAD_DIM + g)),
            pl.BlockSpec((L, HEAD_DIM), lambda b, g: (b, COL_V // HEAD_DIM + g)),
            pl.BlockSpec((L, gw), lambda b, g: (b, COL_ZA // gw + g)),
        ],
        out_specs=pl.BlockSpec((L, gw), lambda b, g: (b, g)),
        out_shape=jax.ShapeDtypeStruct((m, ATTN_WIDTH), BF16),
        compiler_params=_cparams(("arbitrary", "arbitrary")),
        name="attn_context",
    )(sink, act, act, act, act)


def _rope_tables(length):
    sec = HEAD_DIM // 2
    half = sec // 2
    d = np.arange(HEAD_DIM)
    e = d % sec
    freqs = ROPE_BASE ** (-(e % half).astype(np.float64) / half)
    t = np.arange(length)
    pos = np.where((d // sec)[None, :] == 0, (t // GRID_W)[:, None], (t % GRID_W)[:, None])
    ang = pos.astype(np.float64) * freqs[None, :]
    sign = np.where(e < half, -1.0, 1.0)[None, :]
    return (jnp.asarray(np.cos(ang), F32), jnp.asarray(np.sin(ang) * sign, F32))


def _rope(x, cos, sin_signed, first_half):
    partner = jnp.where(first_half, pltpu.roll(x, LANES - HEAD_DIM // 4, 1),
                        pltpu.roll(x, HEAD_DIM // 4, 1))
    return x * cos + partner * sin_signed


def _lat_attn_kernel(sink_ref, q_ref, k_ref, v_ref, z_ref, kc_ref, vc_ref, cos_ref, sin_ref,
                     _oprev_ref, o_ref, kctx_s, vctx_s, keys_s, vals_s, *, L, lc):
    g = pl.program_id(1)
    nb = L // CHUNK
    win = 3 * CHUNK
    rows4 = Q_PER_KV * CHUNK
    lane = lax.broadcasted_iota(jnp.int32, (CHUNK, HEAD_DIM), 1)
    first_half = (lane % (HEAD_DIM // 2)) < (HEAD_DIM // 4)

    kctx_s[...] = kc_ref[...].astype(BF16)
    vctx_s[...] = vc_ref[...].astype(BF16)
    zero_blk = jnp.zeros((CHUNK, HEAD_DIM), BF16)
    for dst in (keys_s, vals_s):
        dst[0:CHUNK, :] = zero_blk
        dst[CHUNK + L:2 * CHUNK + L, :] = zero_blk
    for n in range(nb):
        rows = slice(n * CHUNK, (n + 1) * CHUNK)
        kr = _rope(k_ref[rows, :].astype(F32), cos_ref[rows, :], sin_ref[rows, :], first_half)
        keys_s[CHUNK + n * CHUNK:CHUNK + (n + 1) * CHUNK, :] = kr.astype(BF16)
        vals_s[CHUNK + n * CHUNK:CHUNK + (n + 1) * CHUNK, :] = v_ref[rows, :]

    qi = lax.broadcasted_iota(jnp.int32, (rows4, win), 0) % CHUNK
    wi = lax.broadcasted_iota(jnp.int32, (rows4, win), 1)
    band = jnp.abs(qi - wi + CHUNK) <= WINDOW
    head = lax.broadcasted_iota(jnp.int32, (rows4, 1), 0) // CHUNK
    sink = jnp.zeros((rows4, 1), F32)
    for r in range(Q_PER_KV):
        sink = jnp.where(head == r, sink_ref[g * Q_PER_KV + r], sink)

    def block(n, carry):
        r0 = pl.multiple_of(n * CHUNK, CHUNK)
        rows = pl.ds(r0, CHUNK)
        cos = cos_ref[rows, :]
        sin = sin_ref[rows, :]
        q = jnp.concatenate(
            [_rope(q_ref[rows, r * HEAD_DIM:(r + 1) * HEAD_DIM].astype(F32), cos, sin,
                   first_half).astype(BF16) for r in range(Q_PER_KV)], axis=0)
        in_seq = (wi >= CHUNK - r0) & (wi < L + CHUNK - r0)
        s_ctx = _dot_nt(q, kctx_s[...]) * ATTN_SCALE
        s_lat = _dot_nt(q, keys_s[pl.ds(r0, win), :]) * ATTN_SCALE
        s_lat = jnp.where(in_seq, jnp.where(band, s_lat, -jnp.inf), -jnp.inf)
        m = jnp.maximum(jnp.maximum(jnp.max(s_ctx, axis=-1, keepdims=True),
                                    jnp.max(s_lat, axis=-1, keepdims=True)), sink)
        p_ctx = jnp.exp(s_ctx - m)
        p_lat = jnp.exp(s_lat - m)
        denom = (jnp.sum(p_ctx, axis=-1, keepdims=True) + jnp.sum(p_lat, axis=-1, keepdims=True)
                 + jnp.exp(sink - m))
        o = (_dot(p_ctx.astype(BF16), vctx_s[...])
             + _dot(p_lat.astype(BF16), vals_s[pl.ds(r0, win), :])) / denom
        for r in range(Q_PER_KV):
            ls = slice(r * HEAD_DIM, (r + 1) * HEAD_DIM)
            o_ref[rows, ls] = (o[r * CHUNK:(r + 1) * CHUNK, :]
                               * _silu(z_ref[rows, ls].astype(F32))).astype(BF16)
        return carry

    lax.fori_loop(0, nb, block, 0)


def _lat_attention(act, sink, cache_k, cache_v, o_prev, *, L, nseq, row_block0, layer):
    m = act.shape[0]
    gw = Q_PER_KV * HEAD_DIM
    lc = cache_k.shape[2]
    rb = row_block0
    cos, sin = _rope_tables(L)
    kc = cache_k.reshape(cache_k.shape[0], cache_k.shape[1], lc, KV_WIDTH)
    vc = cache_v.reshape(cache_v.shape[0], cache_v.shape[1], lc, KV_WIDTH)
    kern = functools.partial(_lat_attn_kernel, L=L, lc=lc)
    tab = lambda b, g: (0, 0)
    return pl.pallas_call(
        kern,
        grid=(nseq, N_KV_HEADS),
        in_specs=[
            pl.BlockSpec(memory_space=pltpu.SMEM),
            pl.BlockSpec((L, gw), lambda b, g: (rb + b, COL_Q // gw + g)),
            pl.BlockSpec((L, HEAD_DIM), lambda b, g: (rb + b, COL_K // HEAD_DIM + g)),
            pl.BlockSpec((L, HEAD_DIM), lambda b, g: (rb + b, COL_V // HEAD_DIM + g)),
            pl.BlockSpec((L, gw), lambda b, g: (rb + b, COL_ZA // gw + g)),
            pl.BlockSpec((None, None, lc, HEAD_DIM), lambda b, g: (b, layer, 0, g)),
            pl.BlockSpec((None, None, lc, HEAD_DIM), lambda b, g: (b, layer, 0, g)),
            pl.BlockSpec((L, HEAD_DIM), tab),
            pl.BlockSpec((L, HEAD_DIM), tab),
            pl.BlockSpec(memory_space=pl.ANY),
        ],
        out_specs=pl.BlockSpec((L, gw), lambda b, g: (rb + b, g)),
        out_shape=jax.ShapeDtypeStruct((m, ATTN_WIDTH), BF16),
        scratch_shapes=[pltpu.VMEM((lc, HEAD_DIM), BF16), pltpu.VMEM((lc, HEAD_DIM), BF16),
                        pltpu.VMEM((L + 2 * CHUNK, HEAD_DIM), BF16),
                        pltpu.VMEM((L + 2 * CHUNK, HEAD_DIM), BF16)],
        input_output_aliases={9: 0},
        compiler_params=_cparams(("arbitrary", "arbitrary")),
        name="attn_latent",
    )(sink, act, act, act, act, kc, vc, cos, sin, o_prev)


def _branch_kernel(oa_ref, ys_ref, wpa_ref, wps_ref, ga_ref, gs_ref, o_ref):
    a = _dot(oa_ref[...], wpa_ref[...])
    s = _dot(ys_ref[...], wps_ref[...])
    merged = _sigmoid(ga_ref[...].astype(F32)) * a + _sigmoid(gs_ref[...].astype(F32)) * s
    o_ref[...] = merged.astype(BF16)


def _branches(oa, ys, w_pa, w_ps, act, *, layer, tm):
    m = oa.shape[0]
    tn = 512
    l = layer
    return pl.pallas_call(
        _branch_kernel,
        grid=(m // tm, D_MODEL // tn),
        in_specs=[
            pl.BlockSpec((tm, ATTN_WIDTH), lambda i, j: (i, 0)),
            pl.BlockSpec((tm, SSD_WIDTH), lambda i, j: (i, 0)),
            pl.BlockSpec((None, ATTN_WIDTH, tn), lambda i, j: (l, 0, j)),
            pl.BlockSpec((None, SSD_WIDTH, tn), lambda i, j: (l, 0, j)),
            pl.BlockSpec((tm, tn), lambda i, j: (i, COL_GA // tn + j)),
            pl.BlockSpec((tm, tn), lambda i, j: (i, COL_GS // tn + j)),
        ],
        out_specs=pl.BlockSpec((tm, tn), lambda i, j: (i, j)),
        out_shape=jax.ShapeDtypeStruct((m, D_MODEL), BF16),
        compiler_params=_cparams(("arbitrary", "arbitrary")),
        name="branches",
    )(oa, ys, w_pa, w_ps, act, act)


def _out_kernel(*refs, final, na):
    if final:
        mg_ref, w_ref, xa_ref, xb_ref, gate_ref, fg_ref, ya_ref, yb_ref = refs
    else:
        mg_ref, w_ref, xa_ref, xb_ref, gate_ref, o_ref = refs
    is_ctx = pl.program_id(0) < na
    x = jnp.where(is_ctx, xa_ref[...], xb_ref[...])
    y = x + gate_ref[...] * _dot(mg_ref[...], w_ref[...])
    if not final:
        o_ref[...] = y
        return
    ms = jnp.mean(y * y, axis=-1, keepdims=True)
    y = y * lax.rsqrt(ms + EPS) * fg_ref[...]

    @pl.when(is_ctx)
    def _():
        ya_ref[...] = y

    @pl.when(jnp.logical_not(is_ctx))
    def _():
        yb_ref[...] = y


def _out_proj(merged, w_out, xa, xb, xb_offset, mod, final_g, *, layer, tm, na, group_of):
    m = merged.shape[0]
    l = layer
    final = final_g is not None
    in_specs = [
        pl.BlockSpec((tm, D_MODEL), lambda i: (i, 0)),
        pl.BlockSpec((None, D_MODEL, D_MODEL), lambda i: (l, 0, 0)),
    ] + _row_split_specs(tm, na, xb_offset) + [
        pl.BlockSpec((None, None, 1, D_MODEL), lambda i: (l, group_of(i, tm), 0, 2)),
    ]
    args = [merged, w_out, xa, xb, mod]
    if final:
        in_specs.append(pl.BlockSpec((1, D_MODEL), lambda i: (0, 0)))
        args.append(final_g.reshape(1, D_MODEL))
        out_specs = [
            pl.BlockSpec((tm, D_MODEL), lambda i: (jnp.minimum(i, na - 1), 0)),
            pl.BlockSpec((tm, D_MODEL), lambda i: (jnp.maximum(i - na, 0), 0)),
        ]
        out_shape = [jax.ShapeDtypeStruct((na * tm, D_MODEL), F32),
                     jax.ShapeDtypeStruct((m - na * tm, D_MODEL), F32)]
    else:
        out_specs = pl.BlockSpec((tm, D_MODEL), lambda i: (i, 0))
        out_shape = jax.ShapeDtypeStruct((m, D_MODEL), F32)
    return pl.pallas_call(
        functools.partial(_out_kernel, final=final, na=na),
        grid=(m // tm,),
        in_specs=in_specs,
        out_specs=out_specs,
        out_shape=out_shape,
        compiler_params=_cparams(("arbitrary",)),
        name="out_proj_final" if final else "out_proj",
    )(*args)


def _dt_permutation():
    perm = np.zeros(DT_WIDTH, np.int32)
    for g in range(N_SSD_GROUPS):
        for d in range(2):
            for r in range(HEADS_PER_GROUP):
                perm[g * 16 + d * HEADS_PER_GROUP + r] = d * N_SSD_HEADS + g * HEADS_PER_GROUP + r
    return perm


def kernel(x_prompt, x_sample, c, cache_k, cache_v, state_ssm_fwd, state_ssm_bwd, c_ctx, norm_g, w_mod, b_mod, w_in, conv_w, conv_b, attn_sink, a_log_fwd, a_log_bwd, dt_bias_fwd, dt_bias_bwd, d_skip, ssd_norm_g, w_pa, w_ps, w_out, final_norm_g):
    bc, lc, _ = x_prompt.shape
    bl, ll, _ = x_sample.shape
    depth = w_in.shape[0]
    n_ctx = bc * lc
    m = n_ctx + bl * ll
    assert n_ctx % ll == 0 and ll % lc == 0 and lc % CHUNK == 0
    assert 1 + bl <= COND_ROWS

    def group_of(i, tm):
        return jnp.maximum(i * tm - n_ctx + ll, 0) // ll

    tm_big = math.gcd(1024, math.gcd(n_ctx, ll))
    tm_small = math.gcd(512, tm_big)

    cond = jnp.zeros((COND_ROWS, D_MODEL), F32).at[0].set(c_ctx).at[1:1 + bl].set(c)
    mod = _modulation(cond, w_mod, b_mod)

    perm = _dt_permutation()
    consts = _ssd_constants()
    wdt = w_in[:, :, W_IN_DT:W_IN_DT + DT_WIDTH][:, :, perm]
    wdtt = jnp.swapaxes(wdt, 1, 2)
    bias = jnp.concatenate([dt_bias_fwd, dt_bias_bwd], axis=1)[:, perm]
    alog = jnp.concatenate([a_log_fwd, a_log_bwd], axis=1)[:, perm]
    norm_g3 = norm_g.reshape(depth, 1, D_MODEL)
    conv_b3 = conv_b.reshape(depth, 1, CONV_WIDTH)
    dsk = jnp.repeat(d_skip, SSD_HEADDIM, axis=1).reshape(depth, 1, SSD_WIDTH)
    ng = ssd_norm_g.reshape(depth, 1, SSD_WIDTH)
    w_pa_bf = w_pa.astype(BF16)
    w_ps_bf = w_ps.astype(BF16)
    w_out_bf = w_out.astype(BF16)
    s0f = state_ssm_fwd.reshape(bl, depth, SSD_WIDTH, D_STATE)
    s0b = state_ssm_bwd.reshape(bl, depth, SSD_WIDTH, D_STATE)

    na_small = n_ctx // tm_small
    na_big = n_ctx // tm_big
    xa, xb, xb_off = x_prompt.reshape(n_ctx, D_MODEL), x_sample.reshape(bl * ll, D_MODEL), 0
    k_new = v_new = states = None
    for l in range(depth):
        h, p1, p2, rowp = _prep(xa, xb, xb_off, m, mod, norm_g3, wdt, wdtt, bias, alog, consts,
                                layer=l, tm=tm_small, na=na_small, group_of=group_of)
        act, k_new, v_new = _inproj(h, w_in, k_new, v_new, layer=l, tm=tm_big, na=na_big, lc=lc,
                                    bc=bc)

        sink = attn_sink[l]
        oa = _ctx_attention(act, sink, L=lc, nseq=bc)
        oa = _lat_attention(act, sink, cache_k, cache_v, oa, L=ll, nseq=bl,
                            row_block0=n_ctx // ll, layer=l)

        ys, sf, sb = _ssd(act, p1, p2, rowp, consts, conv_w, conv_b3, dsk, ng, layer=l, L=lc,
                          nseq=bc, row_block0=0, state_prev=states)
        states = (sf, sb)
        (ys,) = _ssd(act, p1, p2, rowp, consts, conv_w, conv_b3, dsk, ng, layer=l, L=ll, nseq=bl,
                     row_block0=n_ctx // ll, s0f=s0f, s0b=s0b, y_prev=ys)

        merged = _branches(oa, ys, w_pa_bf, w_ps_bf, act, layer=l, tm=tm_big)
        last = l == depth - 1
        res = _out_proj(merged, w_out_bf, xa, xb, xb_off, mod, final_norm_g if last else None,
                        layer=l, tm=tm_small, na=na_small, group_of=group_of)
        if not last:
            xa, xb, xb_off = res, res, na_small

    y_prompt = res[0].reshape(bc, lc, D_MODEL)
    y_sample = res[1].reshape(bl, ll, D_MODEL)
    shape_kv = (bc, depth, lc, N_KV_HEADS, HEAD_DIM)
    shape_st = (bc, depth, N_SSD_HEADS, SSD_HEADDIM, D_STATE)
    return (y_prompt, y_sample, k_new.reshape(shape_kv), v_new.reshape(shape_kv),
            states[0].reshape(shape_st), states[1].reshape(shape_st))
```

```python
import functools
import math

import numpy as np
import jax
import jax.numpy as jnp
from jax import lax
from jax.experimental import pallas as pl
from jax.experimental.pallas import tpu as pltpu

F32 = jnp.float32
BF16 = jnp.bfloat16

D_MODEL = 2048
HEAD_DIM = 128
N_Q_HEADS = 16
N_KV_HEADS = 4
Q_PER_KV = 4
ATTN_WIDTH = 2048
KV_WIDTH = 512
WINDOW = 128
GRID_W = 64
ROPE_BASE = 10000.0
SSD_WIDTH = 4096
SSD_HEADDIM = 64
N_SSD_HEADS = 64
D_STATE = 128
N_SSD_GROUPS = 8
HEADS_PER_GROUP = 8
GROUP_WIDTH = SSD_WIDTH // N_SSD_GROUPS
CHUNK = 128
D_CONV = 5
BC_WIDTH = 1024
CONV_WIDTH = 6144
EPS = 1e-6
MOD_WIDTH = 3 * D_MODEL

COL_Q = 0
COL_K = 2048
COL_V = 2560
COL_ZA = 3072
COL_XBC = 5120
COL_ZS = 11264
COL_GA = 15360
COL_GS = 17408
ACT_WIDTH = 19456
W_IN_DT = 15360
DT_WIDTH = 2 * N_SSD_HEADS

LANES = 128
SUBLANES = 8
VMEM_LIMIT = 56 * 1024 * 1024

COND_ROWS = 8


def _cparams(sem):
    return pltpu.CompilerParams(dimension_semantics=sem, vmem_limit_bytes=VMEM_LIMIT)


def _dot(a, b):
    return jnp.dot(a, b, preferred_element_type=F32)


def _dot_nt(a, b):
    return lax.dot_general(a, b, (((1,), (1,)), ((), ())), preferred_element_type=F32)


def _split2(x):
    hi = x.astype(BF16)
    lo = (x - hi.astype(F32)).astype(BF16)
    return hi, lo


def _split3(x):
    p1 = x.astype(BF16)
    r1 = x - p1.astype(F32)
    p2 = r1.astype(BF16)
    p3 = (r1 - p2.astype(F32)).astype(BF16)
    return p1, p2, p3


def _dot3(a, b):
    ah, al = _split2(a)
    bh, bl = _split2(b)
    return _dot(ah, bh) + _dot(al, bh) + _dot(ah, bl)


def _sigmoid(x):
    return 1.0 / (1.0 + jnp.exp(-x))


def _silu(x):
    return x * _sigmoid(x)


def _softplus(x):
    return jnp.maximum(x, 0.0) + jnp.log1p(jnp.exp(-jnp.abs(x)))


def _mod_kernel(cond_ref, w_ref, b_ref, o_ref):
    res = _dot3(_silu(cond_ref[...]), w_ref[...]) + b_ref[...]
    for r in range(COND_ROWS):
        o_ref[r] = res[r:r + 1, :]


def _modulation(cond, w_mod, b_mod):
    depth = w_mod.shape[0]
    tn = 512
    return pl.pallas_call(
        _mod_kernel,
        grid=(depth, MOD_WIDTH // tn),
        in_specs=[
            pl.BlockSpec((COND_ROWS, D_MODEL), lambda l, j: (0, 0)),
            pl.BlockSpec((None, D_MODEL, tn), lambda l, j: (l, 0, j)),
            pl.BlockSpec((None, 1, tn), lambda l, j: (l, 0, j)),
        ],
        out_specs=pl.BlockSpec((None, COND_ROWS, 1, tn), lambda l, j: (l, 0, 0, j)),
        out_shape=jax.ShapeDtypeStruct((depth, COND_ROWS, 1, MOD_WIDTH), F32),
        compiler_params=_cparams(("arbitrary", "arbitrary")),
        name="modulation",
    )(cond, w_mod, b_mod.reshape(depth, 1, MOD_WIDTH))


def _prep_kernel(xa_ref, xb_ref, shift_ref, scale_ref, g_ref, wdtt_ref, biast_ref, alogt_ref,
                 sel1_ref, sel2_ref, h_ref, p1_ref, p2_ref, row_ref, *, tm, na):
    x = jnp.where(pl.program_id(0) < na, xa_ref[...], xb_ref[...])
    ms = jnp.mean(x * x, axis=-1, keepdims=True)
    h = (x * lax.rsqrt(ms + EPS) * g_ref[...]) * (1.0 + scale_ref[...]) + shift_ref[...]
    h_ref[...] = h.astype(BF16)

    hh, hl = _split2(h)
    wth, wtl = _split2(wdtt_ref[...])
    rawt = _dot_nt(wth, hh) + _dot_nt(wth, hl) + _dot_nt(wtl, hh)
    dtt = _softplus(rawt + biast_ref[...])
    dtat = dtt * (-jnp.exp(alogt_ref[...]))

    ii = lax.broadcasted_iota(jnp.int32, (CHUNK, CHUNK), 0)
    kk = lax.broadcasted_iota(jnp.int32, (CHUNK, CHUNK), 1)
    lt = jnp.where(kk <= ii, 1.0, 0.0).astype(BF16)
    ut = jnp.where(kk >= ii, 1.0, 0.0).astype(BF16)
    fwd_row = (ii % 16) < HEADS_PER_GROUP

    for c in range(tm // CHUNK):
        rows = slice(c * CHUNK, (c + 1) * CHUNK)
        dt_t = dtt[:, rows]
        q1, q2, q3 = _split3(dtat[:, rows])
        pre_t = _dot(q1, ut) + _dot(q2, ut) + _dot(q3, ut)
        suf_t = _dot(q1, lt) + _dot(q2, lt) + _dot(q3, lt)
        acs_t = jnp.where(fwd_row, pre_t, suf_t)
        edge_t = jnp.where(fwd_row, acs_t[:, CHUNK - 1:CHUNK], acs_t[:, 0:1])
        w1_t = dt_t * jnp.exp(edge_t - acs_t)
        row_ref[c, 0] = acs_t - jnp.log(dt_t)
        row_ref[c, 1] = dt_t
        acs = acs_t.T
        p1 = _dot(jnp.concatenate(_split3(acs), axis=1), sel1_ref[...])
        p1_ref[rows, :] = p1.astype(BF16)
        p2 = _dot(jnp.concatenate(_split3(jnp.exp(acs)) + _split3(w1_t.T), axis=1), sel2_ref[...])
        p2_ref[rows, :] = p2.astype(BF16)


def _row_split_specs(tm, na, xb_offset):
    return [
        pl.BlockSpec((tm, D_MODEL), lambda i: (jnp.minimum(i, na - 1), 0)),
        pl.BlockSpec((tm, D_MODEL), lambda i: (xb_offset + jnp.maximum(i - na, 0), 0)),
    ]


def _prep(xa, xb, xb_offset, m, mod, norm_g, wdtt, bias, alog, consts, *, layer, tm, na, group_of):
    l = layer
    kern = functools.partial(_prep_kernel, tm=tm, na=na)
    par2 = lambda i: (l, 0, 0)
    whole = lambda i: (0, 0)
    pw = N_SSD_GROUPS * LANES
    return pl.pallas_call(
        kern,
        grid=(m // tm,),
        in_specs=_row_split_specs(tm, na, xb_offset) + [
            pl.BlockSpec((None, None, 1, D_MODEL), lambda i: (l, group_of(i, tm), 0, 0)),
            pl.BlockSpec((None, None, 1, D_MODEL), lambda i: (l, group_of(i, tm), 0, 1)),
            pl.BlockSpec((None, 1, D_MODEL), par2),
            pl.BlockSpec((None, DT_WIDTH, D_MODEL), par2),
            pl.BlockSpec((None, DT_WIDTH, 1), par2),
            pl.BlockSpec((None, DT_WIDTH, 1), par2),
            pl.BlockSpec((N_PIECES * LANES, pw), whole),
            pl.BlockSpec((2 * N_PIECES * LANES, pw), whole),
        ],
        out_specs=[
            pl.BlockSpec((tm, D_MODEL), lambda i: (i, 0)),
            pl.BlockSpec((tm, pw), lambda i: (i, 0)),
            pl.BlockSpec((tm, pw), lambda i: (i, 0)),
            pl.BlockSpec((tm // CHUNK, 2, DT_WIDTH, CHUNK), lambda i: (i, 0, 0, 0)),
        ],
        out_shape=[
            jax.ShapeDtypeStruct((m, D_MODEL), BF16),
            jax.ShapeDtypeStruct((m, pw), BF16),
            jax.ShapeDtypeStruct((m, pw), BF16),
            jax.ShapeDtypeStruct((m // CHUNK, 2, DT_WIDTH, CHUNK), F32),
        ],
        compiler_params=_cparams(("arbitrary",)),
        name="prep",
    )(xa, xb, mod, mod, norm_g, wdtt, bias[:, :, None], alog[:, :, None], consts["sel1"],
      consts["sel2"])


N_PIECES = 3
N_HD = 2 * HEADS_PER_GROUP
PIECE_LANES = N_PIECES * N_HD
N_EXPAND = 4


def _ssd_constants():
    pw = N_SSD_GROUPS * LANES
    sel1 = np.zeros((N_PIECES * LANES, pw), np.float32)
    sel2 = np.zeros((2 * N_PIECES * LANES, pw), np.float32)
    for g in range(N_SSD_GROUPS):
        for hd in range(N_HD):
            lam = g * N_HD + hd
            for p in range(N_PIECES):
                sel1[p * LANES + lam, g * LANES + N_PIECES * hd + p] = 1.0
                for q in range(2):
                    sel2[(q * N_PIECES + p) * LANES + lam,
                         g * LANES + q * PIECE_LANES + N_PIECES * hd + p] = 1.0
    cbc = np.zeros((LANES, N_HD * LANES), np.float32)
    for hd in range(N_HD):
        cbc[N_PIECES * hd:N_PIECES * (hd + 1), hd * LANES:(hd + 1) * LANES] = 1.0
    eexp = np.zeros((N_EXPAND, LANES, GROUP_WIDTH), np.float32)
    for e, (q, d) in enumerate(((0, 0), (1, 0), (0, 1), (1, 1))):
        for r in range(HEADS_PER_GROUP):
            row0 = q * PIECE_LANES + N_PIECES * (d * HEADS_PER_GROUP + r)
            eexp[e, row0:row0 + N_PIECES, r * SSD_HEADDIM:(r + 1) * SSD_HEADDIM] = 1.0
    return {k: jnp.asarray(v, BF16) for k, v in
            (("sel1", sel1), ("sel2", sel2), ("cbc", cbc), ("eexp", eexp))}


INPROJ_TN = 1024
KV_BLOCK = COL_K // INPROJ_TN


def _inproj_kernel(*refs, na, lc, aliased):
    if aliased:
        h_ref, w_ref, _kprev, _vprev, act_ref, k_ref, v_ref, wbf_s = refs
    else:
        h_ref, w_ref, act_ref, k_ref, v_ref, wbf_s = refs
    j = pl.program_id(0)
    i = pl.program_id(1)

    @pl.when(i == 0)
    def _():
        wbf_s[...] = w_ref[...].astype(BF16)

    acc = _dot(h_ref[...], wbf_s[...])
    act_ref[...] = acc.astype(BF16)

    @pl.when((j == KV_BLOCK) & (i < na))
    def _():
        for s in range(k_ref.shape[0]):
            k_ref[s] = acc[s * lc:(s + 1) * lc, :KV_WIDTH]
            v_ref[s] = acc[s * lc:(s + 1) * lc, KV_WIDTH:]


def _inproj(h, w_in, k_prev, v_prev, *, layer, tm, na, lc, bc):
    m = h.shape[0]
    depth = w_in.shape[0]
    tn = INPROJ_TN
    l = layer
    spb = tm // lc
    aliased = k_prev is not None

    def w_col(j):
        skip = jnp.where(j >= W_IN_DT // tn, DT_WIDTH // LANES, 0)
        return (j * (tn // LANES) + skip) * LANES

    def kv_idx(j, i):
        return jnp.where(j < KV_BLOCK, 0, jnp.where(j == KV_BLOCK, jnp.minimum(i, na - 1), na - 1))

    in_specs = [
        pl.BlockSpec((tm, D_MODEL), lambda j, i: (i, 0)),
        pl.BlockSpec((None, pl.Element(D_MODEL), pl.Element(tn)), lambda j, i: (l, 0, w_col(j))),
    ]
    args = [h, w_in]
    aliases = {}
    if aliased:
        in_specs += [pl.BlockSpec(memory_space=pl.ANY)] * 2
        args += [k_prev, v_prev]
        aliases = {2: 1, 3: 2}
    kv_spec = pl.BlockSpec((spb, None, lc, KV_WIDTH), lambda j, i: (kv_idx(j, i), l, 0, 0))
    return pl.pallas_call(
        functools.partial(_inproj_kernel, na=na, lc=lc, aliased=aliased),
        grid=(ACT_WIDTH // tn, m // tm),
        in_specs=in_specs,
        out_specs=[pl.BlockSpec((tm, tn), lambda j, i: (i, j)), kv_spec, kv_spec],
        out_shape=[
            jax.ShapeDtypeStruct((m, ACT_WIDTH), BF16),
            jax.ShapeDtypeStruct((bc, depth, lc, KV_WIDTH), F32),
            jax.ShapeDtypeStruct((bc, depth, lc, KV_WIDTH), F32),
        ],
        scratch_shapes=[pltpu.VMEM((D_MODEL, tn), BF16)],
        input_output_aliases=aliases,
        compiler_params=_cparams(("arbitrary", "arbitrary")),
        name="inproj",
    )(*args)


PAD = SUBLANES


def _ssd_kernel(*refs, L, has_init, emit_state, n_alias):
    refs = list(refs)
    (x_ref, b_ref, c_ref, z_ref, p1_ref, p2_ref, row_ref, cbc_ref, eexp_ref, cwx_ref, cwb_ref,
     cwc_ref, cbx_ref, cbb_ref, cbias_c_ref, dsk_ref, ng_ref) = refs[:17]
    pos = 17
    if has_init:
        s0f_ref, s0b_ref = refs[pos:pos + 2]
        pos += 2
    pos += n_alias
    y_ref = refs[pos]
    pos += 1
    if emit_state:
        sf_ref, sb_ref = refs[pos:pos + 2]
        pos += 2
    pad_s, xc_s, bc_s, cc_s, bt_s, cum_s, exp_s, yacc_s, sft_s, sbt_s = refs[pos:]
    nc = L // CHUNK

    conv_srcs = ((x_ref, cwx_ref, cbx_ref, xc_s, 0, GROUP_WIDTH),
                 (b_ref, cwb_ref, cbb_ref, bc_s, GROUP_WIDTH, D_STATE),
                 (c_ref, cwc_ref, cbias_c_ref, cc_s, GROUP_WIDTH + D_STATE, D_STATE))
    zeros = jnp.zeros((PAD, pad_s.shape[1]), F32)
    pad_s[0:PAD, :] = zeros
    pad_s[L + PAD:L + 2 * PAD, :] = zeros
    for src_ref, _, _, _, off, width in conv_srcs:
        for c in range(nc):
            pad_s[PAD + c * CHUNK:PAD + (c + 1) * CHUNK, off:off + width] = (
                src_ref[c * CHUNK:(c + 1) * CHUNK, :].astype(F32))

    def conv_slab(w_ref, bias_ref, dst_s, off, c, s, zero_row):
        ls = slice(s * LANES, (s + 1) * LANES)
        ps = slice(off + s * LANES, off + (s + 1) * LANES)
        acc = jnp.broadcast_to(bias_ref[:, ls] + zero_row, (CHUNK, LANES))
        for k in range(D_CONV):
            start = PAD + c * CHUNK + k - D_CONV // 2
            acc = acc + w_ref[k:k + 1, ls] * pad_s[start:start + CHUNK, ps]
        dst_s[c * CHUNK:(c + 1) * CHUNK, ls] = _silu(acc)

    conv_items = [functools.partial(conv_slab, w_ref, bias_ref, dst_s, off, c, s)
                  for _, w_ref, bias_ref, dst_s, off, width in conv_srcs
                  for c in range(nc) for s in range(width // LANES)]

    tile = 2 * LANES

    def zero_row_of(res):
        bits = pltpu.bitcast(res[0:SUBLANES, 0:LANES], jnp.uint32)
        return pltpu.bitcast((bits >> 16) >> 16, F32)[0:1, :]

    def cum_tile(t):
        ls = slice(t * tile, (t + 1) * tile)
        res = _dot(p1_ref[...], cbc_ref[:, ls])
        cum_s[:, ls] = res
        return zero_row_of(res)

    def exp_tile(e, t):
        ls = slice(t * tile, (t + 1) * tile)
        res = _dot(p2_ref[...], eexp_ref[e, :, ls])
        exp_s[e, :, ls] = res
        return zero_row_of(res)

    spread_items = ([functools.partial(cum_tile, t) for t in range(N_HD * LANES // tile)]
                    + [functools.partial(exp_tile, e, t) for e in range(N_EXPAND)
                       for t in range(GROUP_WIDTH // tile)])

    merged = sorted([((i + 0.5) / len(conv_items), 0, f) for i, f in enumerate(conv_items)]
                    + [((i + 0.5) / len(spread_items), 1, f) for i, f in enumerate(spread_items)],
                    key=lambda item: item[:2])
    zero_row = jnp.zeros((1, LANES), F32)
    for _, is_spread, emit in merged:
        if is_spread:
            zero_row = emit()
        else:
            emit(zero_row)

    if has_init:
        sft_s[...] = s0f_ref[0].T
        sbt_s[...] = s0b_ref[0].T
    else:
        sft_s[...] = jnp.zeros_like(sft_s)
        sbt_s[...] = jnp.zeros_like(sbt_s)

    ii = lax.broadcasted_iota(jnp.int32, (CHUNK, CHUNK), 0)
    jj = lax.broadcasted_iota(jnp.int32, (CHUNK, CHUNK), 1)
    lower = jj <= ii
    diag = jj == ii
    left = jj < SSD_HEADDIM

    def fwd_chunk(c, carry):
        r0 = pl.multiple_of(c * CHUNK, CHUNK)
        rows = pl.ds(r0, CHUNK)
        xq = xc_s[rows, :]
        bq = bc_s[rows, :]
        cq = cc_s[rows, :].astype(BF16)
        rowa = row_ref[c, 0]
        rowd = row_ref[c, 1]
        cb = _dot_nt(cq, bq.astype(BF16))
        y_off = _dot(cq, sft_s[...].astype(BF16)) * exp_s[0, rows, :]
        y_parts = []
        for k in range(HEADS_PER_GROUP // 2):
            ms = []
            for r in (2 * k, 2 * k + 1):
                rb = HEADS_PER_GROUP + r
                seg_f = cum_s[rows, r * LANES:(r + 1) * LANES] - rowa[r:r + 1, :]
                seg_b = cum_s[rows, rb * LANES:(rb + 1) * LANES] - rowa[rb:rb + 1, :]
                dm = jnp.exp(jnp.where(lower, seg_f, seg_b))
                dm = dm + jnp.where(diag, rowd[rb:rb + 1, :], 0.0)
                ms.append((cb * dm).astype(BF16))
            lhs = jnp.concatenate(ms, axis=1)
            xp = xq[:, k * LANES:(k + 1) * LANES]
            rhs = jnp.concatenate([jnp.where(left, xp, 0.0), jnp.where(left, 0.0, xp)],
                                  axis=0).astype(BF16)
            y_parts.append(_dot(lhs, rhs))
        yacc_s[rows, :] = jnp.concatenate(y_parts, axis=1) + y_off
        decay = exp_s[0, pl.ds(r0 + CHUNK - 1, 1), :]
        bt = bq.T.astype(BF16)
        bt_s[c] = bt
        sft_s[...] = sft_s[...] * decay + _dot(bt, (xq * exp_s[1, rows, :]).astype(BF16))
        return carry

    lax.fori_loop(0, nc, fwd_chunk, 0)

    def bwd_chunk(t, carry):
        c = nc - 1 - t
        r0 = pl.multiple_of(c * CHUNK, CHUNK)
        rows = pl.ds(r0, CHUNK)
        xq = xc_s[rows, :]
        cq = cc_s[rows, :].astype(BF16)
        y = (yacc_s[rows, :] + _dot(cq, sbt_s[...].astype(BF16)) * exp_s[2, rows, :]
             + dsk_ref[...] * xq)
        y = y * _silu(z_ref[rows, :].astype(F32))
        ms = jnp.mean(y * y, axis=-1, keepdims=True)
        y_ref[rows, :] = (y * lax.rsqrt(ms + EPS) * ng_ref[...]).astype(BF16)
        decay = exp_s[2, pl.ds(r0, 1), :]
        sbt_s[...] = sbt_s[...] * decay + _dot(bt_s[c], (xq * exp_s[3, rows, :]).astype(BF16))
        return carry

    lax.fori_loop(0, nc, bwd_chunk, 0)

    if emit_state:
        sf_ref[0] = sft_s[...].T
        sb_ref[0] = sbt_s[...].T


def _ssd(act, p1, p2, rowp, consts, conv_w, conv_b, dsk, ng, *, layer, L, nseq, row_block0,
         s0f=None, s0b=None, y_prev=None, state_prev=None):
    m = act.shape[0]
    depth = conv_w.shape[0]
    l = layer
    has_init = s0f is not None
    emit_state = not has_init
    nc = L // CHUNK
    gw = GROUP_WIDTH
    rb = row_block0
    off_b = SSD_WIDTH // D_STATE
    off_c = (SSD_WIDTH + BC_WIDTH) // D_STATE
    in_specs = [
        pl.BlockSpec((L, gw), lambda b, g: (rb + b, COL_XBC // gw + g)),
        pl.BlockSpec((L, D_STATE), lambda b, g: (rb + b, COL_XBC // D_STATE + off_b + g)),
        pl.BlockSpec((L, D_STATE), lambda b, g: (rb + b, COL_XBC // D_STATE + off_c + g)),
        pl.BlockSpec((L, gw), lambda b, g: (rb + b, COL_ZS // gw + g)),
        pl.BlockSpec((L, LANES), lambda b, g: (rb + b, g)),
        pl.BlockSpec((L, LANES), lambda b, g: (rb + b, g)),
        pl.BlockSpec((nc, 2, N_HD, CHUNK), lambda b, g: (rb + b, 0, g, 0)),
        pl.BlockSpec((LANES, N_HD * LANES), lambda b, g: (0, 0)),
        pl.BlockSpec((N_EXPAND, LANES, gw), lambda b, g: (0, 0, 0)),
        pl.BlockSpec((None, D_CONV, gw), lambda b, g: (l, 0, g)),
        pl.BlockSpec((None, D_CONV, D_STATE), lambda b, g: (l, 0, off_b + g)),
        pl.BlockSpec((None, D_CONV, D_STATE), lambda b, g: (l, 0, off_c + g)),
        pl.BlockSpec((None, 1, gw), lambda b, g: (l, 0, g)),
        pl.BlockSpec((None, 1, D_STATE), lambda b, g: (l, 0, off_b + g)),
        pl.BlockSpec((None, 1, D_STATE), lambda b, g: (l, 0, off_c + g)),
        pl.BlockSpec((None, 1, gw), lambda b, g: (l, 0, g)),
        pl.BlockSpec((None, 1, gw), lambda b, g: (l, 0, g)),
    ]
    args = [act, act, act, act, p1, p2, rowp, consts["cbc"], consts["eexp"], conv_w, conv_w, conv_w,
            conv_b, conv_b, conv_b, dsk, ng]
    aliases = {}
    n_alias = 0
    state_spec = pl.BlockSpec((1, None, gw, D_STATE), lambda b, g: (b, l, g, 0))
    if has_init:
        in_specs += [state_spec, state_spec, pl.BlockSpec(memory_space=pl.ANY)]
        args += [s0f, s0b, y_prev]
        aliases = {len(args) - 1: 0}
        n_alias = 1
    elif state_prev is not None:
        in_specs += [pl.BlockSpec(memory_space=pl.ANY)] * 2
        args += list(state_prev)
        aliases = {len(args) - 2: 1, len(args) - 1: 2}
        n_alias = 2
    out_specs = [pl.BlockSpec((L, gw), lambda b, g: (rb + b, g))]
    out_shape = [jax.ShapeDtypeStruct((m, SSD_WIDTH), BF16)]
    if emit_state:
        out_specs += [state_spec] * 2
        out_shape += [jax.ShapeDtypeStruct((nseq, depth, SSD_WIDTH, D_STATE), F32)] * 2
    scratch = [
        pltpu.VMEM((L + 2 * PAD, gw + 2 * D_STATE), F32),
        pltpu.VMEM((L, gw), F32),
        pltpu.VMEM((L, D_STATE), F32),
        pltpu.VMEM((L, D_STATE), F32),
        pltpu.VMEM((nc, D_STATE, CHUNK), BF16),
        pltpu.VMEM((L, N_HD * LANES), F32),
        pltpu.VMEM((N_EXPAND, L, gw), F32),
        pltpu.VMEM((L, gw), F32),
        pltpu.VMEM((D_STATE, gw), F32),
        pltpu.VMEM((D_STATE, gw), F32),
    ]
    kern = functools.partial(_ssd_kernel, L=L, has_init=has_init, emit_state=emit_state,
                             n_alias=n_alias)
    return pl.pallas_call(
        kern,
        grid=(nseq, N_SSD_GROUPS),
        in_specs=in_specs,
        out_specs=out_specs,
        out_shape=out_shape,
        scratch_shapes=scratch,
        input_output_aliases=aliases,
        compiler_params=_cparams(("arbitrary", "arbitrary")),
        name="ssd_latent" if has_init else "ssd_context",
    )(*args)


ATTN_SCALE = HEAD_DIM ** -0.5


def _ctx_attn_kernel(sink_ref, q_ref, k_ref, v_ref, z_ref, o_ref):
    g = pl.program_id(1)
    k = k_ref[...]
    v = v_ref[...]
    for r in range(Q_PER_KV):
        ls = slice(r * HEAD_DIM, (r + 1) * HEAD_DIM)
        sink = sink_ref[g * Q_PER_KV + r]
        s = _dot_nt(q_ref[:, ls], k) * ATTN_SCALE
        m = jnp.maximum(jnp.max(s, axis=-1, keepdims=True), sink)
        p = jnp.exp(s - m)
        denom = jnp.sum(p, axis=-1, keepdims=True) + jnp.exp(sink - m)
        o = _dot(p.astype(BF16), v) / denom
        o_ref[:, ls] = (o * _silu(z_ref[:, ls].astype(F32))).astype(BF16)


def _ctx_attention(act, sink, *, L, nseq):
    m = act.shape[0]
    gw = Q_PER_KV * HEAD_DIM
    return pl.pallas_call(
        _ctx_attn_kernel,
        grid=(nseq, N_KV_HEADS),
        in_specs=[
            pl.BlockSpec(memory_space=pltpu.SMEM),
            pl.BlockSpec((L, gw), lambda b, g: (b, COL_Q // gw + g)),
            pl.BlockSpec((L, HEAD_DIM), lambda b, g: (b, COL_K // HEAD_DIM + g)),
            pl.BlockSpec((L, HEAD_DIM), lambda b, g: (b, COL_V // HEAD_DIM + g)),
            pl.BlockSpec((L, gw), lambda b, g: (b, COL_ZA // gw + g)),
        ],
        out_specs=pl.BlockSpec((L, gw), lambda b, g: (b, g)),
        out_shape=jax.ShapeDtypeStruct((m, ATTN_WIDTH), BF16),
        compiler_params=_cparams(("arbitrary", "arbitrary")),
        name="attn_context",
    )(sink, act, act, act, act)


def _rope_tables(length):
    sec = HEAD_DIM // 2
    half = sec // 2
    d = np.arange(HEAD_DIM)
    e = d % sec
    freqs = ROPE_BASE ** (-(e % half).astype(np.float64) / half)
    t = np.arange(length)
    pos = np.where((d // sec)[None, :] == 0, (t // GRID_W)[:, None], (t % GRID_W)[:, None])
    ang = pos.astype(np.float64) * freqs[None, :]
    sign = np.where(e < half, -1.0, 1.0)[None, :]
    return (jnp.asarray(np.cos(ang), F32), jnp.asarray(np.sin(ang) * sign, F32))


def _rope(x, cos, sin_signed, first_half):
    partner = jnp.where(first_half, pltpu.roll(x, LANES - HEAD_DIM // 4, 1),
                        pltpu.roll(x, HEAD_DIM // 4, 1))
    return x * cos + partner * sin_signed


def _lat_attn_kernel(sink_ref, q_ref, k_ref, v_ref, z_ref, kc_ref, vc_ref, cos_ref, sin_ref,
                     _oprev_ref, o_ref, kctx_s, vctx_s, keys_s, vals_s, *, L, lc):
    g = pl.program_id(1)
    nb = L // CHUNK
    win = 3 * CHUNK
    rows4 = Q_PER_KV * CHUNK
    lane = lax.broadcasted_iota(jnp.int32, (CHUNK, HEAD_DIM), 1)
    first_half = (lane % (HEAD_DIM // 2)) < (HEAD_DIM // 4)

    kctx_s[...] = kc_ref[...].astype(BF16)
    vctx_s[...] = vc_ref[...].astype(BF16)
    zero_blk = jnp.zeros((CHUNK, HEAD_DIM), BF16)
    for dst in (keys_s, vals_s):
        dst[0:CHUNK, :] = zero_blk
        dst[CHUNK + L:2 * CHUNK + L, :] = zero_blk
    for n in range(nb):
        rows = slice(n * CHUNK, (n + 1) * CHUNK)
        kr = _rope(k_ref[rows, :].astype(F32), cos_ref[rows, :], sin_ref[rows, :], first_half)
        keys_s[CHUNK + n * CHUNK:CHUNK + (n + 1) * CHUNK, :] = kr.astype(BF16)
        vals_s[CHUNK + n * CHUNK:CHUNK + (n + 1) * CHUNK, :] = v_ref[rows, :]

    qi = lax.broadcasted_iota(jnp.int32, (rows4, win), 0) % CHUNK
    wi = lax.broadcasted_iota(jnp.int32, (rows4, win), 1)
    band = jnp.abs(qi - wi + CHUNK) <= WINDOW
    head = lax.broadcasted_iota(jnp.int32, (rows4, 1), 0) // CHUNK
    sink = jnp.zeros((rows4, 1), F32)
    for r in range(Q_PER_KV):
        sink = jnp.where(head == r, sink_ref[g * Q_PER_KV + r], sink)

    def block(n, carry):
        r0 = pl.multiple_of(n * CHUNK, CHUNK)
        rows = pl.ds(r0, CHUNK)
        cos = cos_ref[rows, :]
        sin = sin_ref[rows, :]
        q = jnp.concatenate(
            [_rope(q_ref[rows, r * HEAD_DIM:(r + 1) * HEAD_DIM].astype(F32), cos, sin,
                   first_half).astype(BF16) for r in range(Q_PER_KV)], axis=0)
        in_seq = (wi >= CHUNK - r0) & (wi < L + CHUNK - r0)
        s_ctx = _dot_nt(q, kctx_s[...]) * ATTN_SCALE
        s_lat = _dot_nt(q, keys_s[pl.ds(r0, win), :]) * ATTN_SCALE
        s_lat = jnp.where(in_seq, jnp.where(band, s_lat, -jnp.inf), -jnp.inf)
        m = jnp.maximum(jnp.maximum(jnp.max(s_ctx, axis=-1, keepdims=True),
                                    jnp.max(s_lat, axis=-1, keepdims=True)), sink)
        p_ctx = jnp.exp(s_ctx - m)
        p_lat = jnp.exp(s_lat - m)
        denom = (jnp.sum(p_ctx, axis=-1, keepdims=True) + jnp.sum(p_lat, axis=-1, keepdims=True)
                 + jnp.exp(sink - m))
        o = (_dot(p_ctx.astype(BF16), vctx_s[...])
             + _dot(p_lat.astype(BF16), vals_s[pl.ds(r0, win), :])) / denom
        for r in range(Q_PER_KV):
            ls = slice(r * HEAD_DIM, (r + 1) * HEAD_DIM)
            o_ref[rows, ls] = (o[r * CHUNK:(r + 1) * CHUNK, :]
                               * _silu(z_ref[rows, ls].astype(F32))).astype(BF16)
        return carry

    lax.fori_loop(0, nb, block, 0)


def _lat_attention(act, sink, cache_k, cache_v, o_prev, *, L, nseq, row_block0, layer):
    m = act.shape[0]
    gw = Q_PER_KV * HEAD_DIM
    lc = cache_k.shape[2]
    rb = row_block0
    cos, sin = _rope_tables(L)
    kc = cache_k.reshape(cache_k.shape[0], cache_k.shape[1], lc, KV_WIDTH)
    vc = cache_v.reshape(cache_v.shape[0], cache_v.shape[1], lc, KV_WIDTH)
    kern = functools.partial(_lat_attn_kernel, L=L, lc=lc)
    tab = lambda b, g: (0, 0)
    return pl.pallas_call(
        kern,
        grid=(nseq, N_KV_HEADS),
        in_specs=[
            pl.BlockSpec(memory_space=pltpu.SMEM),
            pl.BlockSpec((L, gw), lambda b, g: (rb + b, COL_Q // gw + g)),
            pl.BlockSpec((L, HEAD_DIM), lambda b, g: (rb + b, COL_K // HEAD_DIM + g)),
            pl.BlockSpec((L, HEAD_DIM), lambda b, g: (rb + b, COL_V // HEAD_DIM + g)),
            pl.BlockSpec((L, gw), lambda b, g: (rb + b, COL_ZA // gw + g)),
            pl.BlockSpec((None, None, lc, HEAD_DIM), lambda b, g: (b, layer, 0, g)),
            pl.BlockSpec((None, None, lc, HEAD_DIM), lambda b, g: (b, layer, 0, g)),
            pl.BlockSpec((L, HEAD_DIM), tab),
            pl.BlockSpec((L, HEAD_DIM), tab),
            pl.BlockSpec(memory_space=pl.ANY),
        ],
        out_specs=pl.BlockSpec((L, gw), lambda b, g: (rb + b, g)),
        out_shape=jax.ShapeDtypeStruct((m, ATTN_WIDTH), BF16),
        scratch_shapes=[pltpu.VMEM((lc, HEAD_DIM), BF16), pltpu.VMEM((lc, HEAD_DIM), BF16),
                        pltpu.VMEM((L + 2 * CHUNK, HEAD_DIM), BF16),
                        pltpu.VMEM((L + 2 * CHUNK, HEAD_DIM), BF16)],
        input_output_aliases={9: 0},
        compiler_params=_cparams(("arbitrary", "arbitrary")),
        name="attn_latent",
    )(sink, act, act, act, act, kc, vc, cos, sin, o_prev)


def _branch_kernel(oa_ref, ys_ref, wpa_ref, wps_ref, ga_ref, gs_ref, o_ref):
    a = _dot(oa_ref[...], wpa_ref[...])
    s = _dot(ys_ref[...], wps_ref[...])
    merged = _sigmoid(ga_ref[...].astype(F32)) * a + _sigmoid(gs_ref[...].astype(F32)) * s
    o_ref[...] = merged.astype(BF16)


def _branches(oa, ys, w_pa, w_ps, act, *, layer, tm):
    m = oa.shape[0]
    tn = 512
    l = layer
    return pl.pallas_call(
        _branch_kernel,
        grid=(m // tm, D_MODEL // tn),
        in_specs=[
            pl.BlockSpec((tm, ATTN_WIDTH), lambda i, j: (i, 0)),
            pl.BlockSpec((tm, SSD_WIDTH), lambda i, j: (i, 0)),
            pl.BlockSpec((None, ATTN_WIDTH, tn), lambda i, j: (l, 0, j)),
            pl.BlockSpec((None, SSD_WIDTH, tn), lambda i, j: (l, 0, j)),
            pl.BlockSpec((tm, tn), lambda i, j: (i, COL_GA // tn + j)),
            pl.BlockSpec((tm, tn), lambda i, j: (i, COL_GS // tn + j)),
        ],
        out_specs=pl.BlockSpec((tm, tn), lambda i, j: (i, j)),
        out_shape=jax.ShapeDtypeStruct((m, D_MODEL), BF16),
        compiler_params=_cparams(("arbitrary", "arbitrary")),
        name="branches",
    )(oa, ys, w_pa, w_ps, act, act)


def _out_kernel(*refs, final, na):
    if final:
        mg_ref, w_ref, xa_ref, xb_ref, gate_ref, fg_ref, ya_ref, yb_ref = refs
    else:
        mg_ref, w_ref, xa_ref, xb_ref, gate_ref, o_ref = refs
    is_ctx = pl.program_id(0) < na
    x = jnp.where(is_ctx, xa_ref[...], xb_ref[...])
    y = x + gate_ref[...] * _dot(mg_ref[...], w_ref[...])
    if not final:
        o_ref[...] = y
        return
    ms = jnp.mean(y * y, axis=-1, keepdims=True)
    y = y * lax.rsqrt(ms + EPS) * fg_ref[...]

    @pl.when(is_ctx)
    def _():
        ya_ref[...] = y

    @pl.when(jnp.logical_not(is_ctx))
    def _():
        yb_ref[...] = y


def _out_proj(merged, w_out, xa, xb, xb_offset, mod, final_g, *, layer, tm, na, group_of):
    m = merged.shape[0]
    l = layer
    final = final_g is not None
    in_specs = [
        pl.BlockSpec((tm, D_MODEL), lambda i: (i, 0)),
        pl.BlockSpec((None, D_MODEL, D_MODEL), lambda i: (l, 0, 0)),
    ] + _row_split_specs(tm, na, xb_offset) + [
        pl.BlockSpec((None, None, 1, D_MODEL), lambda i: (l, group_of(i, tm), 0, 2)),
    ]
    args = [merged, w_out, xa, xb, mod]
    if final:
        in_specs.append(pl.BlockSpec((1, D_MODEL), lambda i: (0, 0)))
        args.append(final_g.reshape(1, D_MODEL))
        out_specs = [
            pl.BlockSpec((tm, D_MODEL), lambda i: (jnp.minimum(i, na - 1), 0)),
            pl.BlockSpec((tm, D_MODEL), lambda i: (jnp.maximum(i - na, 0), 0)),
        ]
        out_shape = [jax.ShapeDtypeStruct((na * tm, D_MODEL), F32),
                     jax.ShapeDtypeStruct((m - na * tm, D_MODEL), F32)]
    else:
        out_specs = pl.BlockSpec((tm, D_MODEL), lambda i: (i, 0))
        out_shape = jax.ShapeDtypeStruct((m, D_MODEL), F32)
    return pl.pallas_call(
        functools.partial(_out_kernel, final=final, na=na),
        grid=(m // tm,),
        in_specs=in_specs,
        out_specs=out_specs,
        out_shape=out_shape,
        compiler_params=_cparams(("arbitrary",)),
        name="out_proj_final" if final else "out_proj",
    )(*args)


def _dt_permutation():
    perm = np.zeros(DT_WIDTH, np.int32)
    for g in range(N_SSD_GROUPS):
        for d in range(2):
            for r in range(HEADS_PER_GROUP):
                perm[g * 16 + d * HEADS_PER_GROUP + r] = d * N_SSD_HEADS + g * HEADS_PER_GROUP + r
    return perm


def kernel(x_prompt, x_sample, c, cache_k, cache_v, state_ssm_fwd, state_ssm_bwd, c_ctx, norm_g, w_mod, b_mod, w_in, conv_w, conv_b, attn_sink, a_log_fwd, a_log_bwd, dt_bias_fwd, dt_bias_bwd, d_skip, ssd_norm_g, w_pa, w_ps, w_out, final_norm_g):
    bc, lc, _ = x_prompt.shape
    bl, ll, _ = x_sample.shape
    depth = w_in.shape[0]
    n_ctx = bc * lc
    m = n_ctx + bl * ll
    assert n_ctx % ll == 0 and ll % lc == 0 and lc % CHUNK == 0
    assert 1 + bl <= COND_ROWS

    def group_of(i, tm):
        return jnp.maximum(i * tm - n_ctx + ll, 0) // ll

    tm_big = math.gcd(1024, math.gcd(n_ctx, ll))
    tm_small = math.gcd(512, tm_big)

    cond = jnp.zeros((COND_ROWS, D_MODEL), F32).at[0].set(c_ctx).at[1:1 + bl].set(c)
    mod = _modulation(cond, w_mod, b_mod)

    perm = _dt_permutation()
    consts = _ssd_constants()
    wdt = w_in[:, :, W_IN_DT:W_IN_DT + DT_WIDTH][:, :, perm]
    wdtt = jnp.swapaxes(wdt, 1, 2)
    bias = jnp.concatenate([dt_bias_fwd, dt_bias_bwd], axis=1)[:, perm]
    alog = jnp.concatenate([a_log_fwd, a_log_bwd], axis=1)[:, perm]
    norm_g3 = norm_g.reshape(depth, 1, D_MODEL)
    conv_b3 = conv_b.reshape(depth, 1, CONV_WIDTH)
    dsk = jnp.repeat(d_skip, SSD_HEADDIM, axis=1).reshape(depth, 1, SSD_WIDTH)
    ng = ssd_norm_g.reshape(depth, 1, SSD_WIDTH)
    w_pa_bf = w_pa.astype(BF16)
    w_ps_bf = w_ps.astype(BF16)
    w_out_bf = w_out.astype(BF16)
    s0f = state_ssm_fwd.reshape(bl, depth, SSD_WIDTH, D_STATE)
    s0b = state_ssm_bwd.reshape(bl, depth, SSD_WIDTH, D_STATE)

    na_small = n_ctx // tm_small
    na_big = n_ctx // tm_big
    xa, xb, xb_off = x_prompt.reshape(n_ctx, D_MODEL), x_sample.reshape(bl * ll, D_MODEL), 0
    k_new = v_new = states = None
    for l in range(depth):
        h, p1, p2, rowp = _prep(xa, xb, xb_off, m, mod, norm_g3, wdtt, bias, alog, consts,
                                layer=l, tm=tm_small, na=na_small, group_of=group_of)
        act, k_new, v_new = _inproj(h, w_in, k_new, v_new, layer=l, tm=tm_big, na=na_big, lc=lc,
                                    bc=bc)

        sink = attn_sink[l]
        oa = _ctx_attention(act, sink, L=lc, nseq=bc)
        oa = _lat_attention(act, sink, cache_k, cache_v, oa, L=ll, nseq=bl,
                            row_block0=n_ctx // ll, layer=l)

        ys, sf, sb = _ssd(act, p1, p2, rowp, consts, conv_w, conv_b3, dsk, ng, layer=l, L=lc,
                          nseq=bc, row_block0=0, state_prev=states)
        states = (sf, sb)
        (ys,) = _ssd(act, p1, p2, rowp, consts, conv_w, conv_b3, dsk, ng, layer=l, L=ll, nseq=bl,
                     row_block0=n_ctx // ll, s0f=s0f, s0b=s0b, y_prev=ys)

        merged = _branches(oa, ys, w_pa_bf, w_ps_bf, act, layer=l, tm=tm_big)
        last = l == depth - 1
        res = _out_proj(merged, w_out_bf, xa, xb, xb_off, mod, final_norm_g if last else None,
                        layer=l, tm=tm_small, na=na_small, group_of=group_of)
        if not last:
            xa, xb, xb_off = res, res, na_small

    y_prompt = res[0].reshape(bc, lc, D_MODEL)
    y_sample = res[1].reshape(bl, ll, D_MODEL)
    shape_kv = (bc, depth, lc, N_KV_HEADS, HEAD_DIM)
    shape_st = (bc, depth, N_SSD_HEADS, SSD_HEADDIM, D_STATE)
    return (y_prompt, y_sample, k_new.reshape(shape_kv), v_new.reshape(shape_kv),
            states[0].reshape(shape_st), states[1].reshape(shape_st))
```

```python
import functools
import math

import numpy as np
import jax
import jax.numpy as jnp
from jax import lax
from jax.experimental import pallas as pl
from jax.experimental.pallas import tpu as pltpu

F32 = jnp.float32
BF16 = jnp.bfloat16

D_MODEL = 2048
HEAD_DIM = 128
N_Q_HEADS = 16
N_KV_HEADS = 4
Q_PER_KV = 4
ATTN_WIDTH = 2048
KV_WIDTH = 512
WINDOW = 128
GRID_W = 64
ROPE_BASE = 10000.0
SSD_WIDTH = 4096
SSD_HEADDIM = 64
N_SSD_HEADS = 64
D_STATE = 128
N_SSD_GROUPS = 8
HEADS_PER_GROUP = 8
GROUP_WIDTH = SSD_WIDTH // N_SSD_GROUPS
CHUNK = 128
D_CONV = 5
BC_WIDTH = 1024
CONV_WIDTH = 6144
EPS = 1e-6
MOD_WIDTH = 3 * D_MODEL

COL_Q = 0
COL_K = 2048
COL_V = 2560
COL_ZA = 3072
COL_XBC = 5120
COL_ZS = 11264
COL_GA = 15360
COL_GS = 17408
ACT_WIDTH = 19456
W_IN_DT = 15360
DT_WIDTH = 2 * N_SSD_HEADS

LANES = 128
SUBLANES = 8
VMEM_LIMIT = 56 * 1024 * 1024

COND_ROWS = 8


def _cparams(sem):
    return pltpu.CompilerParams(dimension_semantics=sem, vmem_limit_bytes=VMEM_LIMIT)


def _dot(a, b):
    return jnp.dot(a, b, preferred_element_type=F32)


def _dot_nt(a, b):
    return lax.dot_general(a, b, (((1,), (1,)), ((), ())), preferred_element_type=F32)


def _split2(x):
    hi = x.astype(BF16)
    lo = (x - hi.astype(F32)).astype(BF16)
    return hi, lo


def _split3(x):
    p1 = x.astype(BF16)
    r1 = x - p1.astype(F32)
    p2 = r1.astype(BF16)
    p3 = (r1 - p2.astype(F32)).astype(BF16)
    return p1, p2, p3


def _dot3(a, b):
    ah, al = _split2(a)
    bh, bl = _split2(b)
    return _dot(ah, bh) + _dot(al, bh) + _dot(ah, bl)


def _sigmoid(x):
    return 0.5 + 0.5 * jnp.tanh(0.5 * x)


def _silu(x):
    half = 0.5 * x
    return half + half * jnp.tanh(half)


def _softplus(x):
    return jnp.maximum(x, 0.0) + jnp.log1p(jnp.exp(-jnp.abs(x)))


def _mod_kernel(cond_ref, w_ref, b_ref, o_ref):
    res = _dot3(_silu(cond_ref[...]), w_ref[...]) + b_ref[...]
    for r in range(COND_ROWS):
        o_ref[r] = res[r:r + 1, :]


def _modulation(cond, w_mod, b_mod):
    depth = w_mod.shape[0]
    tn = 512
    return pl.pallas_call(
        _mod_kernel,
        grid=(depth, MOD_WIDTH // tn),
        in_specs=[
            pl.BlockSpec((COND_ROWS, D_MODEL), lambda l, j: (0, 0)),
            pl.BlockSpec((None, D_MODEL, tn), lambda l, j: (l, 0, j)),
            pl.BlockSpec((None, 1, tn), lambda l, j: (l, 0, j)),
        ],
        out_specs=pl.BlockSpec((None, COND_ROWS, 1, tn), lambda l, j: (l, 0, 0, j)),
        out_shape=jax.ShapeDtypeStruct((depth, COND_ROWS, 1, MOD_WIDTH), F32),
        compiler_params=_cparams(("arbitrary", "arbitrary")),
        name="modulation",
    )(cond, w_mod, b_mod.reshape(depth, 1, MOD_WIDTH))


def _prep_kernel(xa_ref, xb_ref, shift_ref, scale_ref, g_ref, wdtt_ref, biast_ref, alogt_ref,
                 sel1_ref, sel2_ref, h_ref, p1_ref, p2_ref, row_ref, *, tm, na):
    x = jnp.where(pl.program_id(0) < na, xa_ref[...], xb_ref[...])
    ms = jnp.mean(x * x, axis=-1, keepdims=True)
    h = (x * lax.rsqrt(ms + EPS) * g_ref[...]) * (1.0 + scale_ref[...]) + shift_ref[...]
    h_ref[...] = h.astype(BF16)

    hh, hl = _split2(h)
    wth, wtl = _split2(wdtt_ref[...])
    rawt = _dot_nt(wth, hh) + _dot_nt(wth, hl) + _dot_nt(wtl, hh)
    dtt = _softplus(rawt + biast_ref[...])
    dtat = dtt * (-jnp.exp(alogt_ref[...]))

    ii = lax.broadcasted_iota(jnp.int32, (CHUNK, CHUNK), 0)
    kk = lax.broadcasted_iota(jnp.int32, (CHUNK, CHUNK), 1)
    lt = jnp.where(kk <= ii, 1.0, 0.0).astype(BF16)
    ut = jnp.where(kk >= ii, 1.0, 0.0).astype(BF16)
    fwd_row = (ii % 16) < HEADS_PER_GROUP

    for c in range(tm // CHUNK):
        rows = slice(c * CHUNK, (c + 1) * CHUNK)
        dt_t = dtt[:, rows]
        q1, q2, q3 = _split3(dtat[:, rows])
        pre_t = _dot(q1, ut) + _dot(q2, ut) + _dot(q3, ut)
        suf_t = _dot(q1, lt) + _dot(q2, lt) + _dot(q3, lt)
        acs_t = jnp.where(fwd_row, pre_t, suf_t)
        edge_t = jnp.where(fwd_row, acs_t[:, CHUNK - 1:CHUNK], acs_t[:, 0:1])
        w1_t = dt_t * jnp.exp(edge_t - acs_t)
        row_ref[c, 0] = acs_t - jnp.log(dt_t)
        row_ref[c, 1] = dt_t
        acs = acs_t.T
        p1 = _dot(jnp.concatenate(_split3(acs), axis=1), sel1_ref[...])
        p1_ref[rows, :] = p1.astype(BF16)
        p2 = _dot(jnp.concatenate(_split3(jnp.exp(acs)) + _split3(w1_t.T), axis=1), sel2_ref[...])
        p2_ref[rows, :] = p2.astype(BF16)


def _row_split_specs(tm, na, xb_offset):
    return [
        pl.BlockSpec((tm, D_MODEL), lambda i: (jnp.minimum(i, na - 1), 0)),
        pl.BlockSpec((tm, D_MODEL), lambda i: (xb_offset + jnp.maximum(i - na, 0), 0)),
    ]


def _prep(xa, xb, xb_offset, m, mod, norm_g, wdtt, bias, alog, consts, *, layer, tm, na, group_of):
    l = layer
    kern = functools.partial(_prep_kernel, tm=tm, na=na)
    par2 = lambda i: (l, 0, 0)
    whole = lambda i: (0, 0)
    pw = N_SSD_GROUPS * LANES
    return pl.pallas_call(
        kern,
        grid=(m // tm,),
        in_specs=_row_split_specs(tm, na, xb_offset) + [
            pl.BlockSpec((None, None, 1, D_MODEL), lambda i: (l, group_of(i, tm), 0, 0)),
            pl.BlockSpec((None, None, 1, D_MODEL), lambda i: (l, group_of(i, tm), 0, 1)),
            pl.BlockSpec((None, 1, D_MODEL), par2),
            pl.BlockSpec((None, DT_WIDTH, D_MODEL), par2),
            pl.BlockSpec((None, DT_WIDTH, 1), par2),
            pl.BlockSpec((None, DT_WIDTH, 1), par2),
            pl.BlockSpec((N_PIECES * LANES, pw), whole),
            pl.BlockSpec((2 * N_PIECES * LANES, pw), whole),
        ],
        out_specs=[
            pl.BlockSpec((tm, D_MODEL), lambda i: (i, 0)),
            pl.BlockSpec((tm, pw), lambda i: (i, 0)),
            pl.BlockSpec((tm, pw), lambda i: (i, 0)),
            pl.BlockSpec((tm // CHUNK, 2, DT_WIDTH, CHUNK), lambda i: (i, 0, 0, 0)),
        ],
        out_shape=[
            jax.ShapeDtypeStruct((m, D_MODEL), BF16),
            jax.ShapeDtypeStruct((m, pw), BF16),
            jax.ShapeDtypeStruct((m, pw), BF16),
            jax.ShapeDtypeStruct((m // CHUNK, 2, DT_WIDTH, CHUNK), F32),
        ],
        compiler_params=_cparams(("arbitrary",)),
        name="prep",
    )(xa, xb, mod, mod, norm_g, wdtt, bias[:, :, None], alog[:, :, None], consts["sel1"],
      consts["sel2"])


N_PIECES = 3
N_HD = 2 * HEADS_PER_GROUP
PIECE_LANES = N_PIECES * N_HD
N_EXPAND = 4


def _ssd_constants():
    pw = N_SSD_GROUPS * LANES
    sel1 = np.zeros((N_PIECES * LANES, pw), np.float32)
    sel2 = np.zeros((2 * N_PIECES * LANES, pw), np.float32)
    for g in range(N_SSD_GROUPS):
        for hd in range(N_HD):
            lam = g * N_HD + hd
            for p in range(N_PIECES):
                sel1[p * LANES + lam, g * LANES + N_PIECES * hd + p] = 1.0
                for q in range(2):
                    sel2[(q * N_PIECES + p) * LANES + lam,
                         g * LANES + q * PIECE_LANES + N_PIECES * hd + p] = 1.0
    cbc = np.zeros((LANES, N_HD * LANES), np.float32)
    for hd in range(N_HD):
        cbc[N_PIECES * hd:N_PIECES * (hd + 1), hd * LANES:(hd + 1) * LANES] = 1.0
    eexp = np.zeros((N_EXPAND, LANES, GROUP_WIDTH), np.float32)
    for e, (q, d) in enumerate(((0, 0), (1, 0), (0, 1), (1, 1))):
        for r in range(HEADS_PER_GROUP):
            row0 = q * PIECE_LANES + N_PIECES * (d * HEADS_PER_GROUP + r)
            eexp[e, row0:row0 + N_PIECES, r * SSD_HEADDIM:(r + 1) * SSD_HEADDIM] = 1.0
    return {k: jnp.asarray(v, BF16) for k, v in
            (("sel1", sel1), ("sel2", sel2), ("cbc", cbc), ("eexp", eexp))}


INPROJ_TN = 1024
KV_BLOCK = COL_K // INPROJ_TN


def _inproj_kernel(*refs, na, lc, aliased):
    if aliased:
        h_ref, w_ref, _kprev, _vprev, act_ref, k_ref, v_ref, wbf_s = refs
    else:
        h_ref, w_ref, act_ref, k_ref, v_ref, wbf_s = refs
    j = pl.program_id(0)
    i = pl.program_id(1)

    @pl.when(i == 0)
    def _():
        wbf_s[...] = w_ref[...].astype(BF16)

    acc = _dot(h_ref[...], wbf_s[...])
    act_ref[...] = acc.astype(BF16)

    @pl.when((j == KV_BLOCK) & (i < na))
    def _():
        for s in range(k_ref.shape[0]):
            for dst, col0 in ((k_ref, 0), (v_ref, KV_WIDTH)):
                for hh in range(N_KV_HEADS):
                    cols = slice(col0 + hh * HEAD_DIM, col0 + (hh + 1) * HEAD_DIM)
                    dst[s, pl.ds(hh, lc, stride=N_KV_HEADS), :] = acc[s * lc:(s + 1) * lc, cols]


def _inproj(h, w_in, k_prev, v_prev, *, layer, tm, na, lc, bc):
    m = h.shape[0]
    depth = w_in.shape[0]
    tn = INPROJ_TN
    l = layer
    spb = tm // lc
    aliased = k_prev is not None

    def w_col(j):
        skip = jnp.where(j >= W_IN_DT // tn, DT_WIDTH // LANES, 0)
        return (j * (tn // LANES) + skip) * LANES

    def kv_idx(j, i):
        return jnp.where(j < KV_BLOCK, 0, jnp.where(j == KV_BLOCK, jnp.minimum(i, na - 1), na - 1))

    in_specs = [
        pl.BlockSpec((tm, D_MODEL), lambda j, i: (i, 0)),
        pl.BlockSpec((None, pl.Element(D_MODEL), pl.Element(tn)), lambda j, i: (l, 0, w_col(j))),
    ]
    args = [h, w_in]
    aliases = {}
    if aliased:
        in_specs += [pl.BlockSpec(memory_space=pl.ANY)] * 2
        args += [k_prev, v_prev]
        aliases = {2: 1, 3: 2}
    kv_rows = lc * N_KV_HEADS
    kv_spec = pl.BlockSpec((spb, None, kv_rows, HEAD_DIM), lambda j, i: (kv_idx(j, i), l, 0, 0))
    return pl.pallas_call(
        functools.partial(_inproj_kernel, na=na, lc=lc, aliased=aliased),
        grid=(ACT_WIDTH // tn, m // tm),
        in_specs=in_specs,
        out_specs=[pl.BlockSpec((tm, tn), lambda j, i: (i, j)), kv_spec, kv_spec],
        out_shape=[
            jax.ShapeDtypeStruct((m, ACT_WIDTH), BF16),
            jax.ShapeDtypeStruct((bc, depth, kv_rows, HEAD_DIM), F32),
            jax.ShapeDtypeStruct((bc, depth, kv_rows, HEAD_DIM), F32),
        ],
        scratch_shapes=[pltpu.VMEM((D_MODEL, tn), BF16)],
        input_output_aliases=aliases,
        compiler_params=_cparams(("arbitrary", "arbitrary")),
        name="inproj",
    )(*args)


PAD = SUBLANES
CHUNK_UNROLL = 2


def _ssd_kernel(*refs, L, has_init, emit_state, n_alias):
    refs = list(refs)
    (x_ref, b_ref, c_ref, z_ref, p1_ref, p2_ref, row_ref, cbc_ref, eexp_ref, cwx_ref, cwb_ref,
     cwc_ref, cbx_ref, cbb_ref, cbias_c_ref, dsk_ref, ng_ref) = refs[:17]
    pos = 17
    if has_init:
        s0f_ref, s0b_ref = refs[pos:pos + 2]
        pos += 2
    pos += n_alias
    y_ref = refs[pos]
    pos += 1
    if emit_state:
        sf_ref, sb_ref = refs[pos:pos + 2]
        pos += 2
    pad_s, xc_s, bc_s, cc_s, bt_s, cum_s, exp_s, yacc_s, sft_s, sbt_s = refs[pos:]
    nc = L // CHUNK

    tile = 2 * LANES
    conv_srcs = ((x_ref, cwx_ref, cbx_ref, xc_s, 0, GROUP_WIDTH),
                 (b_ref, cwb_ref, cbb_ref, bc_s, GROUP_WIDTH, D_STATE),
                 (c_ref, cwc_ref, cbias_c_ref, cc_s, GROUP_WIDTH + D_STATE, D_STATE))
    zeros = jnp.zeros((PAD, pad_s.shape[1]), F32)
    pad_s[0:PAD, :] = zeros
    pad_s[L + PAD:L + 2 * PAD, :] = zeros
    for src_ref, _, _, _, off, width in conv_srcs:
        for c in range(nc):
            pad_s[PAD + c * CHUNK:PAD + (c + 1) * CHUNK, off:off + width] = (
                src_ref[c * CHUNK:(c + 1) * CHUNK, :].astype(F32))

    def conv_slab(w_ref, bias_ref, dst_s, off, c, s, zero_row):
        ls = slice(s * LANES, (s + 1) * LANES)
        ps = slice(off + s * LANES, off + (s + 1) * LANES)
        acc = jnp.broadcast_to(bias_ref[:, ls] + zero_row, (CHUNK, LANES))
        for k in range(D_CONV):
            start = PAD + c * CHUNK + k - D_CONV // 2
            acc = acc + w_ref[k:k + 1, ls] * pad_s[start:start + CHUNK, ps]
        dst_s[c * CHUNK:(c + 1) * CHUNK, ls] = _silu(acc)

    conv_items = [functools.partial(conv_slab, w_ref, bias_ref, dst_s, off, c, s)
                  for _, w_ref, bias_ref, dst_s, off, width in conv_srcs
                  for c in range(nc) for s in range(width // LANES)]


    def zero_row_of(res):
        bits = pltpu.bitcast(res[0:SUBLANES, 0:LANES], jnp.uint32)
        return pltpu.bitcast((bits >> 16) >> 16, F32)[0:1, :]

    def cum_tile(t):
        ls = slice(t * tile, (t + 1) * tile)
        res = _dot(p1_ref[...], cbc_ref[:, ls])
        cum_s[:, ls] = res
        return zero_row_of(res)

    def exp_tile(e, t):
        ls = slice(t * tile, (t + 1) * tile)
        res = _dot(p2_ref[...], eexp_ref[e, :, ls])
        exp_s[e, :, ls] = res
        return zero_row_of(res)

    spread_items = ([functools.partial(cum_tile, t) for t in range(N_HD * LANES // tile)]
                    + [functools.partial(exp_tile, e, t) for e in range(N_EXPAND)
                       for t in range(GROUP_WIDTH // tile)])

    merged = sorted([((i + 0.5) / len(conv_items), 0, f) for i, f in enumerate(conv_items)]
                    + [((i + 0.5) / len(spread_items), 1, f) for i, f in enumerate(spread_items)],
                    key=lambda item: item[:2])
    zero_row = jnp.zeros((1, LANES), F32)
    for _, is_spread, emit in merged:
        if is_spread:
            zero_row = emit()
        else:
            emit(zero_row)

    if has_init:
        sft_s[...] = s0f_ref[0].T
        sbt_s[...] = s0b_ref[0].T
    else:
        sft_s[...] = jnp.zeros_like(sft_s)
        sbt_s[...] = jnp.zeros_like(sbt_s)

    ii = lax.broadcasted_iota(jnp.int32, (CHUNK, CHUNK), 0)
    jj = lax.broadcasted_iota(jnp.int32, (CHUNK, CHUNK), 1)
    lower = jj <= ii
    diag = jj == ii
    left = jj < SSD_HEADDIM

    def fwd_chunk(c, carry):
        r0 = pl.multiple_of(c * CHUNK, CHUNK)
        rows = pl.ds(r0, CHUNK)
        xq = xc_s[rows, :]
        bq = bc_s[rows, :]
        cq = cc_s[rows, :].astype(BF16)
        rowa = row_ref[c, 0]
        rowd = row_ref[c, 1]
        cb = _dot_nt(cq, bq.astype(BF16))
        y_off = _dot(cq, sft_s[...].astype(BF16)) * exp_s[0, rows, :]
        y_parts = []
        for k in range(HEADS_PER_GROUP // 2):
            ms = []
            for r in (2 * k, 2 * k + 1):
                rb = HEADS_PER_GROUP + r
                seg_f = cum_s[rows, r * LANES:(r + 1) * LANES] - rowa[r:r + 1, :]
                seg_b = cum_s[rows, rb * LANES:(rb + 1) * LANES] - rowa[rb:rb + 1, :]
                dm = jnp.exp(jnp.where(lower, seg_f, seg_b))
                dm = dm + jnp.where(diag, rowd[rb:rb + 1, :], 0.0)
                ms.append((cb * dm).astype(BF16))
            lhs = jnp.concatenate(ms, axis=1)
            xp = xq[:, k * LANES:(k + 1) * LANES]
            rhs = jnp.concatenate([jnp.where(left, xp, 0.0), jnp.where(left, 0.0, xp)],
                                  axis=0).astype(BF16)
            y_parts.append(_dot(lhs, rhs))
        yacc_s[rows, :] = jnp.concatenate(y_parts, axis=1) + y_off
        decay = exp_s[0, pl.ds(r0 + CHUNK - 1, 1), :]
        bt = bq.T.astype(BF16)
        bt_s[c] = bt
        sft_s[...] = sft_s[...] * decay + _dot(bt, (xq * exp_s[1, rows, :]).astype(BF16))
        return carry

    lax.fori_loop(0, nc, fwd_chunk, 0, unroll=CHUNK_UNROLL)

    def bwd_chunk(t, carry):
        c = nc - 1 - t
        r0 = pl.multiple_of(c * CHUNK, CHUNK)
        rows = pl.ds(r0, CHUNK)
        xq = xc_s[rows, :]
        cq = cc_s[rows, :].astype(BF16)
        y = (yacc_s[rows, :] + _dot(cq, sbt_s[...].astype(BF16)) * exp_s[2, rows, :]
             + dsk_ref[...] * xq)
        y = y * _silu(z_ref[rows, :].astype(F32))
        ms = jnp.mean(y * y, axis=-1, keepdims=True)
        y_ref[rows, :] = (y * lax.rsqrt(ms + EPS) * ng_ref[...]).astype(BF16)
        decay = exp_s[2, pl.ds(r0, 1), :]
        sbt_s[...] = sbt_s[...] * decay + _dot(bt_s[c], (xq * exp_s[3, rows, :]).astype(BF16))
        return carry

    lax.fori_loop(0, nc, bwd_chunk, 0, unroll=CHUNK_UNROLL)

    if emit_state:
        sf_ref[0] = sft_s[...].T
        sb_ref[0] = sbt_s[...].T


def _ssd(act, p1, p2, rowp, consts, conv_w, conv_b, dsk, ng, *, layer, L, nseq, row_block0,
         s0f=None, s0b=None, y_prev=None, state_prev=None):
    m = act.shape[0]
    depth = conv_w.shape[0]
    l = layer
    has_init = s0f is not None
    emit_state = not has_init
    nc = L // CHUNK
    gw = GROUP_WIDTH
    rb = row_block0
    off_b = SSD_WIDTH // D_STATE
    off_c = (SSD_WIDTH + BC_WIDTH) // D_STATE
    in_specs = [
        pl.BlockSpec((L, gw), lambda b, g: (rb + b, COL_XBC // gw + g)),
        pl.BlockSpec((L, D_STATE), lambda b, g: (rb + b, COL_XBC // D_STATE + off_b + g)),
        pl.BlockSpec((L, D_STATE), lambda b, g: (rb + b, COL_XBC // D_STATE + off_c + g)),
        pl.BlockSpec((L, gw), lambda b, g: (rb + b, COL_ZS // gw + g)),
        pl.BlockSpec((L, LANES), lambda b, g: (rb + b, g)),
        pl.BlockSpec((L, LANES), lambda b, g: (rb + b, g)),
        pl.BlockSpec((nc, 2, N_HD, CHUNK), lambda b, g: (rb + b, 0, g, 0)),
        pl.BlockSpec((LANES, N_HD * LANES), lambda b, g: (0, 0)),
        pl.BlockSpec((N_EXPAND, LANES, gw), lambda b, g: (0, 0, 0)),
        pl.BlockSpec((None, D_CONV, gw), lambda b, g: (l, 0, g)),
        pl.BlockSpec((None, D_CONV, D_STATE), lambda b, g: (l, 0, off_b + g)),
        pl.BlockSpec((None, D_CONV, D_STATE), lambda b, g: (l, 0, off_c + g)),
        pl.BlockSpec((None, 1, gw), lambda b, g: (l, 0, g)),
        pl.BlockSpec((None, 1, D_STATE), lambda b, g: (l, 0, off_b + g)),
        pl.BlockSpec((None, 1, D_STATE), lambda b, g: (l, 0, off_c + g)),
        pl.BlockSpec((None, 1, gw), lambda b, g: (l, 0, g)),
        pl.BlockSpec((None, 1, gw), lambda b, g: (l, 0, g)),
    ]
    args = [act, act, act, act, p1, p2, rowp, consts["cbc"], consts["eexp"], conv_w, conv_w, conv_w,
            conv_b, conv_b, conv_b, dsk, ng]
    aliases = {}
    n_alias = 0
    state_spec = pl.BlockSpec((1, None, gw, D_STATE), lambda b, g: (b, l, g, 0))
    if has_init:
        in_specs += [state_spec, state_spec, pl.BlockSpec(memory_space=pl.ANY)]
        args += [s0f, s0b, y_prev]
        aliases = {len(args) - 1: 0}
        n_alias = 1
    elif state_prev is not None:
        in_specs += [pl.BlockSpec(memory_space=pl.ANY)] * 2
        args += list(state_prev)
        aliases = {len(args) - 2: 1, len(args) - 1: 2}
        n_alias = 2
    out_specs = [pl.BlockSpec((L, gw), lambda b, g: (rb + b, g))]
    out_shape = [jax.ShapeDtypeStruct((m, SSD_WIDTH), BF16)]
    if emit_state:
        out_specs += [state_spec] * 2
        out_shape += [jax.ShapeDtypeStruct((nseq, depth, SSD_WIDTH, D_STATE), F32)] * 2
    scratch = [
        pltpu.VMEM((L + 2 * PAD, gw + 2 * D_STATE), F32),
        pltpu.VMEM((L, gw), F32),
        pltpu.VMEM((L, D_STATE), F32),
        pltpu.VMEM((L, D_STATE), F32),
        pltpu.VMEM((nc, D_STATE, CHUNK), BF16),
        pltpu.VMEM((L, N_HD * LANES), F32),
        pltpu.VMEM((N_EXPAND, L, gw), F32),
        pltpu.VMEM((L, gw), F32),
        pltpu.VMEM((D_STATE, gw), F32),
        pltpu.VMEM((D_STATE, gw), F32),
    ]
    kern = functools.partial(_ssd_kernel, L=L, has_init=has_init, emit_state=emit_state,
                             n_alias=n_alias)
    return pl.pallas_call(
        kern,
        grid=(nseq, N_SSD_GROUPS),
        in_specs=in_specs,
        out_specs=out_specs,
        out_shape=out_shape,
        scratch_shapes=scratch,
        input_output_aliases=aliases,
        compiler_params=_cparams(("arbitrary", "arbitrary")),
        name="ssd_latent" if has_init else "ssd_context",
    )(*args)


ATTN_SCALE = HEAD_DIM ** -0.5


def _ctx_attn_kernel(sink_ref, q_ref, k_ref, v_ref, z_ref, o_ref, *, L):
    g = pl.program_id(1)
    for sq in range(q_ref.shape[0] // L):
        rows = slice(sq * L, (sq + 1) * L)
        k = k_ref[rows, :]
        v = v_ref[rows, :]
        for r in range(Q_PER_KV):
            ls = slice(r * HEAD_DIM, (r + 1) * HEAD_DIM)
            sink = sink_ref[g * Q_PER_KV + r]
            s = _dot_nt(q_ref[rows, ls], k) * ATTN_SCALE
            m = jnp.maximum(jnp.max(s, axis=-1, keepdims=True), sink)
            p = jnp.exp(s - m)
            denom = jnp.sum(p, axis=-1, keepdims=True) + jnp.exp(sink - m)
            o = _dot(p.astype(BF16), v) / denom
            o_ref[rows, ls] = (o * _silu(z_ref[rows, ls].astype(F32))).astype(BF16)


def _ctx_attention(act, sink, *, L, nseq):
    m = act.shape[0]
    gw = Q_PER_KV * HEAD_DIM
    per_step = math.gcd(nseq, 2)
    rows = per_step * L
    return pl.pallas_call(
        functools.partial(_ctx_attn_kernel, L=L),
        grid=(nseq // per_step, N_KV_HEADS),
        in_specs=[
            pl.BlockSpec(memory_space=pltpu.SMEM),
            pl.BlockSpec((rows, gw), lambda b, g: (b, COL_Q // gw + g)),
            pl.BlockSpec((rows, HEAD_DIM), lambda b, g: (b, COL_K // HEAD_DIM + g)),
            pl.BlockSpec((rows, HEAD_DIM), lambda b, g: (b, COL_V // HEAD_DIM + g)),
            pl.BlockSpec((rows, gw), lambda b, g: (b, COL_ZA // gw + g)),
        ],
        out_specs=pl.BlockSpec((rows, gw), lambda b, g: (b, g)),
        out_shape=jax.ShapeDtypeStruct((m, ATTN_WIDTH), BF16),
        compiler_params=_cparams(("arbitrary", "arbitrary")),
        name="attn_context",
    )(sink, act, act, act, act)


def _rope_tables(length):
    sec = HEAD_DIM // 2
    half = sec // 2
    d = np.arange(HEAD_DIM)
    e = d % sec
    freqs = ROPE_BASE ** (-(e % half).astype(np.float64) / half)
    t = np.arange(length)
    pos = np.where((d // sec)[None, :] == 0, (t // GRID_W)[:, None], (t % GRID_W)[:, None])
    ang = pos.astype(np.float64) * freqs[None, :]
    sign = np.where(e < half, -1.0, 1.0)[None, :]
    return (jnp.asarray(np.cos(ang), F32), jnp.asarray(np.sin(ang) * sign, F32))


def _rope(x, cos, sin_signed, first_half):
    partner = jnp.where(first_half, pltpu.roll(x, LANES - HEAD_DIM // 4, 1),
                        pltpu.roll(x, HEAD_DIM // 4, 1))
    return x * cos + partner * sin_signed


def _lat_attn_kernel(sink_ref, q_ref, k_ref, v_ref, z_ref, kc_ref, vc_ref, cos_ref, sin_ref,
                     _oprev_ref, o_ref, kctx_s, vctx_s, keys_s, vals_s, *, L, lc):
    g = pl.program_id(1)
    nb = L // CHUNK
    win = 3 * CHUNK
    rows4 = Q_PER_KV * CHUNK
    lane = lax.broadcasted_iota(jnp.int32, (CHUNK, HEAD_DIM), 1)
    first_half = (lane % (HEAD_DIM // 2)) < (HEAD_DIM // 4)

    kctx_s[...] = kc_ref[...].astype(BF16)
    vctx_s[...] = vc_ref[...].astype(BF16)
    zero_blk = jnp.zeros((CHUNK, HEAD_DIM), BF16)
    for dst in (keys_s, vals_s):
        dst[0:CHUNK, :] = zero_blk
        dst[CHUNK + L:2 * CHUNK + L, :] = zero_blk
    for n in range(nb):
        rows = slice(n * CHUNK, (n + 1) * CHUNK)
        kr = _rope(k_ref[rows, :].astype(F32), cos_ref[rows, :], sin_ref[rows, :], first_half)
        keys_s[CHUNK + n * CHUNK:CHUNK + (n + 1) * CHUNK, :] = kr.astype(BF16)
        vals_s[CHUNK + n * CHUNK:CHUNK + (n + 1) * CHUNK, :] = v_ref[rows, :]

    qi = lax.broadcasted_iota(jnp.int32, (rows4, win), 0) % CHUNK
    wi = lax.broadcasted_iota(jnp.int32, (rows4, win), 1)
    band = jnp.abs(qi - wi + CHUNK) <= WINDOW
    head = lax.broadcasted_iota(jnp.int32, (rows4, 1), 0) // CHUNK
    sink = jnp.zeros((rows4, 1), F32)
    for r in range(Q_PER_KV):
        sink = jnp.where(head == r, sink_ref[g * Q_PER_KV + r], sink)

    def block(n, carry):
        r0 = pl.multiple_of(n * CHUNK, CHUNK)
        rows = pl.ds(r0, CHUNK)
        cos = cos_ref[rows, :]
        sin = sin_ref[rows, :]
        q = jnp.concatenate(
            [_rope(q_ref[rows, r * HEAD_DIM:(r + 1) * HEAD_DIM].astype(F32), cos, sin,
                   first_half).astype(BF16) for r in range(Q_PER_KV)], axis=0)
        in_seq = (wi >= CHUNK - r0) & (wi < L + CHUNK - r0)
        s_ctx = _dot_nt(q, kctx_s[...]) * ATTN_SCALE
        s_lat = _dot_nt(q, keys_s[pl.ds(r0, win), :]) * ATTN_SCALE
        s_lat = jnp.where(in_seq, jnp.where(band, s_lat, -jnp.inf), -jnp.inf)
        m = jnp.maximum(jnp.maximum(jnp.max(s_ctx, axis=-1, keepdims=True),
                                    jnp.max(s_lat, axis=-1, keepdims=True)), sink)
        p_ctx = jnp.exp(s_ctx - m)
        p_lat = jnp.exp(s_lat - m)
        denom = (jnp.sum(p_ctx, axis=-1, keepdims=True) + jnp.sum(p_lat, axis=-1, keepdims=True)
                 + jnp.exp(sink - m))
        o = (_dot(p_ctx.astype(BF16), vctx_s[...])
             + _dot(p_lat.astype(BF16), vals_s[pl.ds(r0, win), :])) / denom
        for r in range(Q_PER_KV):
            ls = slice(r * HEAD_DIM, (r + 1) * HEAD_DIM)
            o_ref[rows, ls] = (o[r * CHUNK:(r + 1) * CHUNK, :]
                               * _silu(z_ref[rows, ls].astype(F32))).astype(BF16)
        return carry

    lax.fori_loop(0, nb, block, 0)


def _lat_attention(act, sink, cache_k, cache_v, o_prev, *, L, nseq, row_block0, layer):
    m = act.shape[0]
    gw = Q_PER_KV * HEAD_DIM
    lc = cache_k.shape[2]
    rb = row_block0
    cos, sin = _rope_tables(L)
    kc = cache_k.reshape(cache_k.shape[0], cache_k.shape[1], lc, KV_WIDTH)
    vc = cache_v.reshape(cache_v.shape[0], cache_v.shape[1], lc, KV_WIDTH)
    kern = functools.partial(_lat_attn_kernel, L=L, lc=lc)
    tab = lambda b, g: (0, 0)
    return pl.pallas_call(
        kern,
        grid=(nseq, N_KV_HEADS),
        in_specs=[
            pl.BlockSpec(memory_space=pltpu.SMEM),
            pl.BlockSpec((L, gw), lambda b, g: (rb + b, COL_Q // gw + g)),
            pl.BlockSpec((L, HEAD_DIM), lambda b, g: (rb + b, COL_K // HEAD_DIM + g)),
            pl.BlockSpec((L, HEAD_DIM), lambda b, g: (rb + b, COL_V // HEAD_DIM + g)),
            pl.BlockSpec((L, gw), lambda b, g: (rb + b, COL_ZA // gw + g)),
            pl.BlockSpec((None, None, lc, HEAD_DIM), lambda b, g: (b, layer, 0, g)),
            pl.BlockSpec((None, None, lc, HEAD_DIM), lambda b, g: (b, layer, 0, g)),
            pl.BlockSpec((L, HEAD_DIM), tab),
            pl.BlockSpec((L, HEAD_DIM), tab),
            pl.BlockSpec(memory_space=pl.ANY),
        ],
        out_specs=pl.BlockSpec((L, gw), lambda b, g: (rb + b, g)),
        out_shape=jax.ShapeDtypeStruct((m, ATTN_WIDTH), BF16),
        scratch_shapes=[pltpu.VMEM((lc, HEAD_DIM), BF16), pltpu.VMEM((lc, HEAD_DIM), BF16),
                        pltpu.VMEM((L + 2 * CHUNK, HEAD_DIM), BF16),
                        pltpu.VMEM((L + 2 * CHUNK, HEAD_DIM), BF16)],
        input_output_aliases={9: 0},
        compiler_params=_cparams(("arbitrary", "arbitrary")),
        name="attn_latent",
    )(sink, act, act, act, act, kc, vc, cos, sin, o_prev)


def _branch_kernel(oa_ref, ys_ref, wpa_ref, wps_ref, ga_ref, gs_ref, o_ref):
    a = _dot(oa_ref[...], wpa_ref[...])
    s = _dot(ys_ref[...], wps_ref[...])
    merged = _sigmoid(ga_ref[...].astype(F32)) * a + _sigmoid(gs_ref[...].astype(F32)) * s
    o_ref[...] = merged.astype(BF16)


def _branches(oa, ys, w_pa, w_ps, act, *, layer, tm):
    m = oa.shape[0]
    tn = 512
    l = layer
    return pl.pallas_call(
        _branch_kernel,
        grid=(m // tm, D_MODEL // tn),
        in_specs=[
            pl.BlockSpec((tm, ATTN_WIDTH), lambda i, j: (i, 0)),
            pl.BlockSpec((tm, SSD_WIDTH), lambda i, j: (i, 0)),
            pl.BlockSpec((None, ATTN_WIDTH, tn), lambda i, j: (l, 0, j)),
            pl.BlockSpec((None, SSD_WIDTH, tn), lambda i, j: (l, 0, j)),
            pl.BlockSpec((tm, tn), lambda i, j: (i, COL_GA // tn + j)),
            pl.BlockSpec((tm, tn), lambda i, j: (i, COL_GS // tn + j)),
        ],
        out_specs=pl.BlockSpec((tm, tn), lambda i, j: (i, j)),
        out_shape=jax.ShapeDtypeStruct((m, D_MODEL), BF16),
        compiler_params=_cparams(("arbitrary", "arbitrary")),
        name="branches",
    )(oa, ys, w_pa, w_ps, act, act)


def _out_kernel(*refs, final, na):
    if final:
        mg_ref, w_ref, xa_ref, xb_ref, gate_ref, fg_ref, ya_ref, yb_ref = refs
    else:
        mg_ref, w_ref, xa_ref, xb_ref, gate_ref, o_ref = refs
    is_ctx = pl.program_id(0) < na
    x = jnp.where(is_ctx, xa_ref[...], xb_ref[...])
    y = x + gate_ref[...] * _dot(mg_ref[...], w_ref[...])
    if not final:
        o_ref[...] = y
        return
    ms = jnp.mean(y * y, axis=-1, keepdims=True)
    y = y * lax.rsqrt(ms + EPS) * fg_ref[...]

    @pl.when(is_ctx)
    def _():
        ya_ref[...] = y

    @pl.when(jnp.logical_not(is_ctx))
    def _():
        yb_ref[...] = y


def _out_proj(merged, w_out, xa, xb, xb_offset, mod, final_g, *, layer, tm, na, group_of):
    m = merged.shape[0]
    l = layer
    final = final_g is not None
    in_specs = [
        pl.BlockSpec((tm, D_MODEL), lambda i: (i, 0)),
        pl.BlockSpec((None, D_MODEL, D_MODEL), lambda i: (l, 0, 0)),
    ] + _row_split_specs(tm, na, xb_offset) + [
        pl.BlockSpec((None, None, 1, D_MODEL), lambda i: (l, group_of(i, tm), 0, 2)),
    ]
    args = [merged, w_out, xa, xb, mod]
    if final:
        in_specs.append(pl.BlockSpec((1, D_MODEL), lambda i: (0, 0)))
        args.append(final_g.reshape(1, D_MODEL))
        out_specs = [
            pl.BlockSpec((tm, D_MODEL), lambda i: (jnp.minimum(i, na - 1), 0)),
            pl.BlockSpec((tm, D_MODEL), lambda i: (jnp.maximum(i - na, 0), 0)),
        ]
        out_shape = [jax.ShapeDtypeStruct((na * tm, D_MODEL), F32),
                     jax.ShapeDtypeStruct((m - na * tm, D_MODEL), F32)]
    else:
        out_specs = pl.BlockSpec((tm, D_MODEL), lambda i: (i, 0))
        out_shape = jax.ShapeDtypeStruct((m, D_MODEL), F32)
    return pl.pallas_call(
        functools.partial(_out_kernel, final=final, na=na),
        grid=(m // tm,),
        in_specs=in_specs,
        out_specs=out_specs,
        out_shape=out_shape,
        compiler_params=_cparams(("arbitrary",)),
        name="out_proj_final" if final else "out_proj",
    )(*args)


def _dt_permutation():
    perm = np.zeros(DT_WIDTH, np.int32)
    for g in range(N_SSD_GROUPS):
        for d in range(2):
            for r in range(HEADS_PER_GROUP):
                perm[g * 16 + d * HEADS_PER_GROUP + r] = d * N_SSD_HEADS + g * HEADS_PER_GROUP + r
    return perm


def kernel(x_prompt, x_sample, c, cache_k, cache_v, state_ssm_fwd, state_ssm_bwd, c_ctx, norm_g, w_mod, b_mod, w_in, conv_w, conv_b, attn_sink, a_log_fwd, a_log_bwd, dt_bias_fwd, dt_bias_bwd, d_skip, ssd_norm_g, w_pa, w_ps, w_out, final_norm_g):
    bc, lc, _ = x_prompt.shape
    bl, ll, _ = x_sample.shape
    depth = w_in.shape[0]
    n_ctx = bc * lc
    m = n_ctx + bl * ll
    assert n_ctx % ll == 0 and ll % lc == 0 and lc % CHUNK == 0
    assert 1 + bl <= COND_ROWS

    def group_of(i, tm):
        return jnp.maximum(i * tm - n_ctx + ll, 0) // ll

    tm_big = math.gcd(1024, math.gcd(n_ctx, ll))
    tm_small = math.gcd(512, tm_big)

    cond = jnp.zeros((COND_ROWS, D_MODEL), F32).at[0].set(c_ctx).at[1:1 + bl].set(c)
    mod = _modulation(cond, w_mod, b_mod)

    perm = _dt_permutation()
    consts = _ssd_constants()
    wdt = w_in[:, :, W_IN_DT:W_IN_DT + DT_WIDTH][:, :, perm]
    wdtt = jnp.swapaxes(wdt, 1, 2)
    bias = jnp.concatenate([dt_bias_fwd, dt_bias_bwd], axis=1)[:, perm]
    alog = jnp.concatenate([a_log_fwd, a_log_bwd], axis=1)[:, perm]
    norm_g3 = norm_g.reshape(depth, 1, D_MODEL)
    conv_b3 = conv_b.reshape(depth, 1, CONV_WIDTH)
    dsk = jnp.repeat(d_skip, SSD_HEADDIM, axis=1).reshape(depth, 1, SSD_WIDTH)
    ng = ssd_norm_g.reshape(depth, 1, SSD_WIDTH)
    w_pa_bf = w_pa.astype(BF16)
    w_ps_bf = w_ps.astype(BF16)
    w_out_bf = w_out.astype(BF16)
    s0f = state_ssm_fwd.reshape(bl, depth, SSD_WIDTH, D_STATE)
    s0b = state_ssm_bwd.reshape(bl, depth, SSD_WIDTH, D_STATE)

    na_small = n_ctx // tm_small
    na_big = n_ctx // tm_big
    xa, xb, xb_off = x_prompt.reshape(n_ctx, D_MODEL), x_sample.reshape(bl * ll, D_MODEL), 0
    k_new = v_new = states = None
    for l in range(depth):
        h, p1, p2, rowp = _prep(xa, xb, xb_off, m, mod, norm_g3, wdtt, bias, alog, consts,
                                layer=l, tm=tm_small, na=na_small, group_of=group_of)
        act, k_new, v_new = _inproj(h, w_in, k_new, v_new, layer=l, tm=tm_big, na=na_big, lc=lc,
                                    bc=bc)

        sink = attn_sink[l]
        oa = _ctx_attention(act, sink, L=lc, nseq=bc)
        oa = _lat_attention(act, sink, cache_k, cache_v, oa, L=ll, nseq=bl,
                            row_block0=n_ctx // ll, layer=l)

        ys, sf, sb = _ssd(act, p1, p2, rowp, consts, conv_w, conv_b3, dsk, ng, layer=l, L=lc,
                          nseq=bc, row_block0=0, state_prev=states)
        states = (sf, sb)
        (ys,) = _ssd(act, p1, p2, rowp, consts, conv_w, conv_b3, dsk, ng, layer=l, L=ll, nseq=bl,
                     row_block0=n_ctx // ll, s0f=s0f, s0b=s0b, y_prev=ys)

        merged = _branches(oa, ys, w_pa_bf, w_ps_bf, act, layer=l, tm=tm_big)
        last = l == depth - 1
        res = _out_proj(merged, w_out_bf, xa, xb, xb_off, mod, final_norm_g if last else None,
                        layer=l, tm=tm_small, na=na_small, group_of=group_of)
        if not last:
            xa, xb, xb_off = res, res, na_small

    y_prompt = res[0].reshape(bc, lc, D_MODEL)
    y_sample = res[1].reshape(bl, ll, D_MODEL)
    shape_kv = (bc, depth, lc, N_KV_HEADS, HEAD_DIM)
    shape_st = (bc, depth, N_SSD_HEADS, SSD_HEADDIM, D_STATE)
    return (y_prompt, y_sample, k_new.reshape(shape_kv), v_new.reshape(shape_kv),
            states[0].reshape(shape_st), states[1].reshape(shape_st))
```

```python
import functools
import math

import numpy as np
import jax
import jax.numpy as jnp
from jax import lax
from jax.experimental import pallas as pl
from jax.experimental.pallas import tpu as pltpu

F32 = jnp.float32
BF16 = jnp.bfloat16

D_MODEL = 2048
HEAD_DIM = 128
N_Q_HEADS = 16
N_KV_HEADS = 4
Q_PER_KV = 4
ATTN_WIDTH = 2048
KV_WIDTH = 512
WINDOW = 128
GRID_W = 64
ROPE_BASE = 10000.0
SSD_WIDTH = 4096
SSD_HEADDIM = 64
N_SSD_HEADS = 64
D_STATE = 128
N_SSD_GROUPS = 8
HEADS_PER_GROUP = 8
GROUP_WIDTH = SSD_WIDTH // N_SSD_GROUPS
CHUNK = 128
D_CONV = 5
BC_WIDTH = 1024
CONV_WIDTH = 6144
EPS = 1e-6
MOD_WIDTH = 3 * D_MODEL
LOG2_E = math.log2(math.e)

COL_Q = 0
COL_K = 2048
COL_V = 2560
COL_ZA = 3072
COL_XBC = 5120
COL_ZS = 11264
COL_GA = 15360
COL_GS = 17408
ACT_WIDTH = 19456
W_IN_DT = 15360
DT_WIDTH = 2 * N_SSD_HEADS

LANES = 128
SUBLANES = 8
MXU_WIDTH = 256
VMEM_LIMIT = 56 * 1024 * 1024

COND_ROWS = 8


def _cparams(sem):
    return pltpu.CompilerParams(dimension_semantics=sem, vmem_limit_bytes=VMEM_LIMIT)


def _dot(a, b):
    return jnp.dot(a, b, preferred_element_type=F32)


def _dot_nt(a, b):
    return lax.dot_general(a, b, (((1,), (1,)), ((), ())), preferred_element_type=F32)


def _split2(x):
    hi = x.astype(BF16)
    lo = (x - hi.astype(F32)).astype(BF16)
    return hi, lo


def _split3(x):
    p1 = x.astype(BF16)
    r1 = x - p1.astype(F32)
    p2 = r1.astype(BF16)
    p3 = (r1 - p2.astype(F32)).astype(BF16)
    return p1, p2, p3


def _dot3(a, b):
    ah, al = _split2(a)
    bh, bl = _split2(b)
    return _dot(ah, bh) + _dot(al, bh) + _dot(ah, bl)


def _sigmoid(x):
    return 0.5 + 0.5 * jnp.tanh(0.5 * x)


def _silu(x):
    half = 0.5 * x
    return half + half * jnp.tanh(half)


def _softplus(x):
    return jnp.maximum(x, 0.0) + jnp.log1p(jnp.exp(-jnp.abs(x)))


def _mod_kernel(cond_ref, w_ref, b_ref, o_ref):
    res = _dot3(_silu(cond_ref[...]), w_ref[...]) + b_ref[...]
    for r in range(COND_ROWS):
        o_ref[r] = res[r:r + 1, :]


def _modulation(cond, w_mod, b_mod):
    depth = w_mod.shape[0]
    tn = 512
    return pl.pallas_call(
        _mod_kernel,
        grid=(depth, MOD_WIDTH // tn),
        in_specs=[
            pl.BlockSpec((COND_ROWS, D_MODEL), lambda l, j: (0, 0)),
            pl.BlockSpec((None, D_MODEL, tn), lambda l, j: (l, 0, j)),
            pl.BlockSpec((None, 1, tn), lambda l, j: (l, 0, j)),
        ],
        out_specs=pl.BlockSpec((None, COND_ROWS, 1, tn), lambda l, j: (l, 0, 0, j)),
        out_shape=jax.ShapeDtypeStruct((depth, COND_ROWS, 1, MOD_WIDTH), F32),
        compiler_params=_cparams(("arbitrary", "arbitrary")),
        name="modulation",
    )(cond, w_mod, b_mod.reshape(depth, 1, MOD_WIDTH))


def _prep_kernel(xa_ref, xb_ref, shift_ref, scale_ref, g_ref, wdtt_ref, biast_ref, alogt_ref,
                 sel1_ref, sel2_ref, h_ref, p1_ref, p2_ref, row_ref, *, tm, na):
    x = jnp.where(pl.program_id(0) < na, xa_ref[...], xb_ref[...])
    ms = jnp.mean(x * x, axis=-1, keepdims=True)
    h = (x * lax.rsqrt(ms + EPS) * g_ref[...]) * (1.0 + scale_ref[...]) + shift_ref[...]
    h_ref[...] = h.astype(BF16)

    hh, hl = _split2(h)
    wth, wtl = _split2(wdtt_ref[...])
    rawt = _dot_nt(wth, hh) + _dot_nt(wth, hl) + _dot_nt(wtl, hh)
    dtt = _softplus(rawt + biast_ref[...])
    dtat = dtt * (-jnp.exp(alogt_ref[...]))

    ii = lax.broadcasted_iota(jnp.int32, (CHUNK, CHUNK), 0)
    kk = lax.broadcasted_iota(jnp.int32, (CHUNK, CHUNK), 1)
    lt = jnp.where(kk <= ii, 1.0, 0.0).astype(BF16)
    ut = jnp.where(kk >= ii, 1.0, 0.0).astype(BF16)
    fwd_row = (ii % 16) < HEADS_PER_GROUP

    for c in range(tm // CHUNK):
        rows = slice(c * CHUNK, (c + 1) * CHUNK)
        dt_t = dtt[:, rows]
        q1, q2, q3 = _split3(dtat[:, rows])
        pre_t = _dot(q1, ut) + _dot(q2, ut) + _dot(q3, ut)
        suf_t = _dot(q1, lt) + _dot(q2, lt) + _dot(q3, lt)
        acs_t = jnp.where(fwd_row, pre_t, suf_t)
        edge_t = jnp.where(fwd_row, acs_t[:, CHUNK - 1:CHUNK], acs_t[:, 0:1])
        w1_t = dt_t * jnp.exp(edge_t - acs_t)
        row_ref[c, 0] = (acs_t - jnp.log(dt_t)) * LOG2_E
        row_ref[c, 1] = dt_t
        acs = acs_t.T
        p1 = _dot(jnp.concatenate(_split3(acs * LOG2_E), axis=1), sel1_ref[...])
        p1_ref[rows, :] = p1.astype(BF16)
        p2 = _dot(jnp.concatenate(_split3(jnp.exp(acs)) + _split3(w1_t.T), axis=1), sel2_ref[...])
        p2_ref[rows, :] = p2.astype(BF16)


def _row_split_specs(tm, na, xb_offset):
    return [
        pl.BlockSpec((tm, D_MODEL), lambda i: (jnp.minimum(i, na - 1), 0)),
        pl.BlockSpec((tm, D_MODEL), lambda i: (xb_offset + jnp.maximum(i - na, 0), 0)),
    ]


def _prep(xa, xb, xb_offset, m, mod, norm_g, wdtt, bias, alog, consts, *, layer, tm, na, group_of):
    l = layer
    kern = functools.partial(_prep_kernel, tm=tm, na=na)
    par2 = lambda i: (l, 0, 0)
    whole = lambda i: (0, 0)
    pw = N_SSD_GROUPS * LANES
    return pl.pallas_call(
        kern,
        grid=(m // tm,),
        in_specs=_row_split_specs(tm, na, xb_offset) + [
            pl.BlockSpec((None, None, 1, D_MODEL), lambda i: (l, group_of(i, tm), 0, 0)),
            pl.BlockSpec((None, None, 1, D_MODEL), lambda i: (l, group_of(i, tm), 0, 1)),
            pl.BlockSpec((None, 1, D_MODEL), par2),
            pl.BlockSpec((None, DT_WIDTH, D_MODEL), par2),
            pl.BlockSpec((None, DT_WIDTH, 1), par2),
            pl.BlockSpec((None, DT_WIDTH, 1), par2),
            pl.BlockSpec((N_PIECES * LANES, pw), whole),
            pl.BlockSpec((2 * N_PIECES * LANES, pw), whole),
        ],
        out_specs=[
            pl.BlockSpec((tm, D_MODEL), lambda i: (i, 0)),
            pl.BlockSpec((tm, pw), lambda i: (i, 0)),
            pl.BlockSpec((tm, pw), lambda i: (i, 0)),
            pl.BlockSpec((tm // CHUNK, 2, DT_WIDTH, CHUNK), lambda i: (i, 0, 0, 0)),
        ],
        out_shape=[
            jax.ShapeDtypeStruct((m, D_MODEL), BF16),
            jax.ShapeDtypeStruct((m, pw), BF16),
            jax.ShapeDtypeStruct((m, pw), BF16),
            jax.ShapeDtypeStruct((m // CHUNK, 2, DT_WIDTH, CHUNK), F32),
        ],
        compiler_params=_cparams(("arbitrary",)),
        name="prep",
    )(xa, xb, mod, mod, norm_g, wdtt, bias[:, :, None], alog[:, :, None], consts["sel1"],
      consts["sel2"])


N_PIECES = 3
N_HD = 2 * HEADS_PER_GROUP
PIECE_LANES = N_PIECES * N_HD
N_EXPAND = 4


def _ssd_constants():
    pw = N_SSD_GROUPS * LANES
    sel1 = np.zeros((N_PIECES * LANES, pw), np.float32)
    sel2 = np.zeros((2 * N_PIECES * LANES, pw), np.float32)
    for g in range(N_SSD_GROUPS):
        for hd in range(N_HD):
            lam = g * N_HD + hd
            for p in range(N_PIECES):
                sel1[p * LANES + lam, g * LANES + N_PIECES * hd + p] = 1.0
                for q in range(2):
                    sel2[(q * N_PIECES + p) * LANES + lam,
                         g * LANES + q * PIECE_LANES + N_PIECES * hd + p] = 1.0
    cbc = np.zeros((LANES, N_HD * LANES), np.float32)
    for hd in range(N_HD):
        cbc[N_PIECES * hd:N_PIECES * (hd + 1), hd * LANES:(hd + 1) * LANES] = 1.0
    eexp = np.zeros((N_EXPAND, LANES, GROUP_WIDTH), np.float32)
    for e, (q, d) in enumerate(((0, 0), (1, 0), (0, 1), (1, 1))):
        for r in range(HEADS_PER_GROUP):
            row0 = q * PIECE_LANES + N_PIECES * (d * HEADS_PER_GROUP + r)
            eexp[e, row0:row0 + N_PIECES, r * SSD_HEADDIM:(r + 1) * SSD_HEADDIM] = 1.0
    return {k: jnp.asarray(v, BF16) for k, v in
            (("sel1", sel1), ("sel2", sel2), ("cbc", cbc), ("eexp", eexp))}


INPROJ_TN = 1024
KV_BLOCK = COL_K // INPROJ_TN


def _inproj_kernel(*refs, na, lc, aliased):
    if aliased:
        h_ref, w_ref, _kprev, _vprev, act_ref, k_ref, v_ref, wbf_s = refs
    else:
        h_ref, w_ref, act_ref, k_ref, v_ref, wbf_s = refs
    j = pl.program_id(0)
    i = pl.program_id(1)

    @pl.when(i == 0)
    def _():
        wbf_s[...] = w_ref[...].astype(BF16)

    is_kv = (j == KV_BLOCK) & (i < na)

    @pl.when(jnp.logical_not(is_kv))
    def _():
        for t in range(act_ref.shape[1] // MXU_WIDTH):
            cols = slice(t * MXU_WIDTH, (t + 1) * MXU_WIDTH)
            act_ref[:, cols] = _dot(h_ref[...], wbf_s[:, cols]).astype(BF16)

    @pl.when(is_kv)
    def _():
        acc = _dot(h_ref[...], wbf_s[...])
        act_ref[...] = acc.astype(BF16)
        for s in range(k_ref.shape[0]):
            for dst, col0 in ((k_ref, 0), (v_ref, KV_WIDTH)):
                for hh in range(N_KV_HEADS):
                    cols = slice(col0 + hh * HEAD_DIM, col0 + (hh + 1) * HEAD_DIM)
                    dst[s, pl.ds(hh, lc, stride=N_KV_HEADS), :] = acc[s * lc:(s + 1) * lc, cols]


def _inproj(h, w_in, k_prev, v_prev, *, layer, tm, na, lc, bc):
    m = h.shape[0]
    depth = w_in.shape[0]
    tn = INPROJ_TN
    l = layer
    spb = tm // lc
    aliased = k_prev is not None

    def w_col(j):
        skip = jnp.where(j >= W_IN_DT // tn, DT_WIDTH // LANES, 0)
        return (j * (tn // LANES) + skip) * LANES

    def kv_idx(j, i):
        return jnp.where(j < KV_BLOCK, 0, jnp.where(j == KV_BLOCK, jnp.minimum(i, na - 1), na - 1))

    in_specs = [
        pl.BlockSpec((tm, D_MODEL), lambda j, i: (i, 0)),
        pl.BlockSpec((None, pl.Element(D_MODEL), pl.Element(tn)), lambda j, i: (l, 0, w_col(j))),
    ]
    args = [h, w_in]
    aliases = {}
    if aliased:
        in_specs += [pl.BlockSpec(memory_space=pl.ANY)] * 2
        args += [k_prev, v_prev]
        aliases = {2: 1, 3: 2}
    kv_rows = lc * N_KV_HEADS
    kv_spec = pl.BlockSpec((spb, None, kv_rows, HEAD_DIM), lambda j, i: (kv_idx(j, i), l, 0, 0))
    return pl.pallas_call(
        functools.partial(_inproj_kernel, na=na, lc=lc, aliased=aliased),
        grid=(ACT_WIDTH // tn, m // tm),
        in_specs=in_specs,
        out_specs=[pl.BlockSpec((tm, tn), lambda j, i: (i, j)), kv_spec, kv_spec],
        out_shape=[
            jax.ShapeDtypeStruct((m, ACT_WIDTH), BF16),
            jax.ShapeDtypeStruct((bc, depth, kv_rows, HEAD_DIM), F32),
            jax.ShapeDtypeStruct((bc, depth, kv_rows, HEAD_DIM), F32),
        ],
        scratch_shapes=[pltpu.VMEM((D_MODEL, tn), BF16)],
        input_output_aliases=aliases,
        compiler_params=_cparams(("arbitrary", "arbitrary")),
        name="inproj",
    )(*args)


PAD = SUBLANES
CHUNK_UNROLL = 2
SSD_SCRATCH_BUDGET = 16 * 1024 * 1024


def _ssd_kernel(*refs, L, per_step, has_init, emit_state, n_alias):
    refs = list(refs)
    seq_in = refs[:7]
    shared = refs[7:17]
    pos = 17
    state_in = []
    if has_init:
        state_in = refs[pos:pos + 2]
        pos += 2
    pos += n_alias
    y_ref = refs[pos]
    pos += 1
    state_out = []
    if emit_state:
        state_out = refs[pos:pos + 2]
        pos += 2
    scratch = refs[pos:]
    nc = L // CHUNK
    for sq in range(per_step):
        rows = pl.ds(sq * L, L)
        views = [r.at[rows] for r in seq_in[:6]] + [seq_in[6].at[pl.ds(sq * nc, nc)]]
        _ssd_sequence(views, shared, [r.at[pl.ds(sq, 1)] for r in state_in], y_ref.at[rows],
                      [r.at[pl.ds(sq, 1)] for r in state_out], [s.at[sq] for s in scratch], L=L)


def _ssd_sequence(seq_in, shared, state_in, y_ref, state_out, scratch, *, L):
    x_ref, b_ref, c_ref, z_ref, p1_ref, p2_ref, row_ref = seq_in
    (cbc_ref, eexp_ref, cwx_ref, cwb_ref, cwc_ref, cbx_ref, cbb_ref, cbias_c_ref, dsk_ref,
     ng_ref) = shared
    has_init = bool(state_in)
    emit_state = bool(state_out)
    if has_init:
        s0f_ref, s0b_ref = state_in
    if emit_state:
        sf_ref, sb_ref = state_out
    pad_s, xc_s, bc_s, cc_s, bt_s, cum_s, exp_s, yacc_s, sft_s, sbt_s = scratch
    nc = L // CHUNK

    tile = 2 * LANES
    conv_srcs = ((x_ref, cwx_ref, cbx_ref, xc_s, 0, GROUP_WIDTH),
                 (b_ref, cwb_ref, cbb_ref, bc_s, GROUP_WIDTH, D_STATE),
                 (c_ref, cwc_ref, cbias_c_ref, cc_s, GROUP_WIDTH + D_STATE, D_STATE))
    zeros = jnp.zeros((PAD, pad_s.shape[1]), F32)
    pad_s[0:PAD, :] = zeros
    pad_s[L + PAD:L + 2 * PAD, :] = zeros
    for src_ref, _, _, _, off, width in conv_srcs:
        for c in range(nc):
            pad_s[PAD + c * CHUNK:PAD + (c + 1) * CHUNK, off:off + width] = (
                src_ref[c * CHUNK:(c + 1) * CHUNK, :].astype(F32))

    def conv_slab(w_ref, bias_ref, dst_s, off, c, s, zero_row):
        ls = slice(s * LANES, (s + 1) * LANES)
        ps = slice(off + s * LANES, off + (s + 1) * LANES)
        acc = jnp.broadcast_to(bias_ref[:, ls] + zero_row, (CHUNK, LANES))
        for k in range(D_CONV):
            start = PAD + c * CHUNK + k - D_CONV // 2
            acc = acc + w_ref[k:k + 1, ls] * pad_s[start:start + CHUNK, ps]
        dst_s[c * CHUNK:(c + 1) * CHUNK, ls] = _silu(acc)

    conv_items = [functools.partial(conv_slab, w_ref, bias_ref, dst_s, off, c, s)
                  for _, w_ref, bias_ref, dst_s, off, width in conv_srcs
                  for c in range(nc) for s in range(width // LANES)]


    def zero_row_of(res):
        bits = pltpu.bitcast(res[0:SUBLANES, 0:LANES], jnp.uint32)
        return pltpu.bitcast((bits >> 16) >> 16, F32)[0:1, :]

    def cum_tile(t):
        ls = slice(t * tile, (t + 1) * tile)
        res = _dot(p1_ref[...], cbc_ref[:, ls])
        cum_s[:, ls] = res
        return zero_row_of(res)

    def exp_tile(e, t):
        ls = slice(t * tile, (t + 1) * tile)
        res = _dot(p2_ref[...], eexp_ref[e, :, ls])
        exp_s[e, :, ls] = res
        return zero_row_of(res)

    spread_items = ([functools.partial(cum_tile, t) for t in range(N_HD * LANES // tile)]
                    + [functools.partial(exp_tile, e, t) for e in range(N_EXPAND)
                       for t in range(GROUP_WIDTH // tile)])

    merged = sorted([((i + 0.5) / len(conv_items), 0, f) for i, f in enumerate(conv_items)]
                    + [((i + 0.5) / len(spread_items), 1, f) for i, f in enumerate(spread_items)],
                    key=lambda item: item[:2])
    zero_row = jnp.zeros((1, LANES), F32)
    for _, is_spread, emit in merged:
        if is_spread:
            zero_row = emit()
        else:
            emit(zero_row)

    if has_init:
        sft_s[...] = s0f_ref[0].T
        sbt_s[...] = s0b_ref[0].T
    else:
        sft_s[...] = jnp.zeros_like(sft_s)
        sbt_s[...] = jnp.zeros_like(sbt_s)

    ii = lax.broadcasted_iota(jnp.int32, (CHUNK, CHUNK), 0)
    jj = lax.broadcasted_iota(jnp.int32, (CHUNK, CHUNK), 1)
    lower = jj <= ii
    diag = jj == ii
    left = jj < SSD_HEADDIM

    def fwd_chunk(c, carry):
        r0 = pl.multiple_of(c * CHUNK, CHUNK)
        rows = pl.ds(r0, CHUNK)
        xq = xc_s[rows, :]
        bq = bc_s[rows, :]
        cq = cc_s[rows, :].astype(BF16)
        rowa = row_ref[c, 0]
        rowd = row_ref[c, 1]
        cb = _dot_nt(cq, bq.astype(BF16))
        y_off = _dot(cq, sft_s[...].astype(BF16)) * exp_s[0, rows, :]
        y_parts = []
        for k in range(HEADS_PER_GROUP // 2):
            ms = []
            for r in (2 * k, 2 * k + 1):
                rb = HEADS_PER_GROUP + r
                seg_f = cum_s[rows, r * LANES:(r + 1) * LANES] - rowa[r:r + 1, :]
                seg_b = cum_s[rows, rb * LANES:(rb + 1) * LANES] - rowa[rb:rb + 1, :]
                dm = jnp.exp2(jnp.where(lower, seg_f, seg_b))
                dm = dm + jnp.where(diag, rowd[rb:rb + 1, :], 0.0)
                ms.append((cb * dm).astype(BF16))
            lhs = jnp.concatenate(ms, axis=1)
            xp = xq[:, k * LANES:(k + 1) * LANES]
            rhs = jnp.concatenate([jnp.where(left, xp, 0.0), jnp.where(left, 0.0, xp)],
                                  axis=0).astype(BF16)
            y_parts.append(_dot(lhs, rhs))
        yacc_s[rows, :] = jnp.concatenate(y_parts, axis=1) + y_off
        decay = exp_s[0, pl.ds(r0 + CHUNK - 1, 1), :]
        bt = bq.T.astype(BF16)
        bt_s[c] = bt
        sft_s[...] = sft_s[...] * decay + _dot(bt, (xq * exp_s[1, rows, :]).astype(BF16))
        return carry

    lax.fori_loop(0, nc, fwd_chunk, 0, unroll=CHUNK_UNROLL)

    def bwd_chunk(t, carry):
        c = nc - 1 - t
        r0 = pl.multiple_of(c * CHUNK, CHUNK)
        rows = pl.ds(r0, CHUNK)
        xq = xc_s[rows, :]
        cq = cc_s[rows, :].astype(BF16)
        y = (yacc_s[rows, :] + _dot(cq, sbt_s[...].astype(BF16)) * exp_s[2, rows, :]
             + dsk_ref[...] * xq)
        y = y * _silu(z_ref[rows, :].astype(F32))
        ms = jnp.mean(y * y, axis=-1, keepdims=True)
        y_ref[rows, :] = (y * lax.rsqrt(ms + EPS) * ng_ref[...]).astype(BF16)
        decay = exp_s[2, pl.ds(r0, 1), :]
        sbt_s[...] = sbt_s[...] * decay + _dot(bt_s[c], (xq * exp_s[3, rows, :]).astype(BF16))
        return carry

    lax.fori_loop(0, nc, bwd_chunk, 0, unroll=CHUNK_UNROLL)

    if emit_state:
        sf_ref[0] = sft_s[...].T
        sb_ref[0] = sbt_s[...].T


def _ssd(act, p1, p2, rowp, consts, conv_w, conv_b, dsk, ng, *, layer, L, nseq, row_block0,
         s0f=None, s0b=None, y_prev=None, state_prev=None):
    m = act.shape[0]
    depth = conv_w.shape[0]
    l = layer
    has_init = s0f is not None
    emit_state = not has_init
    nc = L // CHUNK
    gw = GROUP_WIDTH
    rb = row_block0
    off_b = SSD_WIDTH // D_STATE
    off_c = (SSD_WIDTH + BC_WIDTH) // D_STATE
    seq_scratch = [
        ((L + 2 * PAD, gw + 2 * D_STATE), F32),
        ((L, gw), F32),
        ((L, D_STATE), F32),
        ((L, D_STATE), F32),
        ((nc, D_STATE, CHUNK), BF16),
        ((L, N_HD * LANES), F32),
        ((N_EXPAND, L, gw), F32),
        ((L, gw), F32),
        ((D_STATE, gw), F32),
        ((D_STATE, gw), F32),
    ]
    seq_bytes = sum(math.prod(shape) * jnp.dtype(dt).itemsize for shape, dt in seq_scratch)
    per_step = 2 if nseq % 2 == 0 and 2 * seq_bytes <= SSD_SCRATCH_BUDGET else 1
    rb = row_block0 // per_step if row_block0 % per_step == 0 else None
    assert rb is not None
    ls = per_step * L
    in_specs = [
        pl.BlockSpec((ls, gw), lambda b, g: (rb + b, COL_XBC // gw + g)),
        pl.BlockSpec((ls, D_STATE), lambda b, g: (rb + b, COL_XBC // D_STATE + off_b + g)),
        pl.BlockSpec((ls, D_STATE), lambda b, g: (rb + b, COL_XBC // D_STATE + off_c + g)),
        pl.BlockSpec((ls, gw), lambda b, g: (rb + b, COL_ZS // gw + g)),
        pl.BlockSpec((ls, LANES), lambda b, g: (rb + b, g)),
        pl.BlockSpec((ls, LANES), lambda b, g: (rb + b, g)),
        pl.BlockSpec((per_step * nc, 2, N_HD, CHUNK), lambda b, g: (rb + b, 0, g, 0)),
        pl.BlockSpec((LANES, N_HD * LANES), lambda b, g: (0, 0)),
        pl.BlockSpec((N_EXPAND, LANES, gw), lambda b, g: (0, 0, 0)),
        pl.BlockSpec((None, D_CONV, gw), lambda b, g: (l, 0, g)),
        pl.BlockSpec((None, D_CONV, D_STATE), lambda b, g: (l, 0, off_b + g)),
        pl.BlockSpec((None, D_CONV, D_STATE), lambda b, g: (l, 0, off_c + g)),
        pl.BlockSpec((None, 1, gw), lambda b, g: (l, 0, g)),
        pl.BlockSpec((None, 1, D_STATE), lambda b, g: (l, 0, off_b + g)),
        pl.BlockSpec((None, 1, D_STATE), lambda b, g: (l, 0, off_c + g)),
        pl.BlockSpec((None, 1, gw), lambda b, g: (l, 0, g)),
        pl.BlockSpec((None, 1, gw), lambda b, g: (l, 0, g)),
    ]
    args = [act, act, act, act, p1, p2, rowp, consts["cbc"], consts["eexp"], conv_w, conv_w, conv_w,
            conv_b, conv_b, conv_b, dsk, ng]
    aliases = {}
    n_alias = 0
    state_spec = pl.BlockSpec((per_step, None, gw, D_STATE), lambda b, g: (b, l, g, 0))
    if has_init:
        in_specs += [state_spec, state_spec, pl.BlockSpec(memory_space=pl.ANY)]
        args += [s0f, s0b, y_prev]
        aliases = {len(args) - 1: 0}
        n_alias = 1
    elif state_prev is not None:
        in_specs += [pl.BlockSpec(memory_space=pl.ANY)] * 2
        args += list(state_prev)
        aliases = {len(args) - 2: 1, len(args) - 1: 2}
        n_alias = 2
    out_specs = [pl.BlockSpec((ls, gw), lambda b, g: (rb + b, g))]
    out_shape = [jax.ShapeDtypeStruct((m, SSD_WIDTH), BF16)]
    if emit_state:
        out_specs += [state_spec] * 2
        out_shape += [jax.ShapeDtypeStruct((nseq, depth, SSD_WIDTH, D_STATE), F32)] * 2
    scratch = [pltpu.VMEM((per_step,) + shape, dt) for shape, dt in seq_scratch]
    kern = functools.partial(_ssd_kernel, L=L, per_step=per_step, has_init=has_init,
                             emit_state=emit_state, n_alias=n_alias)
    return pl.pallas_call(
        kern,
        grid=(nseq // per_step, N_SSD_GROUPS),
        in_specs=in_specs,
        out_specs=out_specs,
        out_shape=out_shape,
        scratch_shapes=scratch,
        input_output_aliases=aliases,
        compiler_params=_cparams(("arbitrary", "arbitrary")),
        name="ssd_latent" if has_init else "ssd_context",
    )(*args)


ATTN_SCALE = HEAD_DIM ** -0.5


def _ctx_attn_kernel(sink_ref, q_ref, k_ref, v_ref, z_ref, o_ref, *, L):
    g = pl.program_id(1)
    for sq in range(q_ref.shape[0] // L):
        rows = slice(sq * L, (sq + 1) * L)
        k = k_ref[rows, :]
        v = v_ref[rows, :]
        for r in range(Q_PER_KV):
            ls = slice(r * HEAD_DIM, (r + 1) * HEAD_DIM)
            sink = sink_ref[g * Q_PER_KV + r]
            s = _dot_nt(q_ref[rows, ls], k) * ATTN_SCALE
            m = jnp.maximum(jnp.max(s, axis=-1, keepdims=True), sink)
            p = jnp.exp(s - m)
            denom = jnp.sum(p, axis=-1, keepdims=True) + jnp.exp(sink - m)
            o = _dot(p.astype(BF16), v) / denom
            o_ref[rows, ls] = (o * _silu(z_ref[rows, ls].astype(F32))).astype(BF16)


def _ctx_attention(act, sink, *, L, nseq):
    m = act.shape[0]
    gw = Q_PER_KV * HEAD_DIM
    per_step = math.gcd(nseq, 4)
    rows = per_step * L
    return pl.pallas_call(
        functools.partial(_ctx_attn_kernel, L=L),
        grid=(nseq // per_step, N_KV_HEADS),
        in_specs=[
            pl.BlockSpec(memory_space=pltpu.SMEM),
            pl.BlockSpec((rows, gw), lambda b, g: (b, COL_Q // gw + g)),
            pl.BlockSpec((rows, HEAD_DIM), lambda b, g: (b, COL_K // HEAD_DIM + g)),
            pl.BlockSpec((rows, HEAD_DIM), lambda b, g: (b, COL_V // HEAD_DIM + g)),
            pl.BlockSpec((rows, gw), lambda b, g: (b, COL_ZA // gw + g)),
        ],
        out_specs=pl.BlockSpec((rows, gw), lambda b, g: (b, g)),
        out_shape=jax.ShapeDtypeStruct((m, ATTN_WIDTH), BF16),
        compiler_params=_cparams(("arbitrary", "arbitrary")),
        name="attn_context",
    )(sink, act, act, act, act)


def _rope_tables(length):
    sec = HEAD_DIM // 2
    half = sec // 2
    d = np.arange(HEAD_DIM)
    e = d % sec
    freqs = ROPE_BASE ** (-(e % half).astype(np.float64) / half)
    t = np.arange(length)
    pos = np.where((d // sec)[None, :] == 0, (t // GRID_W)[:, None], (t % GRID_W)[:, None])
    ang = pos.astype(np.float64) * freqs[None, :]
    sign = np.where(e < half, -1.0, 1.0)[None, :]
    return (jnp.asarray(np.cos(ang), F32), jnp.asarray(np.sin(ang) * sign, F32))


def _rope(x, cos, sin_signed, first_half):
    partner = jnp.where(first_half, pltpu.roll(x, LANES - HEAD_DIM // 4, 1),
                        pltpu.roll(x, HEAD_DIM // 4, 1))
    return x * cos + partner * sin_signed


def _lat_attn_kernel(sink_ref, q_ref, k_ref, v_ref, z_ref, kc_ref, vc_ref, cos_ref, sin_ref,
                     _oprev_ref, o_ref, kctx_s, vctx_s, keys_s, vals_s, *, L, lc):
    g = pl.program_id(1)
    nb = L // CHUNK
    win = 3 * CHUNK
    rows4 = Q_PER_KV * CHUNK
    lane = lax.broadcasted_iota(jnp.int32, (CHUNK, HEAD_DIM), 1)
    first_half = (lane % (HEAD_DIM // 2)) < (HEAD_DIM // 4)

    kctx_s[...] = kc_ref[...].astype(BF16)
    vctx_s[...] = vc_ref[...].astype(BF16)
    zero_blk = jnp.zeros((CHUNK, HEAD_DIM), BF16)
    for dst in (keys_s, vals_s):
        dst[0:CHUNK, :] = zero_blk
        dst[CHUNK + L:2 * CHUNK + L, :] = zero_blk
    for n in range(nb):
        rows = slice(n * CHUNK, (n + 1) * CHUNK)
        kr = _rope(k_ref[rows, :].astype(F32), cos_ref[rows, :], sin_ref[rows, :], first_half)
        keys_s[CHUNK + n * CHUNK:CHUNK + (n + 1) * CHUNK, :] = kr.astype(BF16)
        vals_s[CHUNK + n * CHUNK:CHUNK + (n + 1) * CHUNK, :] = v_ref[rows, :]

    qi = lax.broadcasted_iota(jnp.int32, (rows4, win), 0) % CHUNK
    wi = lax.broadcasted_iota(jnp.int32, (rows4, win), 1)
    band = jnp.abs(qi - wi + CHUNK) <= WINDOW
    head = lax.broadcasted_iota(jnp.int32, (rows4, 1), 0) // CHUNK
    sink = jnp.zeros((rows4, 1), F32)
    for r in range(Q_PER_KV):
        sink = jnp.where(head == r, sink_ref[g * Q_PER_KV + r], sink)

    def block(n, carry):
        r0 = pl.multiple_of(n * CHUNK, CHUNK)
        rows = pl.ds(r0, CHUNK)
        cos = cos_ref[rows, :]
        sin = sin_ref[rows, :]
        q = jnp.concatenate(
            [_rope(q_ref[rows, r * HEAD_DIM:(r + 1) * HEAD_DIM].astype(F32), cos, sin,
                   first_half).astype(BF16) for r in range(Q_PER_KV)], axis=0)
        in_seq = (wi >= CHUNK - r0) & (wi < L + CHUNK - r0)
        s_ctx = _dot_nt(q, kctx_s[...]) * ATTN_SCALE
        s_lat = _dot_nt(q, keys_s[pl.ds(r0, win), :]) * ATTN_SCALE
        s_lat = jnp.where(in_seq, jnp.where(band, s_lat, -jnp.inf), -jnp.inf)
        m = jnp.maximum(jnp.maximum(jnp.max(s_ctx, axis=-1, keepdims=True),
                                    jnp.max(s_lat, axis=-1, keepdims=True)), sink)
        p_ctx = jnp.exp(s_ctx - m)
        p_lat = jnp.exp(s_lat - m)
        denom = (jnp.sum(p_ctx, axis=-1, keepdims=True) + jnp.sum(p_lat, axis=-1, keepdims=True)
                 + jnp.exp(sink - m))
        o = (_dot(p_ctx.astype(BF16), vctx_s[...])
             + _dot(p_lat.astype(BF16), vals_s[pl.ds(r0, win), :])) / denom
        for r in range(Q_PER_KV):
            ls = slice(r * HEAD_DIM, (r + 1) * HEAD_DIM)
            o_ref[rows, ls] = (o[r * CHUNK:(r + 1) * CHUNK, :]
                               * _silu(z_ref[rows, ls].astype(F32))).astype(BF16)
        return carry

    lax.fori_loop(0, nb, block, 0)


def _lat_attention(act, sink, cache_k, cache_v, o_prev, *, L, nseq, row_block0, layer):
    m = act.shape[0]
    gw = Q_PER_KV * HEAD_DIM
    lc = cache_k.shape[2]
    rb = row_block0
    cos, sin = _rope_tables(L)
    kc = cache_k.reshape(cache_k.shape[0], cache_k.shape[1], lc, KV_WIDTH)
    vc = cache_v.reshape(cache_v.shape[0], cache_v.shape[1], lc, KV_WIDTH)
    kern = functools.partial(_lat_attn_kernel, L=L, lc=lc)
    tab = lambda b, g: (0, 0)
    return pl.pallas_call(
        kern,
        grid=(nseq, N_KV_HEADS),
        in_specs=[
            pl.BlockSpec(memory_space=pltpu.SMEM),
            pl.BlockSpec((L, gw), lambda b, g: (rb + b, COL_Q // gw + g)),
            pl.BlockSpec((L, HEAD_DIM), lambda b, g: (rb + b, COL_K // HEAD_DIM + g)),
            pl.BlockSpec((L, HEAD_DIM), lambda b, g: (rb + b, COL_V // HEAD_DIM + g)),
            pl.BlockSpec((L, gw), lambda b, g: (rb + b, COL_ZA // gw + g)),
            pl.BlockSpec((None, None, lc, HEAD_DIM), lambda b, g: (b, layer, 0, g)),
            pl.BlockSpec((None, None, lc, HEAD_DIM), lambda b, g: (b, layer, 0, g)),
            pl.BlockSpec((L, HEAD_DIM), tab),
            pl.BlockSpec((L, HEAD_DIM), tab),
            pl.BlockSpec(memory_space=pl.ANY),
        ],
        out_specs=pl.BlockSpec((L, gw), lambda b, g: (rb + b, g)),
        out_shape=jax.ShapeDtypeStruct((m, ATTN_WIDTH), BF16),
        scratch_shapes=[pltpu.VMEM((lc, HEAD_DIM), BF16), pltpu.VMEM((lc, HEAD_DIM), BF16),
                        pltpu.VMEM((L + 2 * CHUNK, HEAD_DIM), BF16),
                        pltpu.VMEM((L + 2 * CHUNK, HEAD_DIM), BF16)],
        input_output_aliases={9: 0},
        compiler_params=_cparams(("arbitrary", "arbitrary")),
        name="attn_latent",
    )(sink, act, act, act, act, kc, vc, cos, sin, o_prev)


def _branch_kernel(oa_ref, ys_ref, wpa_ref, wps_ref, ga_ref, gs_ref, o_ref):
    a = _dot(oa_ref[...], wpa_ref[...])
    s = _dot(ys_ref[...], wps_ref[...])
    merged = _sigmoid(ga_ref[...].astype(F32)) * a + _sigmoid(gs_ref[...].astype(F32)) * s
    o_ref[...] = merged.astype(BF16)


def _branches(oa, ys, w_pa, w_ps, act, *, layer, tm):
    m = oa.shape[0]
    tn = 512
    l = layer
    return pl.pallas_call(
        _branch_kernel,
        grid=(m // tm, D_MODEL // tn),
        in_specs=[
            pl.BlockSpec((tm, ATTN_WIDTH), lambda i, j: (i, 0)),
            pl.BlockSpec((tm, SSD_WIDTH), lambda i, j: (i, 0)),
            pl.BlockSpec((None, ATTN_WIDTH, tn), lambda i, j: (l, 0, j)),
            pl.BlockSpec((None, SSD_WIDTH, tn), lambda i, j: (l, 0, j)),
            pl.BlockSpec((tm, tn), lambda i, j: (i, COL_GA // tn + j)),
            pl.BlockSpec((tm, tn), lambda i, j: (i, COL_GS // tn + j)),
        ],
        out_specs=pl.BlockSpec((tm, tn), lambda i, j: (i, j)),
        out_shape=jax.ShapeDtypeStruct((m, D_MODEL), BF16),
        compiler_params=_cparams(("arbitrary", "arbitrary")),
        name="branches",
    )(oa, ys, w_pa, w_ps, act, act)


def _out_kernel(*refs, final, na):
    if final:
        mg_ref, w_ref, xa_ref, xb_ref, gate_ref, fg_ref, ya_ref, yb_ref = refs
    else:
        mg_ref, w_ref, xa_ref, xb_ref, gate_ref, o_ref = refs
    is_ctx = pl.program_id(0) < na
    x = jnp.where(is_ctx, xa_ref[...], xb_ref[...])
    y = x + gate_ref[...] * _dot(mg_ref[...], w_ref[...])
    if not final:
        o_ref[...] = y
        return
    ms = jnp.mean(y * y, axis=-1, keepdims=True)
    y = y * lax.rsqrt(ms + EPS) * fg_ref[...]

    @pl.when(is_ctx)
    def _():
        ya_ref[...] = y

    @pl.when(jnp.logical_not(is_ctx))
    def _():
        yb_ref[...] = y


def _out_proj(merged, w_out, xa, xb, xb_offset, mod, final_g, *, layer, tm, na, group_of):
    m = merged.shape[0]
    l = layer
    final = final_g is not None
    in_specs = [
        pl.BlockSpec((tm, D_MODEL), lambda i: (i, 0)),
        pl.BlockSpec((None, D_MODEL, D_MODEL), lambda i: (l, 0, 0)),
    ] + _row_split_specs(tm, na, xb_offset) + [
        pl.BlockSpec((None, None, 1, D_MODEL), lambda i: (l, group_of(i, tm), 0, 2)),
    ]
    args = [merged, w_out, xa, xb, mod]
    if final:
        in_specs.append(pl.BlockSpec((1, D_MODEL), lambda i: (0, 0)))
        args.append(final_g.reshape(1, D_MODEL))
        out_specs = [
            pl.BlockSpec((tm, D_MODEL), lambda i: (jnp.minimum(i, na - 1), 0)),
            pl.BlockSpec((tm, D_MODEL), lambda i: (jnp.maximum(i - na, 0), 0)),
        ]
        out_shape = [jax.ShapeDtypeStruct((na * tm, D_MODEL), F32),
                     jax.ShapeDtypeStruct((m - na * tm, D_MODEL), F32)]
    else:
        out_specs = pl.BlockSpec((tm, D_MODEL), lambda i: (i, 0))
        out_shape = jax.ShapeDtypeStruct((m, D_MODEL), F32)
    return pl.pallas_call(
        functools.partial(_out_kernel, final=final, na=na),
        grid=(m // tm,),
        in_specs=in_specs,
        out_specs=out_specs,
        out_shape=out_shape,
        compiler_params=_cparams(("arbitrary",)),
        name="out_proj_final" if final else "out_proj",
    )(*args)


def _dt_permutation():
    perm = np.zeros(DT_WIDTH, np.int32)
    for g in range(N_SSD_GROUPS):
        for d in range(2):
            for r in range(HEADS_PER_GROUP):
                perm[g * 16 + d * HEADS_PER_GROUP + r] = d * N_SSD_HEADS + g * HEADS_PER_GROUP + r
    return perm


def kernel(x_prompt, x_sample, c, cache_k, cache_v, state_ssm_fwd, state_ssm_bwd, c_ctx, norm_g, w_mod, b_mod, w_in, conv_w, conv_b, attn_sink, a_log_fwd, a_log_bwd, dt_bias_fwd, dt_bias_bwd, d_skip, ssd_norm_g, w_pa, w_ps, w_out, final_norm_g):
    bc, lc, _ = x_prompt.shape
    bl, ll, _ = x_sample.shape
    depth = w_in.shape[0]
    n_ctx = bc * lc
    m = n_ctx + bl * ll
    assert n_ctx % ll == 0 and ll % lc == 0 and lc % CHUNK == 0
    assert 1 + bl <= COND_ROWS

    def group_of(i, tm):
        return jnp.maximum(i * tm - n_ctx + ll, 0) // ll

    tm_big = math.gcd(1024, math.gcd(n_ctx, ll))
    tm_small = math.gcd(512, tm_big)

    cond = jnp.zeros((COND_ROWS, D_MODEL), F32).at[0].set(c_ctx).at[1:1 + bl].set(c)
    mod = _modulation(cond, w_mod, b_mod)

    perm = _dt_permutation()
    consts = _ssd_constants()
    wdt = w_in[:, :, W_IN_DT:W_IN_DT + DT_WIDTH][:, :, perm]
    wdtt = jnp.swapaxes(wdt, 1, 2)
    bias = jnp.concatenate([dt_bias_fwd, dt_bias_bwd], axis=1)[:, perm]
    alog = jnp.concatenate([a_log_fwd, a_log_bwd], axis=1)[:, perm]
    norm_g3 = norm_g.reshape(depth, 1, D_MODEL)
    conv_b3 = conv_b.reshape(depth, 1, CONV_WIDTH)
    dsk = jnp.repeat(d_skip, SSD_HEADDIM, axis=1).reshape(depth, 1, SSD_WIDTH)
    ng = ssd_norm_g.reshape(depth, 1, SSD_WIDTH)
    w_pa_bf = w_pa.astype(BF16)
    w_ps_bf = w_ps.astype(BF16)
    w_out_bf = w_out.astype(BF16)
    s0f = state_ssm_fwd.reshape(bl, depth, SSD_WIDTH, D_STATE)
    s0b = state_ssm_bwd.reshape(bl, depth, SSD_WIDTH, D_STATE)

    na_small = n_ctx // tm_small
    na_big = n_ctx // tm_big
    xa, xb, xb_off = x_prompt.reshape(n_ctx, D_MODEL), x_sample.reshape(bl * ll, D_MODEL), 0
    k_new = v_new = states = None
    for l in range(depth):
        h, p1, p2, rowp = _prep(xa, xb, xb_off, m, mod, norm_g3, wdtt, bias, alog, consts,
                                layer=l, tm=tm_small, na=na_small, group_of=group_of)
        act, k_new, v_new = _inproj(h, w_in, k_new, v_new, layer=l, tm=tm_big, na=na_big, lc=lc,
                                    bc=bc)

        sink = attn_sink[l]
        oa = _ctx_attention(act, sink, L=lc, nseq=bc)
        oa = _lat_attention(act, sink, cache_k, cache_v, oa, L=ll, nseq=bl,
                            row_block0=n_ctx // ll, layer=l)

        ys, sf, sb = _ssd(act, p1, p2, rowp, consts, conv_w, conv_b3, dsk, ng, layer=l, L=lc,
                          nseq=bc, row_block0=0, state_prev=states)
        states = (sf, sb)
        (ys,) = _ssd(act, p1, p2, rowp, consts, conv_w, conv_b3, dsk, ng, layer=l, L=ll, nseq=bl,
                     row_block0=n_ctx // ll, s0f=s0f, s0b=s0b, y_prev=ys)

        merged = _branches(oa, ys, w_pa_bf, w_ps_bf, act, layer=l, tm=tm_big)
        last = l == depth - 1
        res = _out_proj(merged, w_out_bf, xa, xb, xb_off, mod, final_norm_g if last else None,
                        layer=l, tm=tm_small, na=na_small, group_of=group_of)
        if not last:
            xa, xb, xb_off = res, res, na_small

    y_prompt = res[0].reshape(bc, lc, D_MODEL)
    y_sample = res[1].reshape(bl, ll, D_MODEL)
    shape_kv = (bc, depth, lc, N_KV_HEADS, HEAD_DIM)
    shape_st = (bc, depth, N_SSD_HEADS, SSD_HEADDIM, D_STATE)
    return (y_prompt, y_sample, k_new.reshape(shape_kv), v_new.reshape(shape_kv),
            states[0].reshape(shape_st), states[1].reshape(shape_st))
```

```python
import functools
import math

import numpy as np
import jax
import jax.numpy as jnp
from jax import lax
from jax.experimental import pallas as pl
from jax.experimental.pallas import tpu as pltpu

F32 = jnp.float32
BF16 = jnp.bfloat16

D_MODEL = 2048
HEAD_DIM = 128
N_Q_HEADS = 16
N_KV_HEADS = 4
Q_PER_KV = 4
ATTN_WIDTH = 2048
KV_WIDTH = 512
WINDOW = 128
GRID_W = 64
ROPE_BASE = 10000.0
SSD_WIDTH = 4096
SSD_HEADDIM = 64
N_SSD_HEADS = 64
D_STATE = 128
N_SSD_GROUPS = 8
HEADS_PER_GROUP = 8
GROUP_WIDTH = SSD_WIDTH // N_SSD_GROUPS
CHUNK = 128
D_CONV = 5
BC_WIDTH = 1024
CONV_WIDTH = 6144
EPS = 1e-6
MOD_WIDTH = 3 * D_MODEL
LOG2_E = math.log2(math.e)

COL_Q = 0
COL_K = 2048
COL_V = 2560
COL_ZA = 3072
COL_XBC = 5120
COL_ZS = 11264
COL_GA = 15360
COL_GS = 17408
ACT_WIDTH = 19456
W_IN_DT = 15360
DT_WIDTH = 2 * N_SSD_HEADS

LANES = 128
SUBLANES = 8
VMEM_LIMIT = 56 * 1024 * 1024

COND_ROWS = 8


def _cparams(sem):
    return pltpu.CompilerParams(dimension_semantics=sem, vmem_limit_bytes=VMEM_LIMIT)


def _dot(a, b):
    return jnp.dot(a, b, preferred_element_type=F32)


def _dot_nt(a, b):
    return lax.dot_general(a, b, (((1,), (1,)), ((), ())), preferred_element_type=F32)


def _split2(x):
    hi = x.astype(BF16)
    lo = (x - hi.astype(F32)).astype(BF16)
    return hi, lo


def _split3(x):
    p1 = x.astype(BF16)
    r1 = x - p1.astype(F32)
    p2 = r1.astype(BF16)
    p3 = (r1 - p2.astype(F32)).astype(BF16)
    return p1, p2, p3


def _dot3(a, b):
    ah, al = _split2(a)
    bh, bl = _split2(b)
    return _dot(ah, bh) + _dot(al, bh) + _dot(ah, bl)


def _sigmoid(x):
    return 0.5 + 0.5 * jnp.tanh(0.5 * x)


def _silu(x):
    half = 0.5 * x
    return half + half * jnp.tanh(half)


def _softplus(x):
    return jnp.maximum(x, 0.0) + jnp.log1p(jnp.exp(-jnp.abs(x)))


def _mod_kernel(cond_ref, w_ref, b_ref, o_ref):
    res = _dot3(_silu(cond_ref[...]), w_ref[...]) + b_ref[...]
    for r in range(COND_ROWS):
        o_ref[r] = res[r:r + 1, :]


def _modulation(cond, w_mod, b_mod):
    depth = w_mod.shape[0]
    tn = 512
    return pl.pallas_call(
        _mod_kernel,
        grid=(depth, MOD_WIDTH // tn),
        in_specs=[
            pl.BlockSpec((COND_ROWS, D_MODEL), lambda l, j: (0, 0)),
            pl.BlockSpec((None, D_MODEL, tn), lambda l, j: (l, 0, j)),
            pl.BlockSpec((None, 1, tn), lambda l, j: (l, 0, j)),
        ],
        out_specs=pl.BlockSpec((None, COND_ROWS, 1, tn), lambda l, j: (l, 0, 0, j)),
        out_shape=jax.ShapeDtypeStruct((depth, COND_ROWS, 1, MOD_WIDTH), F32),
        compiler_params=_cparams(("arbitrary", "arbitrary")),
        name="modulation",
    )(cond, w_mod, b_mod.reshape(depth, 1, MOD_WIDTH))


def _prep_kernel(xa_ref, xb_ref, shift_ref, scale_ref, g_ref, wdtt_ref, biast_ref, alogt_ref,
                 sel1_ref, sel2_ref, h_ref, p1_ref, p2_ref, row_ref, *, tm, na):
    x = jnp.where(pl.program_id(0) < na, xa_ref[...], xb_ref[...])
    ms = jnp.mean(x * x, axis=-1, keepdims=True)
    h = (x * lax.rsqrt(ms + EPS) * g_ref[...]) * (1.0 + scale_ref[...]) + shift_ref[...]
    h_ref[...] = h.astype(BF16)

    hh, hl = _split2(h)
    wth, wtl = _split2(wdtt_ref[...])
    rawt = _dot_nt(wth, hh) + _dot_nt(wth, hl) + _dot_nt(wtl, hh)
    dtt = _softplus(rawt + biast_ref[...])
    dtat = dtt * (-jnp.exp(alogt_ref[...]))

    ii = lax.broadcasted_iota(jnp.int32, (CHUNK, CHUNK), 0)
    kk = lax.broadcasted_iota(jnp.int32, (CHUNK, CHUNK), 1)
    lt = jnp.where(kk <= ii, 1.0, 0.0).astype(BF16)
    ut = jnp.where(kk >= ii, 1.0, 0.0).astype(BF16)
    fwd_row = (ii % 16) < HEADS_PER_GROUP

    for c in range(tm // CHUNK):
        rows = slice(c * CHUNK, (c + 1) * CHUNK)
        dt_t = dtt[:, rows]
        q1, q2, q3 = _split3(dtat[:, rows])
        pre_t = _dot(q1, ut) + _dot(q2, ut) + _dot(q3, ut)
        suf_t = _dot(q1, lt) + _dot(q2, lt) + _dot(q3, lt)
        acs_t = jnp.where(fwd_row, pre_t, suf_t)
        edge_t = jnp.where(fwd_row, acs_t[:, CHUNK - 1:CHUNK], acs_t[:, 0:1])
        w1_t = dt_t * jnp.exp(edge_t - acs_t)
        row_ref[c, 0] = (acs_t - jnp.log(dt_t)) * LOG2_E
        row_ref[c, 1] = dt_t
        acs = acs_t.T
        p1 = _dot(jnp.concatenate(_split3(acs * LOG2_E), axis=1), sel1_ref[...])
        p1_ref[rows, :] = p1.astype(BF16)
        p2 = _dot(jnp.concatenate(_split3(jnp.exp(acs)) + _split3(w1_t.T), axis=1), sel2_ref[...])
        p2_ref[rows, :] = p2.astype(BF16)


def _row_split_specs(tm, na, xb_offset):
    return [
        pl.BlockSpec((tm, D_MODEL), lambda i: (jnp.minimum(i, na - 1), 0)),
        pl.BlockSpec((tm, D_MODEL), lambda i: (xb_offset + jnp.maximum(i - na, 0), 0)),
    ]


def _prep(xa, xb, xb_offset, m, mod, norm_g, wdtt, bias, alog, consts, *, layer, tm, na, group_of):
    l = layer
    kern = functools.partial(_prep_kernel, tm=tm, na=na)
    par2 = lambda i: (l, 0, 0)
    whole = lambda i: (0, 0)
    pw = N_SSD_GROUPS * LANES
    return pl.pallas_call(
        kern,
        grid=(m // tm,),
        in_specs=_row_split_specs(tm, na, xb_offset) + [
            pl.BlockSpec((None, None, 1, D_MODEL), lambda i: (l, group_of(i, tm), 0, 0)),
            pl.BlockSpec((None, None, 1, D_MODEL), lambda i: (l, group_of(i, tm), 0, 1)),
            pl.BlockSpec((None, 1, D_MODEL), par2),
            pl.BlockSpec((None, DT_WIDTH, D_MODEL), par2),
            pl.BlockSpec((None, DT_WIDTH, 1), par2),
            pl.BlockSpec((None, DT_WIDTH, 1), par2),
            pl.BlockSpec((N_PIECES * LANES, pw), whole),
            pl.BlockSpec((2 * N_PIECES * LANES, pw), whole),
        ],
        out_specs=[
            pl.BlockSpec((tm, D_MODEL), lambda i: (i, 0)),
            pl.BlockSpec((tm, pw), lambda i: (i, 0)),
            pl.BlockSpec((tm, pw), lambda i: (i, 0)),
            pl.BlockSpec((tm // CHUNK, 2, DT_WIDTH, CHUNK), lambda i: (i, 0, 0, 0)),
        ],
        out_shape=[
            jax.ShapeDtypeStruct((m, D_MODEL), BF16),
            jax.ShapeDtypeStruct((m, pw), BF16),
            jax.ShapeDtypeStruct((m, pw), BF16),
            jax.ShapeDtypeStruct((m // CHUNK, 2, DT_WIDTH, CHUNK), F32),
        ],
        compiler_params=_cparams(("arbitrary",)),
        name="prep",
    )(xa, xb, mod, mod, norm_g, wdtt, bias[:, :, None], alog[:, :, None], consts["sel1"],
      consts["sel2"])


N_PIECES = 3
N_HD = 2 * HEADS_PER_GROUP
PIECE_LANES = N_PIECES * N_HD
N_EXPAND = 4


def _ssd_constants():
    pw = N_SSD_GROUPS * LANES
    sel1 = np.zeros((N_PIECES * LANES, pw), np.float32)
    sel2 = np.zeros((2 * N_PIECES * LANES, pw), np.float32)
    for g in range(N_SSD_GROUPS):
        for hd in range(N_HD):
            lam = g * N_HD + hd
            for p in range(N_PIECES):
                sel1[p * LANES + lam, g * LANES + N_PIECES * hd + p] = 1.0
                for q in range(2):
                    sel2[(q * N_PIECES + p) * LANES + lam,
                         g * LANES + q * PIECE_LANES + N_PIECES * hd + p] = 1.0
    cbc = np.zeros((LANES, N_HD * LANES), np.float32)
    for hd in range(N_HD):
        cbc[N_PIECES * hd:N_PIECES * (hd + 1), hd * LANES:(hd + 1) * LANES] = 1.0
    eexp = np.zeros((N_EXPAND, LANES, GROUP_WIDTH), np.float32)
    for e, (q, d) in enumerate(((0, 0), (1, 0), (0, 1), (1, 1))):
        for r in range(HEADS_PER_GROUP):
            row0 = q * PIECE_LANES + N_PIECES * (d * HEADS_PER_GROUP + r)
            eexp[e, row0:row0 + N_PIECES, r * SSD_HEADDIM:(r + 1) * SSD_HEADDIM] = 1.0
    return {k: jnp.asarray(v, BF16) for k, v in
            (("sel1", sel1), ("sel2", sel2), ("cbc", cbc), ("eexp", eexp))}


INPROJ_TN = 1024
KV_BLOCK = COL_K // INPROJ_TN


def _inproj_kernel(*refs, na, lc, aliased):
    if aliased:
        h_ref, w_ref, _kprev, _vprev, act_ref, k_ref, v_ref, wbf_s = refs
    else:
        h_ref, w_ref, act_ref, k_ref, v_ref, wbf_s = refs
    j = pl.program_id(0)
    i = pl.program_id(1)

    @pl.when(i == 0)
    def _():
        wbf_s[...] = w_ref[...].astype(BF16)

    acc = _dot(h_ref[...], wbf_s[...])
    act_ref[...] = acc.astype(BF16)

    @pl.when((j == KV_BLOCK) & (i < na))
    def _():
        for s in range(k_ref.shape[0]):
            for dst, col0 in ((k_ref, 0), (v_ref, KV_WIDTH)):
                for hh in range(N_KV_HEADS):
                    cols = slice(col0 + hh * HEAD_DIM, col0 + (hh + 1) * HEAD_DIM)
                    dst[s, pl.ds(hh, lc, stride=N_KV_HEADS), :] = acc[s * lc:(s + 1) * lc, cols]


def _inproj(h, w_in, k_prev, v_prev, *, layer, tm, na, lc, bc):
    m = h.shape[0]
    depth = w_in.shape[0]
    tn = INPROJ_TN
    l = layer
    spb = tm // lc
    aliased = k_prev is not None

    def w_col(j):
        skip = jnp.where(j >= W_IN_DT // tn, DT_WIDTH // LANES, 0)
        return (j * (tn // LANES) + skip) * LANES

    def kv_idx(j, i):
        return jnp.where(j < KV_BLOCK, 0, jnp.where(j == KV_BLOCK, jnp.minimum(i, na - 1), na - 1))

    in_specs = [
        pl.BlockSpec((tm, D_MODEL), lambda j, i: (i, 0)),
        pl.BlockSpec((None, pl.Element(D_MODEL), pl.Element(tn)), lambda j, i: (l, 0, w_col(j))),
    ]
    args = [h, w_in]
    aliases = {}
    if aliased:
        in_specs += [pl.BlockSpec(memory_space=pl.ANY)] * 2
        args += [k_prev, v_prev]
        aliases = {2: 1, 3: 2}
    kv_rows = lc * N_KV_HEADS
    kv_spec = pl.BlockSpec((spb, None, kv_rows, HEAD_DIM), lambda j, i: (kv_idx(j, i), l, 0, 0))
    return pl.pallas_call(
        functools.partial(_inproj_kernel, na=na, lc=lc, aliased=aliased),
        grid=(ACT_WIDTH // tn, m // tm),
        in_specs=in_specs,
        out_specs=[pl.BlockSpec((tm, tn), lambda j, i: (i, j)), kv_spec, kv_spec],
        out_shape=[
            jax.ShapeDtypeStruct((m, ACT_WIDTH), BF16),
            jax.ShapeDtypeStruct((bc, depth, kv_rows, HEAD_DIM), F32),
            jax.ShapeDtypeStruct((bc, depth, kv_rows, HEAD_DIM), F32),
        ],
        scratch_shapes=[pltpu.VMEM((D_MODEL, tn), BF16)],
        input_output_aliases=aliases,
        compiler_params=_cparams(("arbitrary", "arbitrary")),
        name="inproj",
    )(*args)


PAD = SUBLANES
CHUNK_UNROLL = 2
SSD_SCRATCH_BUDGET = 28 * 1024 * 1024


def _ssd_kernel(*refs, L, per_step, has_init, emit_state, n_alias):
    refs = list(refs)
    seq_in = refs[:7]
    shared = refs[7:17]
    pos = 17
    state_in = []
    if has_init:
        state_in = refs[pos:pos + 2]
        pos += 2
    pos += n_alias
    y_ref = refs[pos]
    pos += 1
    state_out = []
    if emit_state:
        state_out = refs[pos:pos + 2]
        pos += 2
    scratch = refs[pos:]
    nc = L // CHUNK
    for sq in range(per_step):
        rows = pl.ds(sq * L, L)
        views = [r.at[rows] for r in seq_in[:6]] + [seq_in[6].at[pl.ds(sq * nc, nc)]]
        _ssd_sequence(views, shared, [r.at[pl.ds(sq, 1)] for r in state_in], y_ref.at[rows],
                      [r.at[pl.ds(sq, 1)] for r in state_out], [s.at[sq] for s in scratch], L=L)


def _ssd_sequence(seq_in, shared, state_in, y_ref, state_out, scratch, *, L):
    x_ref, b_ref, c_ref, z_ref, p1_ref, p2_ref, row_ref = seq_in
    (cbc_ref, eexp_ref, cwx_ref, cwb_ref, cwc_ref, cbx_ref, cbb_ref, cbias_c_ref, dsk_ref,
     ng_ref) = shared
    has_init = bool(state_in)
    emit_state = bool(state_out)
    if has_init:
        s0f_ref, s0b_ref = state_in
    if emit_state:
        sf_ref, sb_ref = state_out
    pad_s, xc_s, bc_s, cc_s, bt_s, cum_s, exp_s, yacc_s, sft_s, sbt_s = scratch
    nc = L // CHUNK

    tile = 2 * LANES
    conv_srcs = ((x_ref, cwx_ref, cbx_ref, xc_s, 0, GROUP_WIDTH),
                 (b_ref, cwb_ref, cbb_ref, bc_s, GROUP_WIDTH, D_STATE),
                 (c_ref, cwc_ref, cbias_c_ref, cc_s, GROUP_WIDTH + D_STATE, D_STATE))
    zeros = jnp.zeros((PAD, pad_s.shape[1]), F32)
    pad_s[0:PAD, :] = zeros
    pad_s[L + PAD:L + 2 * PAD, :] = zeros
    for src_ref, _, _, _, off, width in conv_srcs:
        for c in range(nc):
            pad_s[PAD + c * CHUNK:PAD + (c + 1) * CHUNK, off:off + width] = (
                src_ref[c * CHUNK:(c + 1) * CHUNK, :].astype(F32))

    def conv_slab(w_ref, bias_ref, dst_s, off, c, s, zero_row):
        ls = slice(s * LANES, (s + 1) * LANES)
        ps = slice(off + s * LANES, off + (s + 1) * LANES)
        halo = pad_s[c * CHUNK:(c + 1) * CHUNK + 2 * PAD, ps]
        centre = D_CONV // 2
        acc = jnp.broadcast_to(bias_ref[:, ls] + zero_row, (CHUNK, LANES))
        for k in range(D_CONV):
            shifted = halo if k == centre else pltpu.roll(halo, (centre - k) % halo.shape[0], 0)
            acc = acc + w_ref[k:k + 1, ls] * shifted[PAD:PAD + CHUNK, :]
        dst_s[c * CHUNK:(c + 1) * CHUNK, ls] = _silu(acc)

    conv_items = [functools.partial(conv_slab, w_ref, bias_ref, dst_s, off, c, s)
                  for _, w_ref, bias_ref, dst_s, off, width in conv_srcs
                  for c in range(nc) for s in range(width // LANES)]


    def zero_row_of(res):
        bits = pltpu.bitcast(res[0:SUBLANES, 0:LANES], jnp.uint32)
        return pltpu.bitcast((bits >> 16) >> 16, F32)[0:1, :]

    def cum_tile(t):
        ls = slice(t * tile, (t + 1) * tile)
        res = _dot(p1_ref[...], cbc_ref[:, ls])
        cum_s[:, ls] = res
        return zero_row_of(res)

    def exp_tile(e, t):
        ls = slice(t * tile, (t + 1) * tile)
        res = _dot(p2_ref[...], eexp_ref[e, :, ls])
        exp_s[e, :, ls] = res
        return zero_row_of(res)

    spread_items = ([functools.partial(cum_tile, t) for t in range(N_HD * LANES // tile)]
                    + [functools.partial(exp_tile, e, t) for e in range(N_EXPAND)
                       for t in range(GROUP_WIDTH // tile)])

    merged = sorted([((i + 0.5) / len(conv_items), 0, f) for i, f in enumerate(conv_items)]
                    + [((i + 0.5) / len(spread_items), 1, f) for i, f in enumerate(spread_items)],
                    key=lambda item: item[:2])
    zero_row = jnp.zeros((1, LANES), F32)
    for _, is_spread, emit in merged:
        if is_spread:
            zero_row = emit()
        else:
            emit(zero_row)

    if has_init:
        sft_s[...] = s0f_ref[0].T
        sbt_s[...] = s0b_ref[0].T
    else:
        sft_s[...] = jnp.zeros_like(sft_s)
        sbt_s[...] = jnp.zeros_like(sbt_s)

    ii = lax.broadcasted_iota(jnp.int32, (CHUNK, CHUNK), 0)
    jj = lax.broadcasted_iota(jnp.int32, (CHUNK, CHUNK), 1)
    lower = jj <= ii
    diag = jj == ii
    left = jj < SSD_HEADDIM

    def fwd_chunk(c, carry):
        r0 = pl.multiple_of(c * CHUNK, CHUNK)
        rows = pl.ds(r0, CHUNK)
        xq = xc_s[rows, :]
        bq = bc_s[rows, :]
        cq = cc_s[rows, :].astype(BF16)
        rowa = row_ref[c, 0]
        rowd = row_ref[c, 1]
        cb = _dot_nt(cq, bq.astype(BF16))
        y_off = _dot(cq, sft_s[...].astype(BF16)) * exp_s[0, rows, :]
        y_parts = []
        for k in range(HEADS_PER_GROUP // 2):
            ms = []
            for r in (2 * k, 2 * k + 1):
                rb = HEADS_PER_GROUP + r
                seg_f = cum_s[rows, r * LANES:(r + 1) * LANES] - rowa[r:r + 1, :]
                seg_b = cum_s[rows, rb * LANES:(rb + 1) * LANES] - rowa[rb:rb + 1, :]
                dm = jnp.exp2(jnp.where(lower, seg_f, seg_b))
                dm = dm + jnp.where(diag, rowd[rb:rb + 1, :], 0.0)
                ms.append((cb * dm).astype(BF16))
            lhs = jnp.concatenate(ms, axis=1)
            xp = xq[:, k * LANES:(k + 1) * LANES]
            rhs = jnp.concatenate([jnp.where(left, xp, 0.0), jnp.where(left, 0.0, xp)],
                                  axis=0).astype(BF16)
            y_parts.append(_dot(lhs, rhs))
        yacc_s[rows, :] = jnp.concatenate(y_parts, axis=1) + y_off
        decay = exp_s[0, pl.ds(r0 + CHUNK - 1, 1), :]
        bt = bq.T.astype(BF16)
        bt_s[c] = bt
        sft_s[...] = sft_s[...] * decay + _dot(bt, (xq * exp_s[1, rows, :]).astype(BF16))
        return carry

    lax.fori_loop(0, nc, fwd_chunk, 0, unroll=CHUNK_UNROLL)

    def bwd_chunk(t, carry):
        c = nc - 1 - t
        r0 = pl.multiple_of(c * CHUNK, CHUNK)
        rows = pl.ds(r0, CHUNK)
        xq = xc_s[rows, :]
        cq = cc_s[rows, :].astype(BF16)
        y = (yacc_s[rows, :] + _dot(cq, sbt_s[...].astype(BF16)) * exp_s[2, rows, :]
             + dsk_ref[...] * xq)
        y = y * _silu(z_ref[rows, :].astype(F32))
        ms = jnp.mean(y * y, axis=-1, keepdims=True)
        y_ref[rows, :] = (y * lax.rsqrt(ms + EPS) * ng_ref[...]).astype(BF16)
        decay = exp_s[2, pl.ds(r0, 1), :]
        sbt_s[...] = sbt_s[...] * decay + _dot(bt_s[c], (xq * exp_s[3, rows, :]).astype(BF16))
        return carry

    lax.fori_loop(0, nc, bwd_chunk, 0, unroll=CHUNK_UNROLL)

    if emit_state:
        sf_ref[0] = sft_s[...].T
        sb_ref[0] = sbt_s[...].T


def _ssd(act, p1, p2, rowp, consts, conv_w, conv_b, dsk, ng, *, layer, L, nseq, row_block0,
         s0f=None, s0b=None, y_prev=None, state_prev=None):
    m = act.shape[0]
    depth = conv_w.shape[0]
    l = layer
    has_init = s0f is not None
    emit_state = not has_init
    nc = L // CHUNK
    gw = GROUP_WIDTH
    rb = row_block0
    off_b = SSD_WIDTH // D_STATE
    off_c = (SSD_WIDTH + BC_WIDTH) // D_STATE
    seq_scratch = [
        ((L + 2 * PAD, gw + 2 * D_STATE), F32),
        ((L, gw), F32),
        ((L, D_STATE), F32),
        ((L, D_STATE), F32),
        ((nc, D_STATE, CHUNK), BF16),
        ((L, N_HD * LANES), F32),
        ((N_EXPAND, L, gw), F32),
        ((L, gw), F32),
        ((D_STATE, gw), F32),
        ((D_STATE, gw), F32),
    ]
    seq_bytes = sum(math.prod(shape) * jnp.dtype(dt).itemsize for shape, dt in seq_scratch)
    per_step = max(n for n in (1, 2, 4)
                   if nseq % n == 0 and row_block0 % n == 0 and n * seq_bytes <= SSD_SCRATCH_BUDGET)
    rb = row_block0 // per_step
    ls = per_step * L
    in_specs = [
        pl.BlockSpec((ls, gw), lambda b, g: (rb + b, COL_XBC // gw + g)),
        pl.BlockSpec((ls, D_STATE), lambda b, g: (rb + b, COL_XBC // D_STATE + off_b + g)),
        pl.BlockSpec((ls, D_STATE), lambda b, g: (rb + b, COL_XBC // D_STATE + off_c + g)),
        pl.BlockSpec((ls, gw), lambda b, g: (rb + b, COL_ZS // gw + g)),
        pl.BlockSpec((ls, LANES), lambda b, g: (rb + b, g)),
        pl.BlockSpec((ls, LANES), lambda b, g: (rb + b, g)),
        pl.BlockSpec((per_step * nc, 2, N_HD, CHUNK), lambda b, g: (rb + b, 0, g, 0)),
        pl.BlockSpec((LANES, N_HD * LANES), lambda b, g: (0, 0)),
        pl.BlockSpec((N_EXPAND, LANES, gw), lambda b, g: (0, 0, 0)),
        pl.BlockSpec((None, D_CONV, gw), lambda b, g: (l, 0, g)),
        pl.BlockSpec((None, D_CONV, D_STATE), lambda b, g: (l, 0, off_b + g)),
        pl.BlockSpec((None, D_CONV, D_STATE), lambda b, g: (l, 0, off_c + g)),
        pl.BlockSpec((None, 1, gw), lambda b, g: (l, 0, g)),
        pl.BlockSpec((None, 1, D_STATE), lambda b, g: (l, 0, off_b + g)),
        pl.BlockSpec((None, 1, D_STATE), lambda b, g: (l, 0, off_c + g)),
        pl.BlockSpec((None, 1, gw), lambda b, g: (l, 0, g)),
        pl.BlockSpec((None, 1, gw), lambda b, g: (l, 0, g)),
    ]
    args = [act, act, act, act, p1, p2, rowp, consts["cbc"], consts["eexp"], conv_w, conv_w, conv_w,
            conv_b, conv_b, conv_b, dsk, ng]
    aliases = {}
    n_alias = 0
    state_spec = pl.BlockSpec((per_step, None, gw, D_STATE), lambda b, g: (b, l, g, 0))
    if has_init:
        in_specs += [state_spec, state_spec, pl.BlockSpec(memory_space=pl.ANY)]
        args += [s0f, s0b, y_prev]
        aliases = {len(args) - 1: 0}
        n_alias = 1
    elif state_prev is not None:
        in_specs += [pl.BlockSpec(memory_space=pl.ANY)] * 2
        args += list(state_prev)
        aliases = {len(args) - 2: 1, len(args) - 1: 2}
        n_alias = 2
    out_specs = [pl.BlockSpec((ls, gw), lambda b, g: (rb + b, g))]
    out_shape = [jax.ShapeDtypeStruct((m, SSD_WIDTH), BF16)]
    if emit_state:
        out_specs += [state_spec] * 2
        out_shape += [jax.ShapeDtypeStruct((nseq, depth, SSD_WIDTH, D_STATE), F32)] * 2
    scratch = [pltpu.VMEM((per_step,) + shape, dt) for shape, dt in seq_scratch]
    kern = functools.partial(_ssd_kernel, L=L, per_step=per_step, has_init=has_init,
                             emit_state=emit_state, n_alias=n_alias)
    return pl.pallas_call(
        kern,
        grid=(nseq // per_step, N_SSD_GROUPS),
        in_specs=in_specs,
        out_specs=out_specs,
        out_shape=out_shape,
        scratch_shapes=scratch,
        input_output_aliases=aliases,
        compiler_params=_cparams(("arbitrary", "arbitrary")),
        name="ssd_latent" if has_init else "ssd_context",
    )(*args)


ATTN_SCALE = HEAD_DIM ** -0.5


def _ctx_attn_kernel(sink_ref, q_ref, k_ref, v_ref, z_ref, o_ref, *, L):
    g = pl.program_id(1)
    for sq in range(q_ref.shape[0] // L):
        rows = slice(sq * L, (sq + 1) * L)
        k = k_ref[rows, :]
        v = v_ref[rows, :]
        for r in range(Q_PER_KV):
            ls = slice(r * HEAD_DIM, (r + 1) * HEAD_DIM)
            sink = sink_ref[g * Q_PER_KV + r]
            s = _dot_nt(q_ref[rows, ls], k) * ATTN_SCALE
            m = jnp.maximum(jnp.max(s, axis=-1, keepdims=True), sink)
            p = jnp.exp(s - m)
            denom = jnp.sum(p, axis=-1, keepdims=True) + jnp.exp(sink - m)
            o = _dot(p.astype(BF16), v) / denom
            o_ref[rows, ls] = (o * _silu(z_ref[rows, ls].astype(F32))).astype(BF16)


def _ctx_attention(act, sink, *, L, nseq):
    m = act.shape[0]
    gw = Q_PER_KV * HEAD_DIM
    per_step = math.gcd(nseq, 4)
    rows = per_step * L
    return pl.pallas_call(
        functools.partial(_ctx_attn_kernel, L=L),
        grid=(nseq // per_step, N_KV_HEADS),
        in_specs=[
            pl.BlockSpec(memory_space=pltpu.SMEM),
            pl.BlockSpec((rows, gw), lambda b, g: (b, COL_Q // gw + g)),
            pl.BlockSpec((rows, HEAD_DIM), lambda b, g: (b, COL_K // HEAD_DIM + g)),
            pl.BlockSpec((rows, HEAD_DIM), lambda b, g: (b, COL_V // HEAD_DIM + g)),
            pl.BlockSpec((rows, gw), lambda b, g: (b, COL_ZA // gw + g)),
        ],
        out_specs=pl.BlockSpec((rows, gw), lambda b, g: (b, g)),
        out_shape=jax.ShapeDtypeStruct((m, ATTN_WIDTH), BF16),
        compiler_params=_cparams(("arbitrary", "arbitrary")),
        name="attn_context",
    )(sink, act, act, act, act)


def _rope_tables(length):
    sec = HEAD_DIM // 2
    half = sec // 2
    d = np.arange(HEAD_DIM)
    e = d % sec
    freqs = ROPE_BASE ** (-(e % half).astype(np.float64) / half)
    t = np.arange(length)
    pos = np.where((d // sec)[None, :] == 0, (t // GRID_W)[:, None], (t % GRID_W)[:, None])
    ang = pos.astype(np.float64) * freqs[None, :]
    sign = np.where(e < half, -1.0, 1.0)[None, :]
    return (jnp.asarray(np.cos(ang), F32), jnp.asarray(np.sin(ang) * sign, F32))


def _rope(x, cos, sin_signed, first_half):
    partner = jnp.where(first_half, pltpu.roll(x, LANES - HEAD_DIM // 4, 1),
                        pltpu.roll(x, HEAD_DIM // 4, 1))
    return x * cos + partner * sin_signed


def _lat_attn_kernel(sink_ref, q_ref, k_ref, v_ref, z_ref, kc_ref, vc_ref, cos_ref, sin_ref,
                     _oprev_ref, o_ref, kctx_s, vctx_s, keys_s, vals_s, *, L, lc):
    g = pl.program_id(1)
    nb = L // CHUNK
    win = 3 * CHUNK
    rows4 = Q_PER_KV * CHUNK
    lane = lax.broadcasted_iota(jnp.int32, (CHUNK, HEAD_DIM), 1)
    first_half = (lane % (HEAD_DIM // 2)) < (HEAD_DIM // 4)

    kctx_s[...] = kc_ref[...].astype(BF16)
    vctx_s[...] = vc_ref[...].astype(BF16)
    zero_blk = jnp.zeros((CHUNK, HEAD_DIM), BF16)
    for dst in (keys_s, vals_s):
        dst[0:CHUNK, :] = zero_blk
        dst[CHUNK + L:2 * CHUNK + L, :] = zero_blk
    for n in range(nb):
        rows = slice(n * CHUNK, (n + 1) * CHUNK)
        kr = _rope(k_ref[rows, :].astype(F32), cos_ref[rows, :], sin_ref[rows, :], first_half)
        keys_s[CHUNK + n * CHUNK:CHUNK + (n + 1) * CHUNK, :] = kr.astype(BF16)
        vals_s[CHUNK + n * CHUNK:CHUNK + (n + 1) * CHUNK, :] = v_ref[rows, :]

    qi = lax.broadcasted_iota(jnp.int32, (rows4, win), 0) % CHUNK
    wi = lax.broadcasted_iota(jnp.int32, (rows4, win), 1)
    band = jnp.abs(qi - wi + CHUNK) <= WINDOW
    head = lax.broadcasted_iota(jnp.int32, (rows4, 1), 0) // CHUNK
    sink = jnp.zeros((rows4, 1), F32)
    for r in range(Q_PER_KV):
        sink = jnp.where(head == r, sink_ref[g * Q_PER_KV + r], sink)

    def block(n, carry):
        r0 = pl.multiple_of(n * CHUNK, CHUNK)
        rows = pl.ds(r0, CHUNK)
        cos = cos_ref[rows, :]
        sin = sin_ref[rows, :]
        q = jnp.concatenate(
            [_rope(q_ref[rows, r * HEAD_DIM:(r + 1) * HEAD_DIM].astype(F32), cos, sin,
                   first_half).astype(BF16) for r in range(Q_PER_KV)], axis=0)
        in_seq = (wi >= CHUNK - r0) & (wi < L + CHUNK - r0)
        s_ctx = _dot_nt(q, kctx_s[...]) * ATTN_SCALE
        s_lat = _dot_nt(q, keys_s[pl.ds(r0, win), :]) * ATTN_SCALE
        s_lat = jnp.where(in_seq, jnp.where(band, s_lat, -jnp.inf), -jnp.inf)
        m = jnp.maximum(jnp.maximum(jnp.max(s_ctx, axis=-1, keepdims=True),
                                    jnp.max(s_lat, axis=-1, keepdims=True)), sink)
        p_ctx = jnp.exp(s_ctx - m)
        p_lat = jnp.exp(s_lat - m)
        denom = (jnp.sum(p_ctx, axis=-1, keepdims=True) + jnp.sum(p_lat, axis=-1, keepdims=True)
                 + jnp.exp(sink - m))
        o = (_dot(p_ctx.astype(BF16), vctx_s[...])
             + _dot(p_lat.astype(BF16), vals_s[pl.ds(r0, win), :])) / denom
        for r in range(Q_PER_KV):
            ls = slice(r * HEAD_DIM, (r + 1) * HEAD_DIM)
            o_ref[rows, ls] = (o[r * CHUNK:(r + 1) * CHUNK, :]
                               * _silu(z_ref[rows, ls].astype(F32))).astype(BF16)
        return carry

    lax.fori_loop(0, nb, block, 0, unroll=2)


def _lat_attention(act, sink, cache_k, cache_v, o_prev, *, L, nseq, row_block0, layer):
    m = act.shape[0]
    gw = Q_PER_KV * HEAD_DIM
    lc = cache_k.shape[2]
    rb = row_block0
    cos, sin = _rope_tables(L)
    kc = cache_k.reshape(cache_k.shape[0], cache_k.shape[1], lc, KV_WIDTH)
    vc = cache_v.reshape(cache_v.shape[0], cache_v.shape[1], lc, KV_WIDTH)
    kern = functools.partial(_lat_attn_kernel, L=L, lc=lc)
    tab = lambda b, g: (0, 0)
    return pl.pallas_call(
        kern,
        grid=(nseq, N_KV_HEADS),
        in_specs=[
            pl.BlockSpec(memory_space=pltpu.SMEM),
            pl.BlockSpec((L, gw), lambda b, g: (rb + b, COL_Q // gw + g)),
            pl.BlockSpec((L, HEAD_DIM), lambda b, g: (rb + b, COL_K // HEAD_DIM + g)),
            pl.BlockSpec((L, HEAD_DIM), lambda b, g: (rb + b, COL_V // HEAD_DIM + g)),
            pl.BlockSpec((L, gw), lambda b, g: (rb + b, COL_ZA // gw + g)),
            pl.BlockSpec((None, None, lc, HEAD_DIM), lambda b, g: (b, layer, 0, g)),
            pl.BlockSpec((None, None, lc, HEAD_DIM), lambda b, g: (b, layer, 0, g)),
            pl.BlockSpec((L, HEAD_DIM), tab),
            pl.BlockSpec((L, HEAD_DIM), tab),
            pl.BlockSpec(memory_space=pl.ANY),
        ],
        out_specs=pl.BlockSpec((L, gw), lambda b, g: (rb + b, g)),
        out_shape=jax.ShapeDtypeStruct((m, ATTN_WIDTH), BF16),
        scratch_shapes=[pltpu.VMEM((lc, HEAD_DIM), BF16), pltpu.VMEM((lc, HEAD_DIM), BF16),
                        pltpu.VMEM((L + 2 * CHUNK, HEAD_DIM), BF16),
                        pltpu.VMEM((L + 2 * CHUNK, HEAD_DIM), BF16)],
        input_output_aliases={9: 0},
        compiler_params=_cparams(("arbitrary", "arbitrary")),
        name="attn_latent",
    )(sink, act, act, act, act, kc, vc, cos, sin, o_prev)


def _branch_kernel(oa_ref, ys_ref, wpa_ref, wps_ref, ga_ref, gs_ref, o_ref):
    a = _dot(oa_ref[...], wpa_ref[...])
    s = _dot(ys_ref[...], wps_ref[...])
    merged = _sigmoid(ga_ref[...].astype(F32)) * a + _sigmoid(gs_ref[...].astype(F32)) * s
    o_ref[...] = merged.astype(BF16)


def _branches(oa, ys, w_pa, w_ps, act, *, layer, tm):
    m = oa.shape[0]
    tn = 512
    l = layer
    return pl.pallas_call(
        _branch_kernel,
        grid=(m // tm, D_MODEL // tn),
        in_specs=[
            pl.BlockSpec((tm, ATTN_WIDTH), lambda i, j: (i, 0)),
            pl.BlockSpec((tm, SSD_WIDTH), lambda i, j: (i, 0)),
            pl.BlockSpec((None, ATTN_WIDTH, tn), lambda i, j: (l, 0, j)),
            pl.BlockSpec((None, SSD_WIDTH, tn), lambda i, j: (l, 0, j)),
            pl.BlockSpec((tm, tn), lambda i, j: (i, COL_GA // tn + j)),
            pl.BlockSpec((tm, tn), lambda i, j: (i, COL_GS // tn + j)),
        ],
        out_specs=pl.BlockSpec((tm, tn), lambda i, j: (i, j)),
        out_shape=jax.ShapeDtypeStruct((m, D_MODEL), BF16),
        compiler_params=_cparams(("arbitrary", "arbitrary")),
        name="branches",
    )(oa, ys, w_pa, w_ps, act, act)


def _out_kernel(*refs, final, na):
    if final:
        mg_ref, w_ref, xa_ref, xb_ref, gate_ref, fg_ref, ya_ref, yb_ref = refs
    else:
        mg_ref, w_ref, xa_ref, xb_ref, gate_ref, o_ref = refs
    is_ctx = pl.program_id(0) < na
    x = jnp.where(is_ctx, xa_ref[...], xb_ref[...])
    y = x + gate_ref[...] * _dot(mg_ref[...], w_ref[...])
    if not final:
        o_ref[...] = y
        return
    ms = jnp.mean(y * y, axis=-1, keepdims=True)
    y = y * lax.rsqrt(ms + EPS) * fg_ref[...]

    @pl.when(is_ctx)
    def _():
        ya_ref[...] = y

    @pl.when(jnp.logical_not(is_ctx))
    def _():
        yb_ref[...] = y


def _out_proj(merged, w_out, xa, xb, xb_offset, mod, final_g, *, layer, tm, na, group_of):
    m = merged.shape[0]
    l = layer
    final = final_g is not None
    in_specs = [
        pl.BlockSpec((tm, D_MODEL), lambda i: (i, 0)),
        pl.BlockSpec((None, D_MODEL, D_MODEL), lambda i: (l, 0, 0)),
    ] + _row_split_specs(tm, na, xb_offset) + [
        pl.BlockSpec((None, None, 1, D_MODEL), lambda i: (l, group_of(i, tm), 0, 2)),
    ]
    args = [merged, w_out, xa, xb, mod]
    if final:
        in_specs.append(pl.BlockSpec((1, D_MODEL), lambda i: (0, 0)))
        args.append(final_g.reshape(1, D_MODEL))
        out_specs = [
            pl.BlockSpec((tm, D_MODEL), lambda i: (jnp.minimum(i, na - 1), 0)),
            pl.BlockSpec((tm, D_MODEL), lambda i: (jnp.maximum(i - na, 0), 0)),
        ]
        out_shape = [jax.ShapeDtypeStruct((na * tm, D_MODEL), F32),
                     jax.ShapeDtypeStruct((m - na * tm, D_MODEL), F32)]
    else:
        out_specs = pl.BlockSpec((tm, D_MODEL), lambda i: (i, 0))
        out_shape = jax.ShapeDtypeStruct((m, D_MODEL), F32)
    return pl.pallas_call(
        functools.partial(_out_kernel, final=final, na=na),
        grid=(m // tm,),
        in_specs=in_specs,
        out_specs=out_specs,
        out_shape=out_shape,
        compiler_params=_cparams(("arbitrary",)),
        name="out_proj_final" if final else "out_proj",
    )(*args)


def _dt_permutation():
    perm = np.zeros(DT_WIDTH, np.int32)
    for g in range(N_SSD_GROUPS):
        for d in range(2):
            for r in range(HEADS_PER_GROUP):
                perm[g * 16 + d * HEADS_PER_GROUP + r] = d * N_SSD_HEADS + g * HEADS_PER_GROUP + r
    return perm


def kernel(x_prompt, x_sample, c, cache_k, cache_v, state_ssm_fwd, state_ssm_bwd, c_ctx, norm_g, w_mod, b_mod, w_in, conv_w, conv_b, attn_sink, a_log_fwd, a_log_bwd, dt_bias_fwd, dt_bias_bwd, d_skip, ssd_norm_g, w_pa, w_ps, w_out, final_norm_g):
    bc, lc, _ = x_prompt.shape
    bl, ll, _ = x_sample.shape
    depth = w_in.shape[0]
    n_ctx = bc * lc
    m = n_ctx + bl * ll
    assert n_ctx % ll == 0 and ll % lc == 0 and lc % CHUNK == 0
    assert 1 + bl <= COND_ROWS

    def group_of(i, tm):
        return jnp.maximum(i * tm - n_ctx + ll, 0) // ll

    tm_big = math.gcd(1024, math.gcd(n_ctx, ll))
    tm_small = math.gcd(512, tm_big)

    cond = jnp.zeros((COND_ROWS, D_MODEL), F32).at[0].set(c_ctx).at[1:1 + bl].set(c)
    mod = _modulation(cond, w_mod, b_mod)

    perm = _dt_permutation()
    consts = _ssd_constants()
    wdt = w_in[:, :, W_IN_DT:W_IN_DT + DT_WIDTH][:, :, perm]
    wdtt = jnp.swapaxes(wdt, 1, 2)
    bias = jnp.concatenate([dt_bias_fwd, dt_bias_bwd], axis=1)[:, perm]
    alog = jnp.concatenate([a_log_fwd, a_log_bwd], axis=1)[:, perm]
    norm_g3 = norm_g.reshape(depth, 1, D_MODEL)
    conv_b3 = conv_b.reshape(depth, 1, CONV_WIDTH)
    dsk = jnp.repeat(d_skip, SSD_HEADDIM, axis=1).reshape(depth, 1, SSD_WIDTH)
    ng = ssd_norm_g.reshape(depth, 1, SSD_WIDTH)
    w_pa_bf = w_pa.astype(BF16)
    w_ps_bf = w_ps.astype(BF16)
    w_out_bf = w_out.astype(BF16)
    s0f = state_ssm_fwd.reshape(bl, depth, SSD_WIDTH, D_STATE)
    s0b = state_ssm_bwd.reshape(bl, depth, SSD_WIDTH, D_STATE)

    na_small = n_ctx // tm_small
    na_big = n_ctx // tm_big
    xa, xb, xb_off = x_prompt.reshape(n_ctx, D_MODEL), x_sample.reshape(bl * ll, D_MODEL), 0
    k_new = v_new = states = None
    for l in range(depth):
        h, p1, p2, rowp = _prep(xa, xb, xb_off, m, mod, norm_g3, wdtt, bias, alog, consts,
                                layer=l, tm=tm_small, na=na_small, group_of=group_of)
        act, k_new, v_new = _inproj(h, w_in, k_new, v_new, layer=l, tm=tm_big, na=na_big, lc=lc,
                                    bc=bc)

        sink = attn_sink[l]
        oa = _ctx_attention(act, sink, L=lc, nseq=bc)
        oa = _lat_attention(act, sink, cache_k, cache_v, oa, L=ll, nseq=bl,
                            row_block0=n_ctx // ll, layer=l)

        ys, sf, sb = _ssd(act, p1, p2, rowp, consts, conv_w, conv_b3, dsk, ng, layer=l, L=lc,
                          nseq=bc, row_block0=0, state_prev=states)
        states = (sf, sb)
        (ys,) = _ssd(act, p1, p2, rowp, consts, conv_w, conv_b3, dsk, ng, layer=l, L=ll, nseq=bl,
                     row_block0=n_ctx // ll, s0f=s0f, s0b=s0b, y_prev=ys)

        merged = _branches(oa, ys, w_pa_bf, w_ps_bf, act, layer=l, tm=tm_big)
        last = l == depth - 1
        res = _out_proj(merged, w_out_bf, xa, xb, xb_off, mod, final_norm_g if last else None,
                        layer=l, tm=tm_small, na=na_small, group_of=group_of)
        if not last:
            xa, xb, xb_off = res, res, na_small

    y_prompt = res[0].reshape(bc, lc, D_MODEL)
    y_sample = res[1].reshape(bl, ll, D_MODEL)
    shape_kv = (bc, depth, lc, N_KV_HEADS, HEAD_DIM)
    shape_st = (bc, depth, N_SSD_HEADS, SSD_HEADDIM, D_STATE)
    return (y_prompt, y_sample, k_new.reshape(shape_kv), v_new.reshape(shape_kv),
            states[0].reshape(shape_st), states[1].reshape(shape_st))
```

```python
import functools
import math

import numpy as np
import jax
import jax.numpy as jnp
from jax import lax
from jax.experimental import pallas as pl
from jax.experimental.pallas import tpu as pltpu

F32 = jnp.float32
BF16 = jnp.bfloat16

D_MODEL = 2048
HEAD_DIM = 128
N_Q_HEADS = 16
N_KV_HEADS = 4
Q_PER_KV = 4
ATTN_WIDTH = 2048
KV_WIDTH = 512
WINDOW = 128
GRID_W = 64
ROPE_BASE = 10000.0
SSD_WIDTH = 4096
SSD_HEADDIM = 64
N_SSD_HEADS = 64
D_STATE = 128
N_SSD_GROUPS = 8
HEADS_PER_GROUP = 8
GROUP_WIDTH = SSD_WIDTH // N_SSD_GROUPS
CHUNK = 128
D_CONV = 5
BC_WIDTH = 1024
CONV_WIDTH = 6144
EPS = 1e-6
MOD_WIDTH = 3 * D_MODEL
LOG2_E = math.log2(math.e)

COL_Q = 0
COL_K = 2048
COL_V = 2560
COL_ZA = 3072
COL_XBC = 5120
COL_ZS = 11264
COL_GA = 15360
COL_GS = 17408
ACT_WIDTH = 19456
W_IN_DT = 15360
DT_WIDTH = 2 * N_SSD_HEADS

LANES = 128
SUBLANES = 8
VMEM_LIMIT = 56 * 1024 * 1024

COND_ROWS = 8


def _cparams(sem):
    return pltpu.CompilerParams(dimension_semantics=sem, vmem_limit_bytes=VMEM_LIMIT)


def _dot(a, b):
    return jnp.dot(a, b, preferred_element_type=F32)


def _dot_nt(a, b):
    return lax.dot_general(a, b, (((1,), (1,)), ((), ())), preferred_element_type=F32)


def _split2(x):
    hi = x.astype(BF16)
    lo = (x - hi.astype(F32)).astype(BF16)
    return hi, lo


def _split3(x):
    p1 = x.astype(BF16)
    r1 = x - p1.astype(F32)
    p2 = r1.astype(BF16)
    p3 = (r1 - p2.astype(F32)).astype(BF16)
    return p1, p2, p3


def _dot3(a, b):
    ah, al = _split2(a)
    bh, bl = _split2(b)
    return _dot(ah, bh) + _dot(al, bh) + _dot(ah, bl)


def _sigmoid(x):
    return 0.5 + 0.5 * jnp.tanh(0.5 * x)


def _silu(x):
    half = 0.5 * x
    return half + half * jnp.tanh(half)


def _softplus(x):
    return jnp.maximum(x, 0.0) + jnp.log1p(jnp.exp(-jnp.abs(x)))


def _mod_kernel(cond_ref, w_ref, b_ref, o_ref):
    res = _dot3(_silu(cond_ref[...]), w_ref[...]) + b_ref[...]
    for r in range(COND_ROWS):
        o_ref[r] = res[r:r + 1, :]


def _modulation(cond, w_mod, b_mod):
    depth = w_mod.shape[0]
    tn = 512
    return pl.pallas_call(
        _mod_kernel,
        grid=(depth, MOD_WIDTH // tn),
        in_specs=[
            pl.BlockSpec((COND_ROWS, D_MODEL), lambda l, j: (0, 0)),
            pl.BlockSpec((None, D_MODEL, tn), lambda l, j: (l, 0, j)),
            pl.BlockSpec((None, 1, tn), lambda l, j: (l, 0, j)),
        ],
        out_specs=pl.BlockSpec((None, COND_ROWS, 1, tn), lambda l, j: (l, 0, 0, j)),
        out_shape=jax.ShapeDtypeStruct((depth, COND_ROWS, 1, MOD_WIDTH), F32),
        compiler_params=_cparams(("arbitrary", "arbitrary")),
        name="modulation",
    )(cond, w_mod, b_mod.reshape(depth, 1, MOD_WIDTH))


def _prep_kernel(xa_ref, xb_ref, shift_ref, scale_ref, g_ref, wdtt_ref, biast_ref, alogt_ref,
                 sel1_ref, sel2_ref, h_ref, p1_ref, p2_ref, row_ref, *, tm, na):
    x = jnp.where(pl.program_id(0) < na, xa_ref[...], xb_ref[...])
    ms = jnp.mean(x * x, axis=-1, keepdims=True)
    h = (x * lax.rsqrt(ms + EPS) * g_ref[...]) * (1.0 + scale_ref[...]) + shift_ref[...]
    h_ref[...] = h.astype(BF16)

    hh, hl = _split2(h)
    wth, wtl = _split2(wdtt_ref[...])
    rawt = _dot_nt(wth, hh) + _dot_nt(wth, hl) + _dot_nt(wtl, hh)
    dtt = _softplus(rawt + biast_ref[...])
    dtat = dtt * (-jnp.exp(alogt_ref[...]))

    ii = lax.broadcasted_iota(jnp.int32, (CHUNK, CHUNK), 0)
    kk = lax.broadcasted_iota(jnp.int32, (CHUNK, CHUNK), 1)
    lt = jnp.where(kk <= ii, 1.0, 0.0).astype(BF16)
    ut = jnp.where(kk >= ii, 1.0, 0.0).astype(BF16)
    fwd_row = (ii % 16) < HEADS_PER_GROUP

    for c in range(tm // CHUNK):
        rows = slice(c * CHUNK, (c + 1) * CHUNK)
        dt_t = dtt[:, rows]
        q1, q2, q3 = _split3(dtat[:, rows])
        pre_t = _dot(q1, ut) + _dot(q2, ut) + _dot(q3, ut)
        suf_t = _dot(q1, lt) + _dot(q2, lt) + _dot(q3, lt)
        acs_t = jnp.where(fwd_row, pre_t, suf_t)
        edge_t = jnp.where(fwd_row, acs_t[:, CHUNK - 1:CHUNK], acs_t[:, 0:1])
        w1_t = dt_t * jnp.exp(edge_t - acs_t)
        row_ref[c, 0] = (acs_t - jnp.log(dt_t)) * LOG2_E
        row_ref[c, 1] = dt_t
        acs = acs_t.T
        p1 = _dot(jnp.concatenate(_split3(acs * LOG2_E), axis=1), sel1_ref[...])
        p1_ref[rows, :] = p1.astype(BF16)
        p2 = _dot(jnp.concatenate(_split3(jnp.exp(acs)) + _split3(w1_t.T), axis=1), sel2_ref[...])
        p2_ref[rows, :] = p2.astype(BF16)


def _row_split_specs(tm, na, xb_offset):
    return [
        pl.BlockSpec((tm, D_MODEL), lambda i: (jnp.minimum(i, na - 1), 0)),
        pl.BlockSpec((tm, D_MODEL), lambda i: (xb_offset + jnp.maximum(i - na, 0), 0)),
    ]


def _prep(xa, xb, xb_offset, m, mod, norm_g, wdtt, bias, alog, consts, *, layer, tm, na, group_of):
    l = layer
    kern = functools.partial(_prep_kernel, tm=tm, na=na)
    par2 = lambda i: (l, 0, 0)
    whole = lambda i: (0, 0)
    pw = N_SSD_GROUPS * LANES
    return pl.pallas_call(
        kern,
        grid=(m // tm,),
        in_specs=_row_split_specs(tm, na, xb_offset) + [
            pl.BlockSpec((None, None, 1, D_MODEL), lambda i: (l, group_of(i, tm), 0, 0)),
            pl.BlockSpec((None, None, 1, D_MODEL), lambda i: (l, group_of(i, tm), 0, 1)),
            pl.BlockSpec((None, 1, D_MODEL), par2),
            pl.BlockSpec((None, DT_WIDTH, D_MODEL), par2),
            pl.BlockSpec((None, DT_WIDTH, 1), par2),
            pl.BlockSpec((None, DT_WIDTH, 1), par2),
            pl.BlockSpec((N_PIECES * LANES, pw), whole),
            pl.BlockSpec((2 * N_PIECES * LANES, pw), whole),
        ],
        out_specs=[
            pl.BlockSpec((tm, D_MODEL), lambda i: (i, 0)),
            pl.BlockSpec((tm, pw), lambda i: (i, 0)),
            pl.BlockSpec((tm, pw), lambda i: (i, 0)),
            pl.BlockSpec((tm // CHUNK, 2, DT_WIDTH, CHUNK), lambda i: (i, 0, 0, 0)),
        ],
        out_shape=[
            jax.ShapeDtypeStruct((m, D_MODEL), BF16),
            jax.ShapeDtypeStruct((m, pw), BF16),
            jax.ShapeDtypeStruct((m, pw), BF16),
            jax.ShapeDtypeStruct((m // CHUNK, 2, DT_WIDTH, CHUNK), F32),
        ],
        compiler_params=_cparams(("arbitrary",)),
        name="prep",
    )(xa, xb, mod, mod, norm_g, wdtt, bias[:, :, None], alog[:, :, None], consts["sel1"],
      consts["sel2"])


N_PIECES = 3
N_HD = 2 * HEADS_PER_GROUP
PIECE_LANES = N_PIECES * N_HD
N_EXPAND = 4


def _ssd_constants():
    pw = N_SSD_GROUPS * LANES
    sel1 = np.zeros((N_PIECES * LANES, pw), np.float32)
    sel2 = np.zeros((2 * N_PIECES * LANES, pw), np.float32)
    for g in range(N_SSD_GROUPS):
        for hd in range(N_HD):
            lam = g * N_HD + hd
            for p in range(N_PIECES):
                sel1[p * LANES + lam, g * LANES + N_PIECES * hd + p] = 1.0
                for q in range(2):
                    sel2[(q * N_PIECES + p) * LANES + lam,
                         g * LANES + q * PIECE_LANES + N_PIECES * hd + p] = 1.0
    cbc = np.zeros((LANES, N_HD * LANES), np.float32)
    for hd in range(N_HD):
        cbc[N_PIECES * hd:N_PIECES * (hd + 1), hd * LANES:(hd + 1) * LANES] = 1.0
    eexp = np.zeros((N_EXPAND, LANES, GROUP_WIDTH), np.float32)
    for e, (q, d) in enumerate(((0, 0), (1, 0), (0, 1), (1, 1))):
        for r in range(HEADS_PER_GROUP):
            row0 = q * PIECE_LANES + N_PIECES * (d * HEADS_PER_GROUP + r)
            eexp[e, row0:row0 + N_PIECES, r * SSD_HEADDIM:(r + 1) * SSD_HEADDIM] = 1.0
    return {k: jnp.asarray(v, BF16) for k, v in
            (("sel1", sel1), ("sel2", sel2), ("cbc", cbc), ("eexp", eexp))}


INPROJ_TN = 1024
KV_BLOCK = COL_K // INPROJ_TN


def _inproj_kernel(*refs, na, lc, aliased):
    if aliased:
        h_ref, w_ref, _kprev, _vprev, act_ref, k_ref, v_ref, wbf_s = refs
    else:
        h_ref, w_ref, act_ref, k_ref, v_ref, wbf_s = refs
    j = pl.program_id(0)
    i = pl.program_id(1)

    @pl.when(i == 0)
    def _():
        wbf_s[...] = w_ref[...].astype(BF16)

    acc = _dot(h_ref[...], wbf_s[...])
    act_ref[...] = acc.astype(BF16)

    @pl.when((j == KV_BLOCK) & (i < na))
    def _():
        for s in range(k_ref.shape[0]):
            for dst, col0 in ((k_ref, 0), (v_ref, KV_WIDTH)):
                for hh in range(N_KV_HEADS):
                    cols = slice(col0 + hh * HEAD_DIM, col0 + (hh + 1) * HEAD_DIM)
                    dst[s, pl.ds(hh, lc, stride=N_KV_HEADS), :] = acc[s * lc:(s + 1) * lc, cols]


def _inproj(h, w_in, k_prev, v_prev, *, layer, tm, na, lc, bc):
    m = h.shape[0]
    depth = w_in.shape[0]
    tn = INPROJ_TN
    l = layer
    spb = tm // lc
    aliased = k_prev is not None

    def w_col(j):
        skip = jnp.where(j >= W_IN_DT // tn, DT_WIDTH // LANES, 0)
        return (j * (tn // LANES) + skip) * LANES

    def kv_idx(j, i):
        return jnp.where(j < KV_BLOCK, 0, jnp.where(j == KV_BLOCK, jnp.minimum(i, na - 1), na - 1))

    in_specs = [
        pl.BlockSpec((tm, D_MODEL), lambda j, i: (i, 0)),
        pl.BlockSpec((None, pl.Element(D_MODEL), pl.Element(tn)), lambda j, i: (l, 0, w_col(j))),
    ]
    args = [h, w_in]
    aliases = {}
    if aliased:
        in_specs += [pl.BlockSpec(memory_space=pl.ANY)] * 2
        args += [k_prev, v_prev]
        aliases = {2: 1, 3: 2}
    kv_rows = lc * N_KV_HEADS
    kv_spec = pl.BlockSpec((spb, None, kv_rows, HEAD_DIM), lambda j, i: (kv_idx(j, i), l, 0, 0))
    return pl.pallas_call(
        functools.partial(_inproj_kernel, na=na, lc=lc, aliased=aliased),
        grid=(ACT_WIDTH // tn, m // tm),
        in_specs=in_specs,
        out_specs=[pl.BlockSpec((tm, tn), lambda j, i: (i, j)), kv_spec, kv_spec],
        out_shape=[
            jax.ShapeDtypeStruct((m, ACT_WIDTH), BF16),
            jax.ShapeDtypeStruct((bc, depth, kv_rows, HEAD_DIM), F32),
            jax.ShapeDtypeStruct((bc, depth, kv_rows, HEAD_DIM), F32),
        ],
        scratch_shapes=[pltpu.VMEM((D_MODEL, tn), BF16)],
        input_output_aliases=aliases,
        compiler_params=_cparams(("arbitrary", "arbitrary")),
        name="inproj",
    )(*args)


PAD = SUBLANES
CHUNK_UNROLL = 4
SSD_SCRATCH_BUDGET = 28 * 1024 * 1024


def _ssd_kernel(*refs, L, per_step, has_init, emit_state, n_alias):
    refs = list(refs)
    seq_in = refs[:7]
    shared = refs[7:17]
    pos = 17
    state_in = []
    if has_init:
        state_in = refs[pos:pos + 2]
        pos += 2
    pos += n_alias
    y_ref = refs[pos]
    pos += 1
    state_out = []
    if emit_state:
        state_out = refs[pos:pos + 2]
        pos += 2
    scratch = refs[pos:]
    nc = L // CHUNK
    for sq in range(per_step):
        rows = pl.ds(sq * L, L)
        views = [r.at[rows] for r in seq_in[:6]] + [seq_in[6].at[pl.ds(sq * nc, nc)]]
        _ssd_sequence(views, shared, [r.at[pl.ds(sq, 1)] for r in state_in], y_ref.at[rows],
                      [r.at[pl.ds(sq, 1)] for r in state_out], [s.at[sq] for s in scratch], L=L)


def _ssd_sequence(seq_in, shared, state_in, y_ref, state_out, scratch, *, L):
    x_ref, b_ref, c_ref, z_ref, p1_ref, p2_ref, row_ref = seq_in
    (cbc_ref, eexp_ref, cwx_ref, cwb_ref, cwc_ref, cbx_ref, cbb_ref, cbias_c_ref, dsk_ref,
     ng_ref) = shared
    has_init = bool(state_in)
    emit_state = bool(state_out)
    if has_init:
        s0f_ref, s0b_ref = state_in
    if emit_state:
        sf_ref, sb_ref = state_out
    pad_s, xc_s, bc_s, cc_s, bt_s, cum_s, exp_s, yacc_s, sft_s, sbt_s = scratch
    nc = L // CHUNK

    tile = 2 * LANES
    conv_srcs = ((x_ref, cwx_ref, cbx_ref, xc_s, 0, GROUP_WIDTH),
                 (b_ref, cwb_ref, cbb_ref, bc_s, GROUP_WIDTH, D_STATE),
                 (c_ref, cwc_ref, cbias_c_ref, cc_s, GROUP_WIDTH + D_STATE, D_STATE))
    zeros = jnp.zeros((PAD, pad_s.shape[1]), F32)
    pad_s[0:PAD, :] = zeros
    pad_s[L + PAD:L + 2 * PAD, :] = zeros
    for src_ref, _, _, _, off, width in conv_srcs:
        for c in range(nc):
            pad_s[PAD + c * CHUNK:PAD + (c + 1) * CHUNK, off:off + width] = (
                src_ref[c * CHUNK:(c + 1) * CHUNK, :].astype(F32))

    def conv_slab(w_ref, bias_ref, dst_s, off, c, s, zero_row):
        ls = slice(s * LANES, (s + 1) * LANES)
        ps = slice(off + s * LANES, off + (s + 1) * LANES)
        halo = pad_s[c * CHUNK:(c + 1) * CHUNK + 2 * PAD, ps]
        centre = D_CONV // 2
        acc = jnp.broadcast_to(bias_ref[:, ls] + zero_row, (CHUNK, LANES))
        for k in range(D_CONV):
            shifted = halo if k == centre else pltpu.roll(halo, (centre - k) % halo.shape[0], 0)
            acc = acc + w_ref[k:k + 1, ls] * shifted[PAD:PAD + CHUNK, :]
        dst_s[c * CHUNK:(c + 1) * CHUNK, ls] = _silu(acc)

    conv_items = [functools.partial(conv_slab, w_ref, bias_ref, dst_s, off, c, s)
                  for _, w_ref, bias_ref, dst_s, off, width in conv_srcs
                  for c in range(nc) for s in range(width // LANES)]


    def zero_row_of(res):
        bits = pltpu.bitcast(res[0:SUBLANES, 0:LANES], jnp.uint32)
        return pltpu.bitcast((bits >> 16) >> 16, F32)[0:1, :]

    def cum_tile(t):
        ls = slice(t * tile, (t + 1) * tile)
        res = _dot(p1_ref[...], cbc_ref[:, ls])
        cum_s[:, ls] = res
        return zero_row_of(res)

    def exp_tile(e, t):
        ls = slice(t * tile, (t + 1) * tile)
        res = _dot(p2_ref[...], eexp_ref[e, :, ls])
        exp_s[e, :, ls] = res
        return zero_row_of(res)

    spread_items = ([functools.partial(cum_tile, t) for t in range(N_HD * LANES // tile)]
                    + [functools.partial(exp_tile, e, t) for e in range(N_EXPAND)
                       for t in range(GROUP_WIDTH // tile)])

    merged = sorted([((i + 0.5) / len(conv_items), 0, f) for i, f in enumerate(conv_items)]
                    + [((i + 0.5) / len(spread_items), 1, f) for i, f in enumerate(spread_items)],
                    key=lambda item: item[:2])
    zero_row = jnp.zeros((1, LANES), F32)
    for _, is_spread, emit in merged:
        if is_spread:
            zero_row = emit()
        else:
            emit(zero_row)

    if has_init:
        sft_s[...] = s0f_ref[0].T
        sbt_s[...] = s0b_ref[0].T
    else:
        sft_s[...] = jnp.zeros_like(sft_s)
        sbt_s[...] = jnp.zeros_like(sbt_s)

    ii = lax.broadcasted_iota(jnp.int32, (CHUNK, CHUNK), 0)
    jj = lax.broadcasted_iota(jnp.int32, (CHUNK, CHUNK), 1)
    lower = jj <= ii
    diag = jj == ii
    left = jj < SSD_HEADDIM

    def fwd_chunk(c, carry):
        r0 = pl.multiple_of(c * CHUNK, CHUNK)
        rows = pl.ds(r0, CHUNK)
        xq = xc_s[rows, :]
        bq = bc_s[rows, :]
        cq = cc_s[rows, :].astype(BF16)
        rowa = row_ref[c, 0]
        rowd = row_ref[c, 1]
        cb = _dot_nt(cq, bq.astype(BF16))
        y_off = _dot(cq, sft_s[...].astype(BF16)) * exp_s[0, rows, :]
        y_parts = []
        for k in range(HEADS_PER_GROUP // 2):
            ms = []
            for r in (2 * k, 2 * k + 1):
                rb = HEADS_PER_GROUP + r
                seg_f = cum_s[rows, r * LANES:(r + 1) * LANES] - rowa[r:r + 1, :]
                seg_b = cum_s[rows, rb * LANES:(rb + 1) * LANES] - rowa[rb:rb + 1, :]
                dm = jnp.exp2(jnp.where(lower, seg_f, seg_b))
                dm = dm + jnp.where(diag, rowd[rb:rb + 1, :], 0.0)
                ms.append((cb * dm).astype(BF16))
            lhs = jnp.concatenate(ms, axis=1)
            xp = xq[:, k * LANES:(k + 1) * LANES]
            rhs = jnp.concatenate([jnp.where(left, xp, 0.0), jnp.where(left, 0.0, xp)],
                                  axis=0).astype(BF16)
            y_parts.append(_dot(lhs, rhs))
        yacc_s[rows, :] = jnp.concatenate(y_parts, axis=1) + y_off
        decay = exp_s[0, pl.ds(r0 + CHUNK - 1, 1), :]
        bt = bq.T.astype(BF16)
        bt_s[c] = bt
        sft_s[...] = sft_s[...] * decay + _dot(bt, (xq * exp_s[1, rows, :]).astype(BF16))
        return carry

    lax.fori_loop(0, nc, fwd_chunk, 0, unroll=CHUNK_UNROLL)

    def bwd_chunk(t, carry):
        c = nc - 1 - t
        r0 = pl.multiple_of(c * CHUNK, CHUNK)
        rows = pl.ds(r0, CHUNK)
        xq = xc_s[rows, :]
        cq = cc_s[rows, :].astype(BF16)
        y = (yacc_s[rows, :] + _dot(cq, sbt_s[...].astype(BF16)) * exp_s[2, rows, :]
             + dsk_ref[...] * xq)
        y = y * _silu(z_ref[rows, :].astype(F32))
        ms = jnp.mean(y * y, axis=-1, keepdims=True)
        y_ref[rows, :] = (y * lax.rsqrt(ms + EPS) * ng_ref[...]).astype(BF16)
        decay = exp_s[2, pl.ds(r0, 1), :]
        sbt_s[...] = sbt_s[...] * decay + _dot(bt_s[c], (xq * exp_s[3, rows, :]).astype(BF16))
        return carry

    lax.fori_loop(0, nc, bwd_chunk, 0, unroll=CHUNK_UNROLL)

    if emit_state:
        sf_ref[0] = sft_s[...].T
        sb_ref[0] = sbt_s[...].T


def _ssd(act, p1, p2, rowp, consts, conv_w, conv_b, dsk, ng, *, layer, L, nseq, row_block0,
         s0f=None, s0b=None, y_prev=None, state_prev=None):
    m = act.shape[0]
    depth = conv_w.shape[0]
    l = layer
    has_init = s0f is not None
    emit_state = not has_init
    nc = L // CHUNK
    gw = GROUP_WIDTH
    rb = row_block0
    off_b = SSD_WIDTH // D_STATE
    off_c = (SSD_WIDTH + BC_WIDTH) // D_STATE
    seq_scratch = [
        ((L + 2 * PAD, gw + 2 * D_STATE), F32),
        ((L, gw), F32),
        ((L, D_STATE), F32),
        ((L, D_STATE), F32),
        ((nc, D_STATE, CHUNK), BF16),
        ((L, N_HD * LANES), F32),
        ((N_EXPAND, L, gw), F32),
        ((L, gw), F32),
        ((D_STATE, gw), F32),
        ((D_STATE, gw), F32),
    ]
    seq_bytes = sum(math.prod(shape) * jnp.dtype(dt).itemsize for shape, dt in seq_scratch)
    per_step = max(n for n in (1, 2, 4)
                   if nseq % n == 0 and row_block0 % n == 0 and n * seq_bytes <= SSD_SCRATCH_BUDGET)
    rb = row_block0 // per_step
    ls = per_step * L
    in_specs = [
        pl.BlockSpec((ls, gw), lambda b, g: (rb + b, COL_XBC // gw + g)),
        pl.BlockSpec((ls, D_STATE), lambda b, g: (rb + b, COL_XBC // D_STATE + off_b + g)),
        pl.BlockSpec((ls, D_STATE), lambda b, g: (rb + b, COL_XBC // D_STATE + off_c + g)),
        pl.BlockSpec((ls, gw), lambda b, g: (rb + b, COL_ZS // gw + g)),
        pl.BlockSpec((ls, LANES), lambda b, g: (rb + b, g)),
        pl.BlockSpec((ls, LANES), lambda b, g: (rb + b, g)),
        pl.BlockSpec((per_step * nc, 2, N_HD, CHUNK), lambda b, g: (rb + b, 0, g, 0)),
        pl.BlockSpec((LANES, N_HD * LANES), lambda b, g: (0, 0)),
        pl.BlockSpec((N_EXPAND, LANES, gw), lambda b, g: (0, 0, 0)),
        pl.BlockSpec((None, D_CONV, gw), lambda b, g: (l, 0, g)),
        pl.BlockSpec((None, D_CONV, D_STATE), lambda b, g: (l, 0, off_b + g)),
        pl.BlockSpec((None, D_CONV, D_STATE), lambda b, g: (l, 0, off_c + g)),
        pl.BlockSpec((None, 1, gw), lambda b, g: (l, 0, g)),
        pl.BlockSpec((None, 1, D_STATE), lambda b, g: (l, 0, off_b + g)),
        pl.BlockSpec((None, 1, D_STATE), lambda b, g: (l, 0, off_c + g)),
        pl.BlockSpec((None, 1, gw), lambda b, g: (l, 0, g)),
        pl.BlockSpec((None, 1, gw), lambda b, g: (l, 0, g)),
    ]
    args = [act, act, act, act, p1, p2, rowp, consts["cbc"], consts["eexp"], conv_w, conv_w, conv_w,
            conv_b, conv_b, conv_b, dsk, ng]
    aliases = {}
    n_alias = 0
    state_spec = pl.BlockSpec((per_step, None, gw, D_STATE), lambda b, g: (b, l, g, 0))
    if has_init:
        in_specs += [state_spec, state_spec, pl.BlockSpec(memory_space=pl.ANY)]
        args += [s0f, s0b, y_prev]
        aliases = {len(args) - 1: 0}
        n_alias = 1
    elif state_prev is not None:
        in_specs += [pl.BlockSpec(memory_space=pl.ANY)] * 2
        args += list(state_prev)
        aliases = {len(args) - 2: 1, len(args) - 1: 2}
        n_alias = 2
    out_specs = [pl.BlockSpec((ls, gw), lambda b, g: (rb + b, g))]
    out_shape = [jax.ShapeDtypeStruct((m, SSD_WIDTH), BF16)]
    if emit_state:
        out_specs += [state_spec] * 2
        out_shape += [jax.ShapeDtypeStruct((nseq, depth, SSD_WIDTH, D_STATE), F32)] * 2
    scratch = [pltpu.VMEM((per_step,) + shape, dt) for shape, dt in seq_scratch]
    kern = functools.partial(_ssd_kernel, L=L, per_step=per_step, has_init=has_init,
                             emit_state=emit_state, n_alias=n_alias)
    return pl.pallas_call(
        kern,
        grid=(nseq // per_step, N_SSD_GROUPS),
        in_specs=in_specs,
        out_specs=out_specs,
        out_shape=out_shape,
        scratch_shapes=scratch,
        input_output_aliases=aliases,
        compiler_params=_cparams(("arbitrary", "arbitrary")),
        name="ssd_latent" if has_init else "ssd_context",
    )(*args)


ATTN_SCALE = HEAD_DIM ** -0.5


def _ctx_attn_kernel(sink_ref, q_ref, k_ref, v_ref, z_ref, o_ref, *, L):
    g = pl.program_id(1)
    for sq in range(q_ref.shape[0] // L):
        rows = slice(sq * L, (sq + 1) * L)
        k = k_ref[rows, :]
        v = v_ref[rows, :]
        for r in range(Q_PER_KV):
            ls = slice(r * HEAD_DIM, (r + 1) * HEAD_DIM)
            sink = sink_ref[g * Q_PER_KV + r]
            s = _dot_nt(q_ref[rows, ls], k) * ATTN_SCALE
            m = jnp.maximum(jnp.max(s, axis=-1, keepdims=True), sink)
            p = jnp.exp(s - m)
            denom = jnp.sum(p, axis=-1, keepdims=True) + jnp.exp(sink - m)
            o = _dot(p.astype(BF16), v) / denom
            o_ref[rows, ls] = (o * _silu(z_ref[rows, ls].astype(F32))).astype(BF16)


def _ctx_attention(act, sink, *, L, nseq):
    m = act.shape[0]
    gw = Q_PER_KV * HEAD_DIM
    per_step = math.gcd(nseq, 8)
    rows = per_step * L
    return pl.pallas_call(
        functools.partial(_ctx_attn_kernel, L=L),
        grid=(nseq // per_step, N_KV_HEADS),
        in_specs=[
            pl.BlockSpec(memory_space=pltpu.SMEM),
            pl.BlockSpec((rows, gw), lambda b, g: (b, COL_Q // gw + g)),
            pl.BlockSpec((rows, HEAD_DIM), lambda b, g: (b, COL_K // HEAD_DIM + g)),
            pl.BlockSpec((rows, HEAD_DIM), lambda b, g: (b, COL_V // HEAD_DIM + g)),
            pl.BlockSpec((rows, gw), lambda b, g: (b, COL_ZA // gw + g)),
        ],
        out_specs=pl.BlockSpec((rows, gw), lambda b, g: (b, g)),
        out_shape=jax.ShapeDtypeStruct((m, ATTN_WIDTH), BF16),
        compiler_params=_cparams(("arbitrary", "arbitrary")),
        name="attn_context",
    )(sink, act, act, act, act)


def _rope_tables(length):
    sec = HEAD_DIM // 2
    half = sec // 2
    d = np.arange(HEAD_DIM)
    e = d % sec
    freqs = ROPE_BASE ** (-(e % half).astype(np.float64) / half)
    t = np.arange(length)
    pos = np.where((d // sec)[None, :] == 0, (t // GRID_W)[:, None], (t % GRID_W)[:, None])
    ang = pos.astype(np.float64) * freqs[None, :]
    sign = np.where(e < half, -1.0, 1.0)[None, :]
    return (jnp.asarray(np.cos(ang), F32), jnp.asarray(np.sin(ang) * sign, F32))


def _rope(x, cos, sin_signed, first_half):
    partner = jnp.where(first_half, pltpu.roll(x, LANES - HEAD_DIM // 4, 1),
                        pltpu.roll(x, HEAD_DIM // 4, 1))
    return x * cos + partner * sin_signed


def _lat_attn_kernel(sink_ref, q_ref, k_ref, v_ref, z_ref, kc_ref, vc_ref, cos_ref, sin_ref,
                     _oprev_ref, o_ref, kctx_s, vctx_s, keys_s, vals_s, *, L, lc):
    g = pl.program_id(1)
    nb = L // CHUNK
    win = 3 * CHUNK
    rows4 = Q_PER_KV * CHUNK
    lane = lax.broadcasted_iota(jnp.int32, (CHUNK, HEAD_DIM), 1)
    first_half = (lane % (HEAD_DIM // 2)) < (HEAD_DIM // 4)

    kctx_s[...] = kc_ref[...].astype(BF16)
    vctx_s[...] = vc_ref[...].astype(BF16)
    zero_blk = jnp.zeros((CHUNK, HEAD_DIM), BF16)
    for dst in (keys_s, vals_s):
        dst[0:CHUNK, :] = zero_blk
        dst[CHUNK + L:2 * CHUNK + L, :] = zero_blk
    for n in range(nb):
        rows = slice(n * CHUNK, (n + 1) * CHUNK)
        kr = _rope(k_ref[rows, :].astype(F32), cos_ref[rows, :], sin_ref[rows, :], first_half)
        keys_s[CHUNK + n * CHUNK:CHUNK + (n + 1) * CHUNK, :] = kr.astype(BF16)
        vals_s[CHUNK + n * CHUNK:CHUNK + (n + 1) * CHUNK, :] = v_ref[rows, :]

    qi = lax.broadcasted_iota(jnp.int32, (rows4, win), 0) % CHUNK
    wi = lax.broadcasted_iota(jnp.int32, (rows4, win), 1)
    band = jnp.abs(qi - wi + CHUNK) <= WINDOW
    head = lax.broadcasted_iota(jnp.int32, (rows4, 1), 0) // CHUNK
    sink = jnp.zeros((rows4, 1), F32)
    for r in range(Q_PER_KV):
        sink = jnp.where(head == r, sink_ref[g * Q_PER_KV + r], sink)

    def block(n, carry):
        r0 = pl.multiple_of(n * CHUNK, CHUNK)
        rows = pl.ds(r0, CHUNK)
        cos = cos_ref[rows, :]
        sin = sin_ref[rows, :]
        q = jnp.concatenate(
            [_rope(q_ref[rows, r * HEAD_DIM:(r + 1) * HEAD_DIM].astype(F32), cos, sin,
                   first_half).astype(BF16) for r in range(Q_PER_KV)], axis=0)
        in_seq = (wi >= CHUNK - r0) & (wi < L + CHUNK - r0)
        s_ctx = _dot_nt(q, kctx_s[...]) * ATTN_SCALE
        s_lat = _dot_nt(q, keys_s[pl.ds(r0, win), :]) * ATTN_SCALE
        s_lat = jnp.where(in_seq, jnp.where(band, s_lat, -jnp.inf), -jnp.inf)
        m = jnp.maximum(jnp.maximum(jnp.max(s_ctx, axis=-1, keepdims=True),
                                    jnp.max(s_lat, axis=-1, keepdims=True)), sink)
        p_ctx = jnp.exp(s_ctx - m)
        p_lat = jnp.exp(s_lat - m)
        denom = (jnp.sum(p_ctx, axis=-1, keepdims=True) + jnp.sum(p_lat, axis=-1, keepdims=True)
                 + jnp.exp(sink - m))
        o = (_dot(p_ctx.astype(BF16), vctx_s[...])
             + _dot(p_lat.astype(BF16), vals_s[pl.ds(r0, win), :])) / denom
        for r in range(Q_PER_KV):
            ls = slice(r * HEAD_DIM, (r + 1) * HEAD_DIM)
            o_ref[rows, ls] = (o[r * CHUNK:(r + 1) * CHUNK, :]
                               * _silu(z_ref[rows, ls].astype(F32))).astype(BF16)
        return carry

    lax.fori_loop(0, nb, block, 0, unroll=4)


def _lat_attention(act, sink, cache_k, cache_v, o_prev, *, L, nseq, row_block0, layer):
    m = act.shape[0]
    gw = Q_PER_KV * HEAD_DIM
    lc = cache_k.shape[2]
    rb = row_block0
    cos, sin = _rope_tables(L)
    kc = cache_k.reshape(cache_k.shape[0], cache_k.shape[1], lc, KV_WIDTH)
    vc = cache_v.reshape(cache_v.shape[0], cache_v.shape[1], lc, KV_WIDTH)
    kern = functools.partial(_lat_attn_kernel, L=L, lc=lc)
    tab = lambda b, g: (0, 0)
    return pl.pallas_call(
        kern,
        grid=(nseq, N_KV_HEADS),
        in_specs=[
            pl.BlockSpec(memory_space=pltpu.SMEM),
            pl.BlockSpec((L, gw), lambda b, g: (rb + b, COL_Q // gw + g)),
            pl.BlockSpec((L, HEAD_DIM), lambda b, g: (rb + b, COL_K // HEAD_DIM + g)),
            pl.BlockSpec((L, HEAD_DIM), lambda b, g: (rb + b, COL_V // HEAD_DIM + g)),
            pl.BlockSpec((L, gw), lambda b, g: (rb + b, COL_ZA // gw + g)),
            pl.BlockSpec((None, None, lc, HEAD_DIM), lambda b, g: (b, layer, 0, g)),
            pl.BlockSpec((None, None, lc, HEAD_DIM), lambda b, g: (b, layer, 0, g)),
            pl.BlockSpec((L, HEAD_DIM), tab),
            pl.BlockSpec((L, HEAD_DIM), tab),
            pl.BlockSpec(memory_space=pl.ANY),
        ],
        out_specs=pl.BlockSpec((L, gw), lambda b, g: (rb + b, g)),
        out_shape=jax.ShapeDtypeStruct((m, ATTN_WIDTH), BF16),
        scratch_shapes=[pltpu.VMEM((lc, HEAD_DIM), BF16), pltpu.VMEM((lc, HEAD_DIM), BF16),
                        pltpu.VMEM((L + 2 * CHUNK, HEAD_DIM), BF16),
                        pltpu.VMEM((L + 2 * CHUNK, HEAD_DIM), BF16)],
        input_output_aliases={9: 0},
        compiler_params=_cparams(("arbitrary", "arbitrary")),
        name="attn_latent",
    )(sink, act, act, act, act, kc, vc, cos, sin, o_prev)


def _branch_kernel(oa_ref, ys_ref, wpa_ref, wps_ref, ga_ref, gs_ref, o_ref):
    a = _dot(oa_ref[...], wpa_ref[...])
    s = _dot(ys_ref[...], wps_ref[...])
    merged = _sigmoid(ga_ref[...].astype(F32)) * a + _sigmoid(gs_ref[...].astype(F32)) * s
    o_ref[...] = merged.astype(BF16)


def _branches(oa, ys, w_pa, w_ps, act, *, layer, tm):
    m = oa.shape[0]
    tn = 512
    l = layer
    return pl.pallas_call(
        _branch_kernel,
        grid=(m // tm, D_MODEL // tn),
        in_specs=[
            pl.BlockSpec((tm, ATTN_WIDTH), lambda i, j: (i, 0)),
            pl.BlockSpec((tm, SSD_WIDTH), lambda i, j: (i, 0)),
            pl.BlockSpec((None, ATTN_WIDTH, tn), lambda i, j: (l, 0, j)),
            pl.BlockSpec((None, SSD_WIDTH, tn), lambda i, j: (l, 0, j)),
            pl.BlockSpec((tm, tn), lambda i, j: (i, COL_GA // tn + j)),
            pl.BlockSpec((tm, tn), lambda i, j: (i, COL_GS // tn + j)),
        ],
        out_specs=pl.BlockSpec((tm, tn), lambda i, j: (i, j)),
        out_shape=jax.ShapeDtypeStruct((m, D_MODEL), BF16),
        compiler_params=_cparams(("arbitrary", "arbitrary")),
        name="branches",
    )(oa, ys, w_pa, w_ps, act, act)


def _out_kernel(*refs, final, na):
    if final:
        mg_ref, w_ref, xa_ref, xb_ref, gate_ref, fg_ref, ya_ref, yb_ref = refs
    else:
        mg_ref, w_ref, xa_ref, xb_ref, gate_ref, o_ref = refs
    is_ctx = pl.program_id(0) < na
    x = jnp.where(is_ctx, xa_ref[...], xb_ref[...])
    y = x + gate_ref[...] * _dot(mg_ref[...], w_ref[...])
    if not final:
        o_ref[...] = y
        return
    ms = jnp.mean(y * y, axis=-1, keepdims=True)
    y = y * lax.rsqrt(ms + EPS) * fg_ref[...]

    @pl.when(is_ctx)
    def _():
        ya_ref[...] = y

    @pl.when(jnp.logical_not(is_ctx))
    def _():
        yb_ref[...] = y


def _out_proj(merged, w_out, xa, xb, xb_offset, mod, final_g, *, layer, tm, na, group_of):
    m = merged.shape[0]
    l = layer
    final = final_g is not None
    in_specs = [
        pl.BlockSpec((tm, D_MODEL), lambda i: (i, 0)),
        pl.BlockSpec((None, D_MODEL, D_MODEL), lambda i: (l, 0, 0)),
    ] + _row_split_specs(tm, na, xb_offset) + [
        pl.BlockSpec((None, None, 1, D_MODEL), lambda i: (l, group_of(i, tm), 0, 2)),
    ]
    args = [merged, w_out, xa, xb, mod]
    if final:
        in_specs.append(pl.BlockSpec((1, D_MODEL), lambda i: (0, 0)))
        args.append(final_g.reshape(1, D_MODEL))
        out_specs = [
            pl.BlockSpec((tm, D_MODEL), lambda i: (jnp.minimum(i, na - 1), 0)),
            pl.BlockSpec((tm, D_MODEL), lambda i: (jnp.maximum(i - na, 0), 0)),
        ]
        out_shape = [jax.ShapeDtypeStruct((na * tm, D_MODEL), F32),
                     jax.ShapeDtypeStruct((m - na * tm, D_MODEL), F32)]
    else:
        out_specs = pl.BlockSpec((tm, D_MODEL), lambda i: (i, 0))
        out_shape = jax.ShapeDtypeStruct((m, D_MODEL), F32)
    return pl.pallas_call(
        functools.partial(_out_kernel, final=final, na=na),
        grid=(m // tm,),
        in_specs=in_specs,
        out_specs=out_specs,
        out_shape=out_shape,
        compiler_params=_cparams(("arbitrary",)),
        name="out_proj_final" if final else "out_proj",
    )(*args)


def _dt_permutation():
    perm = np.zeros(DT_WIDTH, np.int32)
    for g in range(N_SSD_GROUPS):
        for d in range(2):
            for r in range(HEADS_PER_GROUP):
                perm[g * 16 + d * HEADS_PER_GROUP + r] = d * N_SSD_HEADS + g * HEADS_PER_GROUP + r
    return perm


def kernel(x_prompt, x_sample, c, cache_k, cache_v, state_ssm_fwd, state_ssm_bwd, c_ctx, norm_g, w_mod, b_mod, w_in, conv_w, conv_b, attn_sink, a_log_fwd, a_log_bwd, dt_bias_fwd, dt_bias_bwd, d_skip, ssd_norm_g, w_pa, w_ps, w_out, final_norm_g):
    bc, lc, _ = x_prompt.shape
    bl, ll, _ = x_sample.shape
    depth = w_in.shape[0]
    n_ctx = bc * lc
    m = n_ctx + bl * ll
    assert n_ctx % ll == 0 and ll % lc == 0 and lc % CHUNK == 0
    assert 1 + bl <= COND_ROWS

    def group_of(i, tm):
        return jnp.maximum(i * tm - n_ctx + ll, 0) // ll

    tm_big = math.gcd(1024, math.gcd(n_ctx, ll))
    tm_small = math.gcd(512, tm_big)

    cond = jnp.zeros((COND_ROWS, D_MODEL), F32).at[0].set(c_ctx).at[1:1 + bl].set(c)
    mod = _modulation(cond, w_mod, b_mod)

    perm = _dt_permutation()
    consts = _ssd_constants()
    wdt = w_in[:, :, W_IN_DT:W_IN_DT + DT_WIDTH][:, :, perm]
    wdtt = jnp.swapaxes(wdt, 1, 2)
    bias = jnp.concatenate([dt_bias_fwd, dt_bias_bwd], axis=1)[:, perm]
    alog = jnp.concatenate([a_log_fwd, a_log_bwd], axis=1)[:, perm]
    norm_g3 = norm_g.reshape(depth, 1, D_MODEL)
    conv_b3 = conv_b.reshape(depth, 1, CONV_WIDTH)
    dsk = jnp.repeat(d_skip, SSD_HEADDIM, axis=1).reshape(depth, 1, SSD_WIDTH)
    ng = ssd_norm_g.reshape(depth, 1, SSD_WIDTH)
    w_pa_bf = w_pa.astype(BF16)
    w_ps_bf = w_ps.astype(BF16)
    w_out_bf = w_out.astype(BF16)
    s0f = state_ssm_fwd.reshape(bl, depth, SSD_WIDTH, D_STATE)
    s0b = state_ssm_bwd.reshape(bl, depth, SSD_WIDTH, D_STATE)

    na_small = n_ctx // tm_small
    na_big = n_ctx // tm_big
    xa, xb, xb_off = x_prompt.reshape(n_ctx, D_MODEL), x_sample.reshape(bl * ll, D_MODEL), 0
    k_new = v_new = states = None
    for l in range(depth):
        h, p1, p2, rowp = _prep(xa, xb, xb_off, m, mod, norm_g3, wdtt, bias, alog, consts,
                                layer=l, tm=tm_small, na=na_small, group_of=group_of)
        act, k_new, v_new = _inproj(h, w_in, k_new, v_new, layer=l, tm=tm_big, na=na_big, lc=lc,
                                    bc=bc)

        sink = attn_sink[l]
        oa = _ctx_attention(act, sink, L=lc, nseq=bc)
        oa = _lat_attention(act, sink, cache_k, cache_v, oa, L=ll, nseq=bl,
                            row_block0=n_ctx // ll, layer=l)

        ys, sf, sb = _ssd(act, p1, p2, rowp, consts, conv_w, conv_b3, dsk, ng, layer=l, L=lc,
                          nseq=bc, row_block0=0, state_prev=states)
        states = (sf, sb)
        (ys,) = _ssd(act, p1, p2, rowp, consts, conv_w, conv_b3, dsk, ng, layer=l, L=ll, nseq=bl,
                     row_block0=n_ctx // ll, s0f=s0f, s0b=s0b, y_prev=ys)

        merged = _branches(oa, ys, w_pa_bf, w_ps_bf, act, layer=l, tm=tm_big)
        last = l == depth - 1
        res = _out_proj(merged, w_out_bf, xa, xb, xb_off, mod, final_norm_g if last else None,
                        layer=l, tm=tm_small, na=na_small, group_of=group_of)
        if not last:
            xa, xb, xb_off = res, res, na_small

    y_prompt = res[0].reshape(bc, lc, D_MODEL)
    y_sample = res[1].reshape(bl, ll, D_MODEL)
    shape_kv = (bc, depth, lc, N_KV_HEADS, HEAD_DIM)
    shape_st = (bc, depth, N_SSD_HEADS, SSD_HEADDIM, D_STATE)
    return (y_prompt, y_sample, k_new.reshape(shape_kv), v_new.reshape(shape_kv),
            states[0].reshape(shape_st), states[1].reshape(shape_st))
```

```python
import functools
import math

import numpy as np
import jax
import jax.numpy as jnp
from jax import lax
from jax.experimental import pallas as pl
from jax.experimental.pallas import tpu as pltpu

F32 = jnp.float32
BF16 = jnp.bfloat16

D_MODEL = 2048
HEAD_DIM = 128
N_Q_HEADS = 16
N_KV_HEADS = 4
Q_PER_KV = 4
ATTN_WIDTH = 2048
KV_WIDTH = 512
WINDOW = 128
GRID_W = 64
ROPE_BASE = 10000.0
SSD_WIDTH = 4096
SSD_HEADDIM = 64
N_SSD_HEADS = 64
D_STATE = 128
N_SSD_GROUPS = 8
HEADS_PER_GROUP = 8
GROUP_WIDTH = SSD_WIDTH // N_SSD_GROUPS
CHUNK = 128
D_CONV = 5
BC_WIDTH = 1024
CONV_WIDTH = 6144
EPS = 1e-6
MOD_WIDTH = 3 * D_MODEL
LOG2_E = math.log2(math.e)

COL_Q = 0
COL_K = 2048
COL_V = 2560
COL_ZA = 3072
COL_XBC = 5120
COL_ZS = 11264
COL_GA = 15360
COL_GS = 17408
ACT_WIDTH = 19456
W_IN_DT = 15360
DT_WIDTH = 2 * N_SSD_HEADS

LANES = 128
SUBLANES = 8
VMEM_LIMIT = 56 * 1024 * 1024

ROWS_BIG = 1024
ROWS_SMALL = 512
MOD_TN = 512
BRANCH_TN = 512
CTX_ATTN_SEQS = 8
LAT_ATTN_UNROLL = 4

COND_ROWS = 8


def _cparams(sem):
    return pltpu.CompilerParams(dimension_semantics=sem, vmem_limit_bytes=VMEM_LIMIT)


def _dot(a, b):
    return jnp.dot(a, b, preferred_element_type=F32)


def _dot_nt(a, b):
    return lax.dot_general(a, b, (((1,), (1,)), ((), ())), preferred_element_type=F32)


def _split2(x):
    hi = x.astype(BF16)
    lo = (x - hi.astype(F32)).astype(BF16)
    return hi, lo


def _split3(x):
    p1 = x.astype(BF16)
    r1 = x - p1.astype(F32)
    p2 = r1.astype(BF16)
    p3 = (r1 - p2.astype(F32)).astype(BF16)
    return p1, p2, p3


def _dot3(a, b):
    ah, al = _split2(a)
    bh, bl = _split2(b)
    return _dot(ah, bh) + _dot(al, bh) + _dot(ah, bl)


def _sigmoid(x):
    return 0.5 + 0.5 * jnp.tanh(0.5 * x)


def _silu(x):
    half = 0.5 * x
    return half + half * jnp.tanh(half)


def _softplus(x):
    return jnp.maximum(x, 0.0) + jnp.log1p(jnp.exp(-jnp.abs(x)))


def _mod_kernel(cond_ref, w_ref, b_ref, o_ref):
    res = _dot3(_silu(cond_ref[...]), w_ref[...]) + b_ref[...]
    for r in range(COND_ROWS):
        o_ref[r] = res[r:r + 1, :]


def _modulation(cond, w_mod, b_mod):
    depth = w_mod.shape[0]
    tn = MOD_TN
    return pl.pallas_call(
        _mod_kernel,
        grid=(depth, MOD_WIDTH // tn),
        in_specs=[
            pl.BlockSpec((COND_ROWS, D_MODEL), lambda l, j: (0, 0)),
            pl.BlockSpec((None, D_MODEL, tn), lambda l, j: (l, 0, j)),
            pl.BlockSpec((None, 1, tn), lambda l, j: (l, 0, j)),
        ],
        out_specs=pl.BlockSpec((None, COND_ROWS, 1, tn), lambda l, j: (l, 0, 0, j)),
        out_shape=jax.ShapeDtypeStruct((depth, COND_ROWS, 1, MOD_WIDTH), F32),
        compiler_params=_cparams(("arbitrary", "arbitrary")),
        name="modulation",
    )(cond, w_mod, b_mod.reshape(depth, 1, MOD_WIDTH))


def _prep_kernel(xa_ref, xb_ref, shift_ref, scale_ref, g_ref, wdtt_ref, biast_ref, alogt_ref,
                 sel1_ref, sel2_ref, h_ref, p1_ref, p2_ref, row_ref, *, tm, na):
    x = jnp.where(pl.program_id(0) < na, xa_ref[...], xb_ref[...])
    ms = jnp.mean(x * x, axis=-1, keepdims=True)
    h = (x * lax.rsqrt(ms + EPS) * g_ref[...]) * (1.0 + scale_ref[...]) + shift_ref[...]
    h_ref[...] = h.astype(BF16)

    hh, hl = _split2(h)
    wth, wtl = _split2(wdtt_ref[...])
    rawt = _dot_nt(wth, hh) + _dot_nt(wth, hl) + _dot_nt(wtl, hh)
    dtt = _softplus(rawt + biast_ref[...])
    dtat = dtt * (-jnp.exp(alogt_ref[...]))

    ii = lax.broadcasted_iota(jnp.int32, (CHUNK, CHUNK), 0)
    kk = lax.broadcasted_iota(jnp.int32, (CHUNK, CHUNK), 1)
    lt = jnp.where(kk <= ii, 1.0, 0.0).astype(BF16)
    ut = jnp.where(kk >= ii, 1.0, 0.0).astype(BF16)
    fwd_row = (ii % N_HD) < HEADS_PER_GROUP

    for c in range(tm // CHUNK):
        rows = slice(c * CHUNK, (c + 1) * CHUNK)
        dt_t = dtt[:, rows]
        q1, q2, q3 = _split3(dtat[:, rows])
        pre_t = _dot(q1, ut) + _dot(q2, ut) + _dot(q3, ut)
        suf_t = _dot(q1, lt) + _dot(q2, lt) + _dot(q3, lt)
        acs_t = jnp.where(fwd_row, pre_t, suf_t)
        edge_t = jnp.where(fwd_row, acs_t[:, CHUNK - 1:CHUNK], acs_t[:, 0:1])
        w1_t = dt_t * jnp.exp(edge_t - acs_t)
        row_ref[c, 0] = (acs_t - jnp.log(dt_t)) * LOG2_E
        row_ref[c, 1] = dt_t
        acs = acs_t.T
        p1 = _dot(jnp.concatenate(_split3(acs * LOG2_E), axis=1), sel1_ref[...])
        p1_ref[rows, :] = p1.astype(BF16)
        p2 = _dot(jnp.concatenate(_split3(jnp.exp(acs)) + _split3(w1_t.T), axis=1), sel2_ref[...])
        p2_ref[rows, :] = p2.astype(BF16)


def _row_split_specs(tm, na, xb_offset):
    return [
        pl.BlockSpec((tm, D_MODEL), lambda i: (jnp.minimum(i, na - 1), 0)),
        pl.BlockSpec((tm, D_MODEL), lambda i: (xb_offset + jnp.maximum(i - na, 0), 0)),
    ]


def _prep(xa, xb, xb_offset, m, mod, norm_g, wdtt, bias, alog, consts, *, layer, tm, na, group_of):
    l = layer
    kern = functools.partial(_prep_kernel, tm=tm, na=na)
    par2 = lambda i: (l, 0, 0)
    whole = lambda i: (0, 0)
    pw = N_SSD_GROUPS * LANES
    return pl.pallas_call(
        kern,
        grid=(m // tm,),
        in_specs=_row_split_specs(tm, na, xb_offset) + [
            pl.BlockSpec((None, None, 1, D_MODEL), lambda i: (l, group_of(i, tm), 0, 0)),
            pl.BlockSpec((None, None, 1, D_MODEL), lambda i: (l, group_of(i, tm), 0, 1)),
            pl.BlockSpec((None, 1, D_MODEL), par2),
            pl.BlockSpec((None, DT_WIDTH, D_MODEL), par2),
            pl.BlockSpec((None, DT_WIDTH, 1), par2),
            pl.BlockSpec((None, DT_WIDTH, 1), par2),
            pl.BlockSpec((N_PIECES * LANES, pw), whole),
            pl.BlockSpec((2 * N_PIECES * LANES, pw), whole),
        ],
        out_specs=[
            pl.BlockSpec((tm, D_MODEL), lambda i: (i, 0)),
            pl.BlockSpec((tm, pw), lambda i: (i, 0)),
            pl.BlockSpec((tm, pw), lambda i: (i, 0)),
            pl.BlockSpec((tm // CHUNK, 2, DT_WIDTH, CHUNK), lambda i: (i, 0, 0, 0)),
        ],
        out_shape=[
            jax.ShapeDtypeStruct((m, D_MODEL), BF16),
            jax.ShapeDtypeStruct((m, pw), BF16),
            jax.ShapeDtypeStruct((m, pw), BF16),
            jax.ShapeDtypeStruct((m // CHUNK, 2, DT_WIDTH, CHUNK), F32),
        ],
        compiler_params=_cparams(("arbitrary",)),
        name="prep",
    )(xa, xb, mod, mod, norm_g, wdtt, bias[:, :, None], alog[:, :, None], consts["sel1"],
      consts["sel2"])


N_PIECES = 3
N_HD = 2 * HEADS_PER_GROUP
PIECE_LANES = N_PIECES * N_HD
N_EXPAND = 4


def _ssd_constants():
    pw = N_SSD_GROUPS * LANES
    sel1 = np.zeros((N_PIECES * LANES, pw), np.float32)
    sel2 = np.zeros((2 * N_PIECES * LANES, pw), np.float32)
    for g in range(N_SSD_GROUPS):
        for hd in range(N_HD):
            lam = g * N_HD + hd
            for p in range(N_PIECES):
                sel1[p * LANES + lam, g * LANES + N_PIECES * hd + p] = 1.0
                for q in range(2):
                    sel2[(q * N_PIECES + p) * LANES + lam,
                         g * LANES + q * PIECE_LANES + N_PIECES * hd + p] = 1.0
    cbc = np.zeros((LANES, N_HD * LANES), np.float32)
    for hd in range(N_HD):
        cbc[N_PIECES * hd:N_PIECES * (hd + 1), hd * LANES:(hd + 1) * LANES] = 1.0
    eexp = np.zeros((N_EXPAND, LANES, GROUP_WIDTH), np.float32)
    for e, (q, d) in enumerate(((0, 0), (1, 0), (0, 1), (1, 1))):
        for r in range(HEADS_PER_GROUP):
            row0 = q * PIECE_LANES + N_PIECES * (d * HEADS_PER_GROUP + r)
            eexp[e, row0:row0 + N_PIECES, r * SSD_HEADDIM:(r + 1) * SSD_HEADDIM] = 1.0
    return {k: jnp.asarray(v, BF16) for k, v in
            (("sel1", sel1), ("sel2", sel2), ("cbc", cbc), ("eexp", eexp))}


INPROJ_TN = 1024
KV_BLOCK = COL_K // INPROJ_TN


def _inproj_kernel(*refs, na, lc, aliased):
    if aliased:
        h_ref, w_ref, _kprev, _vprev, act_ref, k_ref, v_ref, wbf_s = refs
    else:
        h_ref, w_ref, act_ref, k_ref, v_ref, wbf_s = refs
    j = pl.program_id(0)
    i = pl.program_id(1)

    @pl.when(i == 0)
    def _():
        wbf_s[...] = w_ref[...].astype(BF16)

    acc = _dot(h_ref[...], wbf_s[...])
    act_ref[...] = acc.astype(BF16)

    @pl.when((j == KV_BLOCK) & (i < na))
    def _():
        for s in range(k_ref.shape[0]):
            for dst, col0 in ((k_ref, 0), (v_ref, KV_WIDTH)):
                for hh in range(N_KV_HEADS):
                    cols = slice(col0 + hh * HEAD_DIM, col0 + (hh + 1) * HEAD_DIM)
                    dst[s, pl.ds(hh, lc, stride=N_KV_HEADS), :] = acc[s * lc:(s + 1) * lc, cols]


def _inproj(h, w_in, k_prev, v_prev, *, layer, tm, na, lc, bc):
    m = h.shape[0]
    depth = w_in.shape[0]
    tn = INPROJ_TN
    l = layer
    spb = tm // lc
    aliased = k_prev is not None

    def w_col(j):
        skip = jnp.where(j >= W_IN_DT // tn, DT_WIDTH // LANES, 0)
        return (j * (tn // LANES) + skip) * LANES

    def kv_idx(j, i):
        return jnp.where(j < KV_BLOCK, 0, jnp.where(j == KV_BLOCK, jnp.minimum(i, na - 1), na - 1))

    in_specs = [
        pl.BlockSpec((tm, D_MODEL), lambda j, i: (i, 0)),
        pl.BlockSpec((None, pl.Element(D_MODEL), pl.Element(tn)), lambda j, i: (l, 0, w_col(j))),
    ]
    args = [h, w_in]
    aliases = {}
    if aliased:
        in_specs += [pl.BlockSpec(memory_space=pl.ANY)] * 2
        args += [k_prev, v_prev]
        aliases = {2: 1, 3: 2}
    kv_rows = lc * N_KV_HEADS
    kv_spec = pl.BlockSpec((spb, None, kv_rows, HEAD_DIM), lambda j, i: (kv_idx(j, i), l, 0, 0))
    return pl.pallas_call(
        functools.partial(_inproj_kernel, na=na, lc=lc, aliased=aliased),
        grid=(ACT_WIDTH // tn, m // tm),
        in_specs=in_specs,
        out_specs=[pl.BlockSpec((tm, tn), lambda j, i: (i, j)), kv_spec, kv_spec],
        out_shape=[
            jax.ShapeDtypeStruct((m, ACT_WIDTH), BF16),
            jax.ShapeDtypeStruct((bc, depth, kv_rows, HEAD_DIM), F32),
            jax.ShapeDtypeStruct((bc, depth, kv_rows, HEAD_DIM), F32),
        ],
        scratch_shapes=[pltpu.VMEM((D_MODEL, tn), BF16)],
        input_output_aliases=aliases,
        compiler_params=_cparams(("arbitrary", "arbitrary")),
        name="inproj",
    )(*args)


PAD = SUBLANES
CHUNK_UNROLL = 4
SSD_SCRATCH_BUDGET = 28 * 1024 * 1024


def _ssd_kernel(*refs, L, per_step, has_init, emit_state, n_alias):
    refs = list(refs)
    seq_in = refs[:7]
    shared = refs[7:17]
    pos = 17
    state_in = []
    if has_init:
        state_in = refs[pos:pos + 2]
        pos += 2
    pos += n_alias
    y_ref = refs[pos]
    pos += 1
    state_out = []
    if emit_state:
        state_out = refs[pos:pos + 2]
        pos += 2
    scratch = refs[pos:]
    nc = L // CHUNK
    for sq in range(per_step):
        rows = pl.ds(sq * L, L)
        views = [r.at[rows] for r in seq_in[:6]] + [seq_in[6].at[pl.ds(sq * nc, nc)]]
        _ssd_sequence(views, shared, [r.at[pl.ds(sq, 1)] for r in state_in], y_ref.at[rows],
                      [r.at[pl.ds(sq, 1)] for r in state_out], [s.at[sq] for s in scratch], L=L)


def _ssd_sequence(seq_in, shared, state_in, y_ref, state_out, scratch, *, L):
    x_ref, b_ref, c_ref, z_ref, p1_ref, p2_ref, row_ref = seq_in
    (cbc_ref, eexp_ref, cwx_ref, cwb_ref, cwc_ref, cbx_ref, cbb_ref, cbias_c_ref, dsk_ref,
     ng_ref) = shared
    has_init = bool(state_in)
    emit_state = bool(state_out)
    if has_init:
        s0f_ref, s0b_ref = state_in
    if emit_state:
        sf_ref, sb_ref = state_out
    pad_s, xc_s, bc_s, cc_s, bt_s, cum_s, exp_s, yacc_s, sft_s, sbt_s = scratch
    nc = L // CHUNK

    tile = 2 * LANES
    conv_srcs = ((x_ref, cwx_ref, cbx_ref, xc_s, 0, GROUP_WIDTH),
                 (b_ref, cwb_ref, cbb_ref, bc_s, GROUP_WIDTH, D_STATE),
                 (c_ref, cwc_ref, cbias_c_ref, cc_s, GROUP_WIDTH + D_STATE, D_STATE))
    zeros = jnp.zeros((PAD, pad_s.shape[1]), F32)
    pad_s[0:PAD, :] = zeros
    pad_s[L + PAD:L + 2 * PAD, :] = zeros
    for src_ref, _, _, _, off, width in conv_srcs:
        for c in range(nc):
            pad_s[PAD + c * CHUNK:PAD + (c + 1) * CHUNK, off:off + width] = (
                src_ref[c * CHUNK:(c + 1) * CHUNK, :].astype(F32))

    def conv_slab(w_ref, bias_ref, dst_s, off, c, s, zero_row):
        ls = slice(s * LANES, (s + 1) * LANES)
        ps = slice(off + s * LANES, off + (s + 1) * LANES)
        halo = pad_s[c * CHUNK:(c + 1) * CHUNK + 2 * PAD, ps]
        centre = D_CONV // 2
        acc = jnp.broadcast_to(bias_ref[:, ls] + zero_row, (CHUNK, LANES))
        for k in range(D_CONV):
            shifted = halo if k == centre else pltpu.roll(halo, (centre - k) % halo.shape[0], 0)
            acc = acc + w_ref[k:k + 1, ls] * shifted[PAD:PAD + CHUNK, :]
        dst_s[c * CHUNK:(c + 1) * CHUNK, ls] = _silu(acc)

    conv_items = [functools.partial(conv_slab, w_ref, bias_ref, dst_s, off, c, s)
                  for _, w_ref, bias_ref, dst_s, off, width in conv_srcs
                  for c in range(nc) for s in range(width // LANES)]


    def zero_row_of(res):
        bits = pltpu.bitcast(res[0:SUBLANES, 0:LANES], jnp.uint32)
        return pltpu.bitcast((bits >> 16) >> 16, F32)[0:1, :]

    def cum_tile(t):
        ls = slice(t * tile, (t + 1) * tile)
        res = _dot(p1_ref[...], cbc_ref[:, ls])
        cum_s[:, ls] = res
        return zero_row_of(res)

    def exp_tile(e, t):
        ls = slice(t * tile, (t + 1) * tile)
        res = _dot(p2_ref[...], eexp_ref[e, :, ls])
        exp_s[e, :, ls] = res
        return zero_row_of(res)

    spread_items = ([functools.partial(cum_tile, t) for t in range(N_HD * LANES // tile)]
                    + [functools.partial(exp_tile, e, t) for e in range(N_EXPAND)
                       for t in range(GROUP_WIDTH // tile)])

    merged = sorted([((i + 0.5) / len(conv_items), 0, f) for i, f in enumerate(conv_items)]
                    + [((i + 0.5) / len(spread_items), 1, f) for i, f in enumerate(spread_items)],
                    key=lambda item: item[:2])
    zero_row = jnp.zeros((1, LANES), F32)
    for _, is_spread, emit in merged:
        if is_spread:
            zero_row = emit()
        else:
            emit(zero_row)

    if has_init:
        sft_s[...] = s0f_ref[0].T
        sbt_s[...] = s0b_ref[0].T
    else:
        sft_s[...] = jnp.zeros_like(sft_s)
        sbt_s[...] = jnp.zeros_like(sbt_s)

    ii = lax.broadcasted_iota(jnp.int32, (CHUNK, CHUNK), 0)
    jj = lax.broadcasted_iota(jnp.int32, (CHUNK, CHUNK), 1)
    lower = jj <= ii
    diag = jj == ii
    left = jj < SSD_HEADDIM

    def fwd_chunk(c, carry):
        r0 = pl.multiple_of(c * CHUNK, CHUNK)
        rows = pl.ds(r0, CHUNK)
        xq = xc_s[rows, :]
        bq = bc_s[rows, :]
        cq = cc_s[rows, :].astype(BF16)
        rowa = row_ref[c, 0]
        rowd = row_ref[c, 1]
        cb = _dot_nt(cq, bq.astype(BF16))
        y_off = _dot(cq, sft_s[...].astype(BF16)) * exp_s[0, rows, :]
        y_parts = []
        for k in range(HEADS_PER_GROUP // 2):
            ms = []
            for r in (2 * k, 2 * k + 1):
                rb = HEADS_PER_GROUP + r
                seg_f = cum_s[rows, r * LANES:(r + 1) * LANES] - rowa[r:r + 1, :]
                seg_b = cum_s[rows, rb * LANES:(rb + 1) * LANES] - rowa[rb:rb + 1, :]
                dm = jnp.exp2(jnp.where(lower, seg_f, seg_b))
                dm = dm + jnp.where(diag, rowd[rb:rb + 1, :], 0.0)
                ms.append((cb * dm).astype(BF16))
            lhs = jnp.concatenate(ms, axis=1)
            xp = xq[:, k * LANES:(k + 1) * LANES]
            rhs = jnp.concatenate([jnp.where(left, xp, 0.0), jnp.where(left, 0.0, xp)],
                                  axis=0).astype(BF16)
            y_parts.append(_dot(lhs, rhs))
        yacc_s[rows, :] = jnp.concatenate(y_parts, axis=1) + y_off
        decay = exp_s[0, pl.ds(r0 + CHUNK - 1, 1), :]
        bt = bq.T.astype(BF16)
        bt_s[c] = bt
        sft_s[...] = sft_s[...] * decay + _dot(bt, (xq * exp_s[1, rows, :]).astype(BF16))
        return carry

    lax.fori_loop(0, nc, fwd_chunk, 0, unroll=CHUNK_UNROLL)

    def bwd_chunk(t, carry):
        c = nc - 1 - t
        r0 = pl.multiple_of(c * CHUNK, CHUNK)
        rows = pl.ds(r0, CHUNK)
        xq = xc_s[rows, :]
        cq = cc_s[rows, :].astype(BF16)
        y = (yacc_s[rows, :] + _dot(cq, sbt_s[...].astype(BF16)) * exp_s[2, rows, :]
             + dsk_ref[...] * xq)
        y = y * _silu(z_ref[rows, :].astype(F32))
        ms = jnp.mean(y * y, axis=-1, keepdims=True)
        y_ref[rows, :] = (y * lax.rsqrt(ms + EPS) * ng_ref[...]).astype(BF16)
        decay = exp_s[2, pl.ds(r0, 1), :]
        sbt_s[...] = sbt_s[...] * decay + _dot(bt_s[c], (xq * exp_s[3, rows, :]).astype(BF16))
        return carry

    lax.fori_loop(0, nc, bwd_chunk, 0, unroll=CHUNK_UNROLL)

    if emit_state:
        sf_ref[0] = sft_s[...].T
        sb_ref[0] = sbt_s[...].T


def _ssd(act, p1, p2, rowp, consts, conv_w, conv_b, dsk, ng, *, layer, L, nseq, row_block0,
         s0f=None, s0b=None, y_prev=None, state_prev=None):
    m = act.shape[0]
    depth = conv_w.shape[0]
    l = layer
    has_init = s0f is not None
    emit_state = not has_init
    nc = L // CHUNK
    gw = GROUP_WIDTH
    rb = row_block0
    off_b = SSD_WIDTH // D_STATE
    off_c = (SSD_WIDTH + BC_WIDTH) // D_STATE
    seq_scratch = [
        ((L + 2 * PAD, gw + 2 * D_STATE), F32),
        ((L, gw), F32),
        ((L, D_STATE), F32),
        ((L, D_STATE), F32),
        ((nc, D_STATE, CHUNK), BF16),
        ((L, N_HD * LANES), F32),
        ((N_EXPAND, L, gw), F32),
        ((L, gw), F32),
        ((D_STATE, gw), F32),
        ((D_STATE, gw), F32),
    ]
    seq_bytes = sum(math.prod(shape) * jnp.dtype(dt).itemsize for shape, dt in seq_scratch)
    per_step = max(n for n in (1, 2, 4)
                   if nseq % n == 0 and row_block0 % n == 0 and n * seq_bytes <= SSD_SCRATCH_BUDGET)
    rb = row_block0 // per_step
    ls = per_step * L
    in_specs = [
        pl.BlockSpec((ls, gw), lambda b, g: (rb + b, COL_XBC // gw + g)),
        pl.BlockSpec((ls, D_STATE), lambda b, g: (rb + b, COL_XBC // D_STATE + off_b + g)),
        pl.BlockSpec((ls, D_STATE), lambda b, g: (rb + b, COL_XBC // D_STATE + off_c + g)),
        pl.BlockSpec((ls, gw), lambda b, g: (rb + b, COL_ZS // gw + g)),
        pl.BlockSpec((ls, LANES), lambda b, g: (rb + b, g)),
        pl.BlockSpec((ls, LANES), lambda b, g: (rb + b, g)),
        pl.BlockSpec((per_step * nc, 2, N_HD, CHUNK), lambda b, g: (rb + b, 0, g, 0)),
        pl.BlockSpec((LANES, N_HD * LANES), lambda b, g: (0, 0)),
        pl.BlockSpec((N_EXPAND, LANES, gw), lambda b, g: (0, 0, 0)),
        pl.BlockSpec((None, D_CONV, gw), lambda b, g: (l, 0, g)),
        pl.BlockSpec((None, D_CONV, D_STATE), lambda b, g: (l, 0, off_b + g)),
        pl.BlockSpec((None, D_CONV, D_STATE), lambda b, g: (l, 0, off_c + g)),
        pl.BlockSpec((None, 1, gw), lambda b, g: (l, 0, g)),
        pl.BlockSpec((None, 1, D_STATE), lambda b, g: (l, 0, off_b + g)),
        pl.BlockSpec((None, 1, D_STATE), lambda b, g: (l, 0, off_c + g)),
        pl.BlockSpec((None, 1, gw), lambda b, g: (l, 0, g)),
        pl.BlockSpec((None, 1, gw), lambda b, g: (l, 0, g)),
    ]
    args = [act, act, act, act, p1, p2, rowp, consts["cbc"], consts["eexp"], conv_w, conv_w, conv_w,
            conv_b, conv_b, conv_b, dsk, ng]
    aliases = {}
    n_alias = 0
    state_spec = pl.BlockSpec((per_step, None, gw, D_STATE), lambda b, g: (b, l, g, 0))
    if has_init:
        in_specs += [state_spec, state_spec, pl.BlockSpec(memory_space=pl.ANY)]
        args += [s0f, s0b, y_prev]
        aliases = {len(args) - 1: 0}
        n_alias = 1
    elif state_prev is not None:
        in_specs += [pl.BlockSpec(memory_space=pl.ANY)] * 2
        args += list(state_prev)
        aliases = {len(args) - 2: 1, len(args) - 1: 2}
        n_alias = 2
    out_specs = [pl.BlockSpec((ls, gw), lambda b, g: (rb + b, g))]
    out_shape = [jax.ShapeDtypeStruct((m, SSD_WIDTH), BF16)]
    if emit_state:
        out_specs += [state_spec] * 2
        out_shape += [jax.ShapeDtypeStruct((nseq, depth, SSD_WIDTH, D_STATE), F32)] * 2
    scratch = [pltpu.VMEM((per_step,) + shape, dt) for shape, dt in seq_scratch]
    kern = functools.partial(_ssd_kernel, L=L, per_step=per_step, has_init=has_init,
                             emit_state=emit_state, n_alias=n_alias)
    return pl.pallas_call(
        kern,
        grid=(nseq // per_step, N_SSD_GROUPS),
        in_specs=in_specs,
        out_specs=out_specs,
        out_shape=out_shape,
        scratch_shapes=scratch,
        input_output_aliases=aliases,
        compiler_params=_cparams(("arbitrary", "arbitrary")),
        name="ssd_latent" if has_init else "ssd_context",
    )(*args)


ATTN_SCALE = HEAD_DIM ** -0.5


def _ctx_attn_kernel(sink_ref, q_ref, k_ref, v_ref, z_ref, o_ref, *, L):
    g = pl.program_id(1)
    for sq in range(q_ref.shape[0] // L):
        rows = slice(sq * L, (sq + 1) * L)
        k = k_ref[rows, :]
        v = v_ref[rows, :]
        for r in range(Q_PER_KV):
            ls = slice(r * HEAD_DIM, (r + 1) * HEAD_DIM)
            sink = sink_ref[g * Q_PER_KV + r]
            s = _dot_nt(q_ref[rows, ls], k) * ATTN_SCALE
            m = jnp.maximum(jnp.max(s, axis=-1, keepdims=True), sink)
            p = jnp.exp(s - m)
            denom = jnp.sum(p, axis=-1, keepdims=True) + jnp.exp(sink - m)
            o = _dot(p.astype(BF16), v) / denom
            o_ref[rows, ls] = (o * _silu(z_ref[rows, ls].astype(F32))).astype(BF16)


def _ctx_attention(act, sink, *, L, nseq):
    m = act.shape[0]
    gw = Q_PER_KV * HEAD_DIM
    per_step = math.gcd(nseq, CTX_ATTN_SEQS)
    rows = per_step * L
    return pl.pallas_call(
        functools.partial(_ctx_attn_kernel, L=L),
        grid=(nseq // per_step, N_KV_HEADS),
        in_specs=[
            pl.BlockSpec(memory_space=pltpu.SMEM),
            pl.BlockSpec((rows, gw), lambda b, g: (b, COL_Q // gw + g)),
            pl.BlockSpec((rows, HEAD_DIM), lambda b, g: (b, COL_K // HEAD_DIM + g)),
            pl.BlockSpec((rows, HEAD_DIM), lambda b, g: (b, COL_V // HEAD_DIM + g)),
            pl.BlockSpec((rows, gw), lambda b, g: (b, COL_ZA // gw + g)),
        ],
        out_specs=pl.BlockSpec((rows, gw), lambda b, g: (b, g)),
        out_shape=jax.ShapeDtypeStruct((m, ATTN_WIDTH), BF16),
        compiler_params=_cparams(("arbitrary", "arbitrary")),
        name="attn_context",
    )(sink, act, act, act, act)


def _rope_tables(length):
    sec = HEAD_DIM // 2
    half = sec // 2
    d = np.arange(HEAD_DIM)
    e = d % sec
    freqs = ROPE_BASE ** (-(e % half).astype(np.float64) / half)
    t = np.arange(length)
    pos = np.where((d // sec)[None, :] == 0, (t // GRID_W)[:, None], (t % GRID_W)[:, None])
    ang = pos.astype(np.float64) * freqs[None, :]
    sign = np.where(e < half, -1.0, 1.0)[None, :]
    return (jnp.asarray(np.cos(ang), F32), jnp.asarray(np.sin(ang) * sign, F32))


def _rope(x, cos, sin_signed, first_half):
    partner = jnp.where(first_half, pltpu.roll(x, LANES - HEAD_DIM // 4, 1),
                        pltpu.roll(x, HEAD_DIM // 4, 1))
    return x * cos + partner * sin_signed


def _lat_attn_kernel(sink_ref, q_ref, k_ref, v_ref, z_ref, kc_ref, vc_ref, cos_ref, sin_ref,
                     _oprev_ref, o_ref, kctx_s, vctx_s, keys_s, vals_s, *, L, lc):
    g = pl.program_id(1)
    nb = L // CHUNK
    win = 3 * CHUNK
    rows4 = Q_PER_KV * CHUNK
    lane = lax.broadcasted_iota(jnp.int32, (CHUNK, HEAD_DIM), 1)
    first_half = (lane % (HEAD_DIM // 2)) < (HEAD_DIM // 4)

    kctx_s[...] = kc_ref[...].astype(BF16)
    vctx_s[...] = vc_ref[...].astype(BF16)
    zero_blk = jnp.zeros((CHUNK, HEAD_DIM), BF16)
    for dst in (keys_s, vals_s):
        dst[0:CHUNK, :] = zero_blk
        dst[CHUNK + L:2 * CHUNK + L, :] = zero_blk
    for n in range(nb):
        rows = slice(n * CHUNK, (n + 1) * CHUNK)
        kr = _rope(k_ref[rows, :].astype(F32), cos_ref[rows, :], sin_ref[rows, :], first_half)
        keys_s[CHUNK + n * CHUNK:CHUNK + (n + 1) * CHUNK, :] = kr.astype(BF16)
        vals_s[CHUNK + n * CHUNK:CHUNK + (n + 1) * CHUNK, :] = v_ref[rows, :]

    qi = lax.broadcasted_iota(jnp.int32, (rows4, win), 0) % CHUNK
    wi = lax.broadcasted_iota(jnp.int32, (rows4, win), 1)
    band = jnp.abs(qi - wi + CHUNK) <= WINDOW
    head = lax.broadcasted_iota(jnp.int32, (rows4, 1), 0) // CHUNK
    sink = jnp.zeros((rows4, 1), F32)
    for r in range(Q_PER_KV):
        sink = jnp.where(head == r, sink_ref[g * Q_PER_KV + r], sink)

    def block(n, carry):
        r0 = pl.multiple_of(n * CHUNK, CHUNK)
        rows = pl.ds(r0, CHUNK)
        cos = cos_ref[rows, :]
        sin = sin_ref[rows, :]
        q = jnp.concatenate(
            [_rope(q_ref[rows, r * HEAD_DIM:(r + 1) * HEAD_DIM].astype(F32), cos, sin,
                   first_half).astype(BF16) for r in range(Q_PER_KV)], axis=0)
        in_seq = (wi >= CHUNK - r0) & (wi < L + CHUNK - r0)
        s_ctx = _dot_nt(q, kctx_s[...]) * ATTN_SCALE
        s_lat = _dot_nt(q, keys_s[pl.ds(r0, win), :]) * ATTN_SCALE
        s_lat = jnp.where(in_seq, jnp.where(band, s_lat, -jnp.inf), -jnp.inf)
        m = jnp.maximum(jnp.maximum(jnp.max(s_ctx, axis=-1, keepdims=True),
                                    jnp.max(s_lat, axis=-1, keepdims=True)), sink)
        p_ctx = jnp.exp(s_ctx - m)
        p_lat = jnp.exp(s_lat - m)
        denom = (jnp.sum(p_ctx, axis=-1, keepdims=True) + jnp.sum(p_lat, axis=-1, keepdims=True)
                 + jnp.exp(sink - m))
        o = (_dot(p_ctx.astype(BF16), vctx_s[...])
             + _dot(p_lat.astype(BF16), vals_s[pl.ds(r0, win), :])) / denom
        for r in range(Q_PER_KV):
            ls = slice(r * HEAD_DIM, (r + 1) * HEAD_DIM)
            o_ref[rows, ls] = (o[r * CHUNK:(r + 1) * CHUNK, :]
                               * _silu(z_ref[rows, ls].astype(F32))).astype(BF16)
        return carry

    lax.fori_loop(0, nb, block, 0, unroll=LAT_ATTN_UNROLL)


def _lat_attention(act, sink, cache_k, cache_v, o_prev, *, L, nseq, row_block0, layer):
    m = act.shape[0]
    gw = Q_PER_KV * HEAD_DIM
    lc = cache_k.shape[2]
    rb = row_block0
    cos, sin = _rope_tables(L)
    kc = cache_k.reshape(cache_k.shape[0], cache_k.shape[1], lc, KV_WIDTH)
    vc = cache_v.reshape(cache_v.shape[0], cache_v.shape[1], lc, KV_WIDTH)
    kern = functools.partial(_lat_attn_kernel, L=L, lc=lc)
    tab = lambda b, g: (0, 0)
    return pl.pallas_call(
        kern,
        grid=(nseq, N_KV_HEADS),
        in_specs=[
            pl.BlockSpec(memory_space=pltpu.SMEM),
            pl.BlockSpec((L, gw), lambda b, g: (rb + b, COL_Q // gw + g)),
            pl.BlockSpec((L, HEAD_DIM), lambda b, g: (rb + b, COL_K // HEAD_DIM + g)),
            pl.BlockSpec((L, HEAD_DIM), lambda b, g: (rb + b, COL_V // HEAD_DIM + g)),
            pl.BlockSpec((L, gw), lambda b, g: (rb + b, COL_ZA // gw + g)),
            pl.BlockSpec((None, None, lc, HEAD_DIM), lambda b, g: (b, layer, 0, g)),
            pl.BlockSpec((None, None, lc, HEAD_DIM), lambda b, g: (b, layer, 0, g)),
            pl.BlockSpec((L, HEAD_DIM), tab),
            pl.BlockSpec((L, HEAD_DIM), tab),
            pl.BlockSpec(memory_space=pl.ANY),
        ],
        out_specs=pl.BlockSpec((L, gw), lambda b, g: (rb + b, g)),
        out_shape=jax.ShapeDtypeStruct((m, ATTN_WIDTH), BF16),
        scratch_shapes=[pltpu.VMEM((lc, HEAD_DIM), BF16), pltpu.VMEM((lc, HEAD_DIM), BF16),
                        pltpu.VMEM((L + 2 * CHUNK, HEAD_DIM), BF16),
                        pltpu.VMEM((L + 2 * CHUNK, HEAD_DIM), BF16)],
        input_output_aliases={9: 0},
        compiler_params=_cparams(("arbitrary", "arbitrary")),
        name="attn_latent",
    )(sink, act, act, act, act, kc, vc, cos, sin, o_prev)


def _branch_kernel(oa_ref, ys_ref, wpa_ref, wps_ref, ga_ref, gs_ref, o_ref):
    a = _dot(oa_ref[...], wpa_ref[...])
    s = _dot(ys_ref[...], wps_ref[...])
    merged = _sigmoid(ga_ref[...].astype(F32)) * a + _sigmoid(gs_ref[...].astype(F32)) * s
    o_ref[...] = merged.astype(BF16)


def _branches(oa, ys, w_pa, w_ps, act, *, layer, tm):
    m = oa.shape[0]
    tn = BRANCH_TN
    l = layer
    return pl.pallas_call(
        _branch_kernel,
        grid=(m // tm, D_MODEL // tn),
        in_specs=[
            pl.BlockSpec((tm, ATTN_WIDTH), lambda i, j: (i, 0)),
            pl.BlockSpec((tm, SSD_WIDTH), lambda i, j: (i, 0)),
            pl.BlockSpec((None, ATTN_WIDTH, tn), lambda i, j: (l, 0, j)),
            pl.BlockSpec((None, SSD_WIDTH, tn), lambda i, j: (l, 0, j)),
            pl.BlockSpec((tm, tn), lambda i, j: (i, COL_GA // tn + j)),
            pl.BlockSpec((tm, tn), lambda i, j: (i, COL_GS // tn + j)),
        ],
        out_specs=pl.BlockSpec((tm, tn), lambda i, j: (i, j)),
        out_shape=jax.ShapeDtypeStruct((m, D_MODEL), BF16),
        compiler_params=_cparams(("arbitrary", "arbitrary")),
        name="branches",
    )(oa, ys, w_pa, w_ps, act, act)


def _out_kernel(*refs, final, na):
    if final:
        mg_ref, w_ref, xa_ref, xb_ref, gate_ref, fg_ref, ya_ref, yb_ref = refs
    else:
        mg_ref, w_ref, xa_ref, xb_ref, gate_ref, o_ref = refs
    is_ctx = pl.program_id(0) < na
    x = jnp.where(is_ctx, xa_ref[...], xb_ref[...])
    y = x + gate_ref[...] * _dot(mg_ref[...], w_ref[...])
    if not final:
        o_ref[...] = y
        return
    ms = jnp.mean(y * y, axis=-1, keepdims=True)
    y = y * lax.rsqrt(ms + EPS) * fg_ref[...]

    @pl.when(is_ctx)
    def _():
        ya_ref[...] = y

    @pl.when(jnp.logical_not(is_ctx))
    def _():
        yb_ref[...] = y


def _out_proj(merged, w_out, xa, xb, xb_offset, mod, final_g, *, layer, tm, na, group_of):
    m = merged.shape[0]
    l = layer
    final = final_g is not None
    in_specs = [
        pl.BlockSpec((tm, D_MODEL), lambda i: (i, 0)),
        pl.BlockSpec((None, D_MODEL, D_MODEL), lambda i: (l, 0, 0)),
    ] + _row_split_specs(tm, na, xb_offset) + [
        pl.BlockSpec((None, None, 1, D_MODEL), lambda i: (l, group_of(i, tm), 0, 2)),
    ]
    args = [merged, w_out, xa, xb, mod]
    if final:
        in_specs.append(pl.BlockSpec((1, D_MODEL), lambda i: (0, 0)))
        args.append(final_g.reshape(1, D_MODEL))
        out_specs = [
            pl.BlockSpec((tm, D_MODEL), lambda i: (jnp.minimum(i, na - 1), 0)),
            pl.BlockSpec((tm, D_MODEL), lambda i: (jnp.maximum(i - na, 0), 0)),
        ]
        out_shape = [jax.ShapeDtypeStruct((na * tm, D_MODEL), F32),
                     jax.ShapeDtypeStruct((m - na * tm, D_MODEL), F32)]
    else:
        out_specs = pl.BlockSpec((tm, D_MODEL), lambda i: (i, 0))
        out_shape = jax.ShapeDtypeStruct((m, D_MODEL), F32)
    return pl.pallas_call(
        functools.partial(_out_kernel, final=final, na=na),
        grid=(m // tm,),
        in_specs=in_specs,
        out_specs=out_specs,
        out_shape=out_shape,
        compiler_params=_cparams(("arbitrary",)),
        name="out_proj_final" if final else "out_proj",
    )(*args)


def _dt_permutation():
    perm = np.zeros(DT_WIDTH, np.int32)
    for g in range(N_SSD_GROUPS):
        for d in range(2):
            for r in range(HEADS_PER_GROUP):
                perm[g * 16 + d * HEADS_PER_GROUP + r] = d * N_SSD_HEADS + g * HEADS_PER_GROUP + r
    return perm


def kernel(x_prompt, x_sample, c, cache_k, cache_v, state_ssm_fwd, state_ssm_bwd, c_ctx, norm_g, w_mod, b_mod, w_in, conv_w, conv_b, attn_sink, a_log_fwd, a_log_bwd, dt_bias_fwd, dt_bias_bwd, d_skip, ssd_norm_g, w_pa, w_ps, w_out, final_norm_g):
    bc, lc, _ = x_prompt.shape
    bl, ll, _ = x_sample.shape
    depth = w_in.shape[0]
    n_ctx = bc * lc
    m = n_ctx + bl * ll
    assert n_ctx % ll == 0 and ll % lc == 0 and lc % CHUNK == 0
    assert 1 + bl <= COND_ROWS

    def group_of(i, tm):
        return jnp.maximum(i * tm - n_ctx + ll, 0) // ll

    tm_big = math.gcd(ROWS_BIG, math.gcd(n_ctx, ll))
    tm_small = math.gcd(ROWS_SMALL, tm_big)

    cond = jnp.zeros((COND_ROWS, D_MODEL), F32).at[0].set(c_ctx).at[1:1 + bl].set(c)
    mod = _modulation(cond, w_mod, b_mod)

    perm = _dt_permutation()
    consts = _ssd_constants()
    wdt = w_in[:, :, W_IN_DT:W_IN_DT + DT_WIDTH][:, :, perm]
    wdtt = jnp.swapaxes(wdt, 1, 2)
    bias = jnp.concatenate([dt_bias_fwd, dt_bias_bwd], axis=1)[:, perm]
    alog = jnp.concatenate([a_log_fwd, a_log_bwd], axis=1)[:, perm]
    norm_g3 = norm_g.reshape(depth, 1, D_MODEL)
    conv_b3 = conv_b.reshape(depth, 1, CONV_WIDTH)
    dsk = jnp.repeat(d_skip, SSD_HEADDIM, axis=1).reshape(depth, 1, SSD_WIDTH)
    ng = ssd_norm_g.reshape(depth, 1, SSD_WIDTH)
    w_pa_bf = w_pa.astype(BF16)
    w_ps_bf = w_ps.astype(BF16)
    w_out_bf = w_out.astype(BF16)
    s0f = state_ssm_fwd.reshape(bl, depth, SSD_WIDTH, D_STATE)
    s0b = state_ssm_bwd.reshape(bl, depth, SSD_WIDTH, D_STATE)

    na_small = n_ctx // tm_small
    na_big = n_ctx // tm_big
    xa, xb, xb_off = x_prompt.reshape(n_ctx, D_MODEL), x_sample.reshape(bl * ll, D_MODEL), 0
    k_new = v_new = states = None
    for l in range(depth):
        h, p1, p2, rowp = _prep(xa, xb, xb_off, m, mod, norm_g3, wdtt, bias, alog, consts,
                                layer=l, tm=tm_small, na=na_small, group_of=group_of)
        act, k_new, v_new = _inproj(h, w_in, k_new, v_new, layer=l, tm=tm_big, na=na_big, lc=lc,
                                    bc=bc)

        sink = attn_sink[l]
        oa = _ctx_attention(act, sink, L=lc, nseq=bc)
        oa = _lat_attention(act, sink, cache_k, cache_v, oa, L=ll, nseq=bl,
                            row_block0=n_ctx // ll, layer=l)

        ys, sf, sb = _ssd(act, p1, p2, rowp, consts, conv_w, conv_b3, dsk, ng, layer=l, L=lc,
                          nseq=bc, row_block0=0, state_prev=states)
        states = (sf, sb)
        (ys,) = _ssd(act, p1, p2, rowp, consts, conv_w, conv_b3, dsk, ng, layer=l, L=ll, nseq=bl,
                     row_block0=n_ctx // ll, s0f=s0f, s0b=s0b, y_prev=ys)

        merged = _branches(oa, ys, w_pa_bf, w_ps_bf, act, layer=l, tm=tm_big)
        last = l == depth - 1
        res = _out_proj(merged, w_out_bf, xa, xb, xb_off, mod, final_norm_g if last else None,
                        layer=l, tm=tm_small, na=na_small, group_of=group_of)
        if not last:
            xa, xb, xb_off = res, res, na_small

    y_prompt = res[0].reshape(bc, lc, D_MODEL)
    y_sample = res[1].reshape(bl, ll, D_MODEL)
    shape_kv = (bc, depth, lc, N_KV_HEADS, HEAD_DIM)
    shape_st = (bc, depth, N_SSD_HEADS, SSD_HEADDIM, D_STATE)
    return (y_prompt, y_sample, k_new.reshape(shape_kv), v_new.reshape(shape_kv),
            states[0].reshape(shape_st), states[1].reshape(shape_st))
```

```python
import functools
import math

import numpy as np
import jax
import jax.numpy as jnp
from jax import lax
from jax.experimental import pallas as pl
from jax.experimental.pallas import tpu as pltpu

F32 = jnp.float32
BF16 = jnp.bfloat16

D_MODEL = 2048
HEAD_DIM = 128
N_Q_HEADS = 16
N_KV_HEADS = 4
Q_PER_KV = 4
ATTN_WIDTH = 2048
KV_WIDTH = 512
WINDOW = 128
GRID_W = 64
ROPE_BASE = 10000.0
SSD_WIDTH = 4096
SSD_HEADDIM = 64
N_SSD_HEADS = 64
D_STATE = 128
N_SSD_GROUPS = 8
HEADS_PER_GROUP = 8
GROUP_WIDTH = SSD_WIDTH // N_SSD_GROUPS
CHUNK = 128
D_CONV = 5
BC_WIDTH = 1024
CONV_WIDTH = 6144
EPS = 1e-6
MOD_WIDTH = 3 * D_MODEL
LOG2_E = math.log2(math.e)

COL_Q = 0
COL_K = 2048
COL_V = 2560
COL_ZA = 3072
COL_XBC = 5120
COL_ZS = 11264
COL_GA = 15360
COL_GS = 17408
ACT_WIDTH = 19456
W_IN_DT = 15360
DT_WIDTH = 2 * N_SSD_HEADS

LANES = 128
SUBLANES = 8
VMEM_LIMIT = 56 * 1024 * 1024

ROWS_BIG = 1024
ROWS_SMALL = 512
MOD_TN = 512
BRANCH_TN = 512
CTX_ATTN_SEQS = 8
LAT_ATTN_UNROLL = 4

COND_ROWS = 8


def _cparams(sem):
    return pltpu.CompilerParams(dimension_semantics=sem, vmem_limit_bytes=VMEM_LIMIT)


def _dot(a, b):
    return jnp.dot(a, b, preferred_element_type=F32)


def _dot_nt(a, b):
    return lax.dot_general(a, b, (((1,), (1,)), ((), ())), preferred_element_type=F32)


def _split2(x):
    hi = x.astype(BF16)
    lo = (x - hi.astype(F32)).astype(BF16)
    return hi, lo


def _split3(x):
    p1 = x.astype(BF16)
    r1 = x - p1.astype(F32)
    p2 = r1.astype(BF16)
    p3 = (r1 - p2.astype(F32)).astype(BF16)
    return p1, p2, p3


def _dot3(a, b):
    ah, al = _split2(a)
    bh, bl = _split2(b)
    return _dot(ah, bh) + _dot(al, bh) + _dot(ah, bl)


def _sigmoid(x):
    return 0.5 + 0.5 * jnp.tanh(0.5 * x)


def _silu(x):
    half = 0.5 * x
    return half + half * jnp.tanh(half)


def _softplus(x):
    return jnp.maximum(x, 0.0) + jnp.log1p(jnp.exp(-jnp.abs(x)))


def _mod_kernel(cond_ref, w_ref, b_ref, o_ref):
    res = _dot3(_silu(cond_ref[...]), w_ref[...]) + b_ref[...]
    for r in range(COND_ROWS):
        o_ref[r] = res[r:r + 1, :]


def _modulation(cond, w_mod, b_mod):
    depth = w_mod.shape[0]
    tn = MOD_TN
    return pl.pallas_call(
        _mod_kernel,
        grid=(depth, MOD_WIDTH // tn),
        in_specs=[
            pl.BlockSpec((COND_ROWS, D_MODEL), lambda l, j: (0, 0)),
            pl.BlockSpec((None, D_MODEL, tn), lambda l, j: (l, 0, j)),
            pl.BlockSpec((None, 1, tn), lambda l, j: (l, 0, j)),
        ],
        out_specs=pl.BlockSpec((None, COND_ROWS, 1, tn), lambda l, j: (l, 0, 0, j)),
        out_shape=jax.ShapeDtypeStruct((depth, COND_ROWS, 1, MOD_WIDTH), F32),
        compiler_params=_cparams(("arbitrary", "arbitrary")),
        name="modulation",
    )(cond, w_mod, b_mod.reshape(depth, 1, MOD_WIDTH))


def _prep_kernel(xa_ref, xb_ref, shift_ref, scale_ref, g_ref, wdtt_ref, biast_ref, alogt_ref,
                 sel1_ref, sel2_ref, h_ref, p1_ref, p2_ref, row_ref, *, tm, na):
    x = jnp.where(pl.program_id(0) < na, xa_ref[...], xb_ref[...])
    ms = jnp.mean(x * x, axis=-1, keepdims=True)
    h = (x * lax.rsqrt(ms + EPS) * g_ref[...]) * (1.0 + scale_ref[...]) + shift_ref[...]
    h_ref[...] = h.astype(BF16)

    hh, hl = _split2(h)
    wth, wtl = _split2(wdtt_ref[...])
    rawt = _dot_nt(wth, hh) + _dot_nt(wth, hl) + _dot_nt(wtl, hh)
    dtt = _softplus(rawt + biast_ref[...])
    dtat = dtt * (-jnp.exp(alogt_ref[...]))

    ii = lax.broadcasted_iota(jnp.int32, (CHUNK, CHUNK), 0)
    kk = lax.broadcasted_iota(jnp.int32, (CHUNK, CHUNK), 1)
    lt = jnp.where(kk <= ii, 1.0, 0.0).astype(BF16)
    ut = jnp.where(kk >= ii, 1.0, 0.0).astype(BF16)
    fwd_row = (ii % N_HD) < HEADS_PER_GROUP

    for c in range(tm // CHUNK):
        rows = slice(c * CHUNK, (c + 1) * CHUNK)
        dt_t = dtt[:, rows]
        q1, q2, q3 = _split3(dtat[:, rows])
        pre_t = _dot(q1, ut) + _dot(q2, ut) + _dot(q3, ut)
        suf_t = _dot(q1, lt) + _dot(q2, lt) + _dot(q3, lt)
        acs_t = jnp.where(fwd_row, pre_t, suf_t)
        edge_t = jnp.where(fwd_row, acs_t[:, CHUNK - 1:CHUNK], acs_t[:, 0:1])
        w1_t = dt_t * jnp.exp(edge_t - acs_t)
        row_ref[c, 0] = (acs_t - jnp.log(dt_t)) * LOG2_E
        row_ref[c, 1] = dt_t
        acs = acs_t.T
        p1 = _dot(jnp.concatenate(_split3(acs * LOG2_E), axis=1), sel1_ref[...])
        p1_ref[rows, :] = p1.astype(BF16)
        p2 = _dot(jnp.concatenate(_split3(jnp.exp(acs)) + _split3(w1_t.T), axis=1), sel2_ref[...])
        p2_ref[rows, :] = p2.astype(BF16)


def _row_split_specs(tm, na, xb_offset):
    return [
        pl.BlockSpec((tm, D_MODEL), lambda i: (jnp.minimum(i, na - 1), 0)),
        pl.BlockSpec((tm, D_MODEL), lambda i: (xb_offset + jnp.maximum(i - na, 0), 0)),
    ]


def _prep(xa, xb, xb_offset, m, mod, norm_g, wdtt, bias, alog, consts, *, layer, tm, na, group_of):
    l = layer
    kern = functools.partial(_prep_kernel, tm=tm, na=na)
    par2 = lambda i: (l, 0, 0)
    whole = lambda i: (0, 0)
    pw = N_SSD_GROUPS * LANES
    return pl.pallas_call(
        kern,
        grid=(m // tm,),
        in_specs=_row_split_specs(tm, na, xb_offset) + [
            pl.BlockSpec((None, None, 1, D_MODEL), lambda i: (l, group_of(i, tm), 0, 0)),
            pl.BlockSpec((None, None, 1, D_MODEL), lambda i: (l, group_of(i, tm), 0, 1)),
            pl.BlockSpec((None, 1, D_MODEL), par2),
            pl.BlockSpec((None, DT_WIDTH, D_MODEL), par2),
            pl.BlockSpec((None, DT_WIDTH, 1), par2),
            pl.BlockSpec((None, DT_WIDTH, 1), par2),
            pl.BlockSpec((N_PIECES * LANES, pw), whole),
            pl.BlockSpec((2 * N_PIECES * LANES, pw), whole),
        ],
        out_specs=[
            pl.BlockSpec((tm, D_MODEL), lambda i: (i, 0)),
            pl.BlockSpec((tm, pw), lambda i: (i, 0)),
            pl.BlockSpec((tm, pw), lambda i: (i, 0)),
            pl.BlockSpec((tm // CHUNK, 2, DT_WIDTH, CHUNK), lambda i: (i, 0, 0, 0)),
        ],
        out_shape=[
            jax.ShapeDtypeStruct((m, D_MODEL), BF16),
            jax.ShapeDtypeStruct((m, pw), BF16),
            jax.ShapeDtypeStruct((m, pw), BF16),
            jax.ShapeDtypeStruct((m // CHUNK, 2, DT_WIDTH, CHUNK), F32),
        ],
        compiler_params=_cparams(("arbitrary",)),
        name="prep",
    )(xa, xb, mod, mod, norm_g, wdtt, bias[:, :, None], alog[:, :, None], consts["sel1"],
      consts["sel2"])


N_PIECES = 3
N_HD = 2 * HEADS_PER_GROUP
PIECE_LANES = N_PIECES * N_HD
N_EXPAND = 4


def _ssd_constants():
    pw = N_SSD_GROUPS * LANES
    sel1 = np.zeros((N_PIECES * LANES, pw), np.float32)
    sel2 = np.zeros((2 * N_PIECES * LANES, pw), np.float32)
    for g in range(N_SSD_GROUPS):
        for hd in range(N_HD):
            lam = g * N_HD + hd
            for p in range(N_PIECES):
                sel1[p * LANES + lam, g * LANES + N_PIECES * hd + p] = 1.0
                for q in range(2):
                    sel2[(q * N_PIECES + p) * LANES + lam,
                         g * LANES + q * PIECE_LANES + N_PIECES * hd + p] = 1.0
    cbc = np.zeros((LANES, N_HD * LANES), np.float32)
    for hd in range(N_HD):
        cbc[N_PIECES * hd:N_PIECES * (hd + 1), hd * LANES:(hd + 1) * LANES] = 1.0
    eexp = np.zeros((N_EXPAND, LANES, GROUP_WIDTH), np.float32)
    for e, (q, d) in enumerate(((0, 0), (1, 0), (0, 1), (1, 1))):
        for r in range(HEADS_PER_GROUP):
            row0 = q * PIECE_LANES + N_PIECES * (d * HEADS_PER_GROUP + r)
            eexp[e, row0:row0 + N_PIECES, r * SSD_HEADDIM:(r + 1) * SSD_HEADDIM] = 1.0
    return {k: jnp.asarray(v, BF16) for k, v in
            (("sel1", sel1), ("sel2", sel2), ("cbc", cbc), ("eexp", eexp))}


INPROJ_TN = 1024
KV_BLOCK = COL_K // INPROJ_TN


def _inproj_kernel(*refs, na, lc, aliased):
    if aliased:
        h_ref, w_ref, _kprev, _vprev, act_ref, k_ref, v_ref, wbf_s = refs
    else:
        h_ref, w_ref, act_ref, k_ref, v_ref, wbf_s = refs
    j = pl.program_id(0)
    i = pl.program_id(1)

    @pl.when(i == 0)
    def _():
        wbf_s[...] = w_ref[...].astype(BF16)

    is_gate = (((j >= COL_ZA // INPROJ_TN) & (j < COL_XBC // INPROJ_TN))
               | ((j >= COL_ZS // INPROJ_TN) & (j < COL_GA // INPROJ_TN)))

    @pl.when(is_gate)
    def _():
        act_ref[...] = _silu(_dot(h_ref[...], wbf_s[...])).astype(BF16)

    @pl.when(jnp.logical_not(is_gate))
    def _():
        acc = _dot(h_ref[...], wbf_s[...])
        act_ref[...] = acc.astype(BF16)

        @pl.when((j == KV_BLOCK) & (i < na))
        def _():
            for s in range(k_ref.shape[0]):
                for dst, col0 in ((k_ref, 0), (v_ref, KV_WIDTH)):
                    for hh in range(N_KV_HEADS):
                        cols = slice(col0 + hh * HEAD_DIM, col0 + (hh + 1) * HEAD_DIM)
                        dst[s, pl.ds(hh, lc, stride=N_KV_HEADS), :] = (
                            acc[s * lc:(s + 1) * lc, cols])


def _inproj(h, w_in, k_prev, v_prev, *, layer, tm, na, lc, bc):
    m = h.shape[0]
    depth = w_in.shape[0]
    tn = INPROJ_TN
    l = layer
    spb = tm // lc
    aliased = k_prev is not None

    def w_col(j):
        skip = jnp.where(j >= W_IN_DT // tn, DT_WIDTH // LANES, 0)
        return (j * (tn // LANES) + skip) * LANES

    def kv_idx(j, i):
        return jnp.where(j < KV_BLOCK, 0, jnp.where(j == KV_BLOCK, jnp.minimum(i, na - 1), na - 1))

    in_specs = [
        pl.BlockSpec((tm, D_MODEL), lambda j, i: (i, 0)),
        pl.BlockSpec((None, pl.Element(D_MODEL), pl.Element(tn)), lambda j, i: (l, 0, w_col(j))),
    ]
    args = [h, w_in]
    aliases = {}
    if aliased:
        in_specs += [pl.BlockSpec(memory_space=pl.ANY)] * 2
        args += [k_prev, v_prev]
        aliases = {2: 1, 3: 2}
    kv_rows = lc * N_KV_HEADS
    kv_spec = pl.BlockSpec((spb, None, kv_rows, HEAD_DIM), lambda j, i: (kv_idx(j, i), l, 0, 0))
    return pl.pallas_call(
        functools.partial(_inproj_kernel, na=na, lc=lc, aliased=aliased),
        grid=(ACT_WIDTH // tn, m // tm),
        in_specs=in_specs,
        out_specs=[pl.BlockSpec((tm, tn), lambda j, i: (i, j)), kv_spec, kv_spec],
        out_shape=[
            jax.ShapeDtypeStruct((m, ACT_WIDTH), BF16),
            jax.ShapeDtypeStruct((bc, depth, kv_rows, HEAD_DIM), F32),
            jax.ShapeDtypeStruct((bc, depth, kv_rows, HEAD_DIM), F32),
        ],
        scratch_shapes=[pltpu.VMEM((D_MODEL, tn), BF16)],
        input_output_aliases=aliases,
        compiler_params=_cparams(("arbitrary", "arbitrary")),
        name="inproj",
    )(*args)


PAD = SUBLANES
CHUNK_UNROLL = 4
SSD_SCRATCH_BUDGET = 28 * 1024 * 1024


def _ssd_kernel(*refs, L, per_step, has_init, emit_state, n_alias):
    refs = list(refs)
    seq_in = refs[:7]
    shared = refs[7:17]
    pos = 17
    state_in = []
    if has_init:
        state_in = refs[pos:pos + 2]
        pos += 2
    pos += n_alias
    y_ref = refs[pos]
    pos += 1
    state_out = []
    if emit_state:
        state_out = refs[pos:pos + 2]
        pos += 2
    scratch = refs[pos:]
    nc = L // CHUNK
    for sq in range(per_step):
        rows = pl.ds(sq * L, L)
        views = [r.at[rows] for r in seq_in[:6]] + [seq_in[6].at[pl.ds(sq * nc, nc)]]
        _ssd_sequence(views, shared, [r.at[pl.ds(sq, 1)] for r in state_in], y_ref.at[rows],
                      [r.at[pl.ds(sq, 1)] for r in state_out], [s.at[sq] for s in scratch], L=L)


def _ssd_sequence(seq_in, shared, state_in, y_ref, state_out, scratch, *, L):
    x_ref, b_ref, c_ref, z_ref, p1_ref, p2_ref, row_ref = seq_in
    (cbc_ref, eexp_ref, cwx_ref, cwb_ref, cwc_ref, cbx_ref, cbb_ref, cbias_c_ref, dsk_ref,
     ng_ref) = shared
    has_init = bool(state_in)
    emit_state = bool(state_out)
    if has_init:
        s0f_ref, s0b_ref = state_in
    if emit_state:
        sf_ref, sb_ref = state_out
    pad_s, xc_s, bc_s, cc_s, bt_s, cum_s, exp_s, yacc_s, sft_s, sbt_s = scratch
    nc = L // CHUNK

    tile = 2 * LANES
    conv_srcs = ((x_ref, cwx_ref, cbx_ref, xc_s, 0, GROUP_WIDTH),
                 (b_ref, cwb_ref, cbb_ref, bc_s, GROUP_WIDTH, D_STATE),
                 (c_ref, cwc_ref, cbias_c_ref, cc_s, GROUP_WIDTH + D_STATE, D_STATE))
    zeros = jnp.zeros((PAD, pad_s.shape[1]), F32)
    pad_s[0:PAD, :] = zeros
    pad_s[L + PAD:L + 2 * PAD, :] = zeros
    for src_ref, _, _, _, off, width in conv_srcs:
        for c in range(nc):
            pad_s[PAD + c * CHUNK:PAD + (c + 1) * CHUNK, off:off + width] = (
                src_ref[c * CHUNK:(c + 1) * CHUNK, :].astype(F32))

    def conv_slab(w_ref, bias_ref, dst_s, off, c, s, zero_row):
        ls = slice(s * LANES, (s + 1) * LANES)
        ps = slice(off + s * LANES, off + (s + 1) * LANES)
        halo = pad_s[c * CHUNK:(c + 1) * CHUNK + 2 * PAD, ps]
        centre = D_CONV // 2
        acc = jnp.broadcast_to(bias_ref[:, ls] + zero_row, (CHUNK, LANES))
        for k in range(D_CONV):
            shifted = halo if k == centre else pltpu.roll(halo, (centre - k) % halo.shape[0], 0)
            acc = acc + w_ref[k:k + 1, ls] * shifted[PAD:PAD + CHUNK, :]
        dst_s[c * CHUNK:(c + 1) * CHUNK, ls] = _silu(acc)

    conv_items = [functools.partial(conv_slab, w_ref, bias_ref, dst_s, off, c, s)
                  for _, w_ref, bias_ref, dst_s, off, width in conv_srcs
                  for c in range(nc) for s in range(width // LANES)]


    def zero_row_of(res):
        bits = pltpu.bitcast(res[0:SUBLANES, 0:LANES], jnp.uint32)
        return pltpu.bitcast((bits >> 16) >> 16, F32)[0:1, :]

    def cum_tile(t):
        ls = slice(t * tile, (t + 1) * tile)
        res = _dot(p1_ref[...], cbc_ref[:, ls])
        cum_s[:, ls] = res
        return zero_row_of(res)

    def exp_tile(e, t):
        ls = slice(t * tile, (t + 1) * tile)
        res = _dot(p2_ref[...], eexp_ref[e, :, ls])
        exp_s[e, :, ls] = res
        return zero_row_of(res)

    spread_items = ([functools.partial(cum_tile, t) for t in range(N_HD * LANES // tile)]
                    + [functools.partial(exp_tile, e, t) for e in range(N_EXPAND)
                       for t in range(GROUP_WIDTH // tile)])

    merged = sorted([((i + 0.5) / len(conv_items), 0, f) for i, f in enumerate(conv_items)]
                    + [((i + 0.5) / len(spread_items), 1, f) for i, f in enumerate(spread_items)],
                    key=lambda item: item[:2])
    zero_row = jnp.zeros((1, LANES), F32)
    for _, is_spread, emit in merged:
        if is_spread:
            zero_row = emit()
        else:
            emit(zero_row)

    if has_init:
        sft_s[...] = s0f_ref[0].T
        sbt_s[...] = s0b_ref[0].T
    else:
        sft_s[...] = jnp.zeros_like(sft_s)
        sbt_s[...] = jnp.zeros_like(sbt_s)

    ii = lax.broadcasted_iota(jnp.int32, (CHUNK, CHUNK), 0)
    jj = lax.broadcasted_iota(jnp.int32, (CHUNK, CHUNK), 1)
    lower = jj <= ii
    diag = jj == ii
    left = jj < SSD_HEADDIM

    def fwd_chunk(c, carry):
        r0 = pl.multiple_of(c * CHUNK, CHUNK)
        rows = pl.ds(r0, CHUNK)
        xq = xc_s[rows, :]
        bq = bc_s[rows, :]
        cq = cc_s[rows, :].astype(BF16)
        rowa = row_ref[c, 0]
        rowd = row_ref[c, 1]
        cb = _dot_nt(cq, bq.astype(BF16))
        y_off = _dot(cq, sft_s[...].astype(BF16)) * exp_s[0, rows, :]
        y_parts = []
        for k in range(HEADS_PER_GROUP // 2):
            ms = []
            for r in (2 * k, 2 * k + 1):
                rb = HEADS_PER_GROUP + r
                seg_f = cum_s[rows, r * LANES:(r + 1) * LANES] - rowa[r:r + 1, :]
                seg_b = cum_s[rows, rb * LANES:(rb + 1) * LANES] - rowa[rb:rb + 1, :]
                dm = jnp.exp2(jnp.where(lower, seg_f, seg_b))
                dm = dm + jnp.where(diag, rowd[rb:rb + 1, :], 0.0)
                ms.append((cb * dm).astype(BF16))
            lhs = jnp.concatenate(ms, axis=1)
            xp = xq[:, k * LANES:(k + 1) * LANES]
            rhs = jnp.concatenate([jnp.where(left, xp, 0.0), jnp.where(left, 0.0, xp)],
                                  axis=0).astype(BF16)
            y_parts.append(_dot(lhs, rhs))
        yacc_s[rows, :] = jnp.concatenate(y_parts, axis=1) + y_off
        decay = exp_s[0, pl.ds(r0 + CHUNK - 1, 1), :]
        bt = bq.T.astype(BF16)
        bt_s[c] = bt
        sft_s[...] = sft_s[...] * decay + _dot(bt, (xq * exp_s[1, rows, :]).astype(BF16))
        return carry

    lax.fori_loop(0, nc, fwd_chunk, 0, unroll=CHUNK_UNROLL)

    def bwd_chunk(t, carry):
        c = nc - 1 - t
        r0 = pl.multiple_of(c * CHUNK, CHUNK)
        rows = pl.ds(r0, CHUNK)
        xq = xc_s[rows, :]
        cq = cc_s[rows, :].astype(BF16)
        y = (yacc_s[rows, :] + _dot(cq, sbt_s[...].astype(BF16)) * exp_s[2, rows, :]
             + dsk_ref[...] * xq)
        y = y * z_ref[rows, :].astype(F32)
        ms = jnp.mean(y * y, axis=-1, keepdims=True)
        y_ref[rows, :] = (y * lax.rsqrt(ms + EPS) * ng_ref[...]).astype(BF16)
        decay = exp_s[2, pl.ds(r0, 1), :]
        sbt_s[...] = sbt_s[...] * decay + _dot(bt_s[c], (xq * exp_s[3, rows, :]).astype(BF16))
        return carry

    lax.fori_loop(0, nc, bwd_chunk, 0, unroll=CHUNK_UNROLL)

    if emit_state:
        sf_ref[0] = sft_s[...].T
        sb_ref[0] = sbt_s[...].T


def _ssd(act, p1, p2, rowp, consts, conv_w, conv_b, dsk, ng, *, layer, L, nseq, row_block0,
         s0f=None, s0b=None, y_prev=None, state_prev=None):
    m = act.shape[0]
    depth = conv_w.shape[0]
    l = layer
    has_init = s0f is not None
    emit_state = not has_init
    nc = L // CHUNK
    gw = GROUP_WIDTH
    rb = row_block0
    off_b = SSD_WIDTH // D_STATE
    off_c = (SSD_WIDTH + BC_WIDTH) // D_STATE
    seq_scratch = [
        ((L + 2 * PAD, gw + 2 * D_STATE), F32),
        ((L, gw), F32),
        ((L, D_STATE), F32),
        ((L, D_STATE), F32),
        ((nc, D_STATE, CHUNK), BF16),
        ((L, N_HD * LANES), F32),
        ((N_EXPAND, L, gw), F32),
        ((L, gw), F32),
        ((D_STATE, gw), F32),
        ((D_STATE, gw), F32),
    ]
    seq_bytes = sum(math.prod(shape) * jnp.dtype(dt).itemsize for shape, dt in seq_scratch)
    per_step = max(n for n in (1, 2, 4)
                   if nseq % n == 0 and row_block0 % n == 0 and n * seq_bytes <= SSD_SCRATCH_BUDGET)
    rb = row_block0 // per_step
    ls = per_step * L
    in_specs = [
        pl.BlockSpec((ls, gw), lambda b, g: (rb + b, COL_XBC // gw + g)),
        pl.BlockSpec((ls, D_STATE), lambda b, g: (rb + b, COL_XBC // D_STATE + off_b + g)),
        pl.BlockSpec((ls, D_STATE), lambda b, g: (rb + b, COL_XBC // D_STATE + off_c + g)),
        pl.BlockSpec((ls, gw), lambda b, g: (rb + b, COL_ZS // gw + g)),
        pl.BlockSpec((ls, LANES), lambda b, g: (rb + b, g)),
        pl.BlockSpec((ls, LANES), lambda b, g: (rb + b, g)),
        pl.BlockSpec((per_step * nc, 2, N_HD, CHUNK), lambda b, g: (rb + b, 0, g, 0)),
        pl.BlockSpec((LANES, N_HD * LANES), lambda b, g: (0, 0)),
        pl.BlockSpec((N_EXPAND, LANES, gw), lambda b, g: (0, 0, 0)),
        pl.BlockSpec((None, D_CONV, gw), lambda b, g: (l, 0, g)),
        pl.BlockSpec((None, D_CONV, D_STATE), lambda b, g: (l, 0, off_b + g)),
        pl.BlockSpec((None, D_CONV, D_STATE), lambda b, g: (l, 0, off_c + g)),
        pl.BlockSpec((None, 1, gw), lambda b, g: (l, 0, g)),
        pl.BlockSpec((None, 1, D_STATE), lambda b, g: (l, 0, off_b + g)),
        pl.BlockSpec((None, 1, D_STATE), lambda b, g: (l, 0, off_c + g)),
        pl.BlockSpec((None, 1, gw), lambda b, g: (l, 0, g)),
        pl.BlockSpec((None, 1, gw), lambda b, g: (l, 0, g)),
    ]
    args = [act, act, act, act, p1, p2, rowp, consts["cbc"], consts["eexp"], conv_w, conv_w, conv_w,
            conv_b, conv_b, conv_b, dsk, ng]
    aliases = {}
    n_alias = 0
    state_spec = pl.BlockSpec((per_step, None, gw, D_STATE), lambda b, g: (b, l, g, 0))
    if has_init:
        in_specs += [state_spec, state_spec, pl.BlockSpec(memory_space=pl.ANY)]
        args += [s0f, s0b, y_prev]
        aliases = {len(args) - 1: 0}
        n_alias = 1
    elif state_prev is not None:
        in_specs += [pl.BlockSpec(memory_space=pl.ANY)] * 2
        args += list(state_prev)
        aliases = {len(args) - 2: 1, len(args) - 1: 2}
        n_alias = 2
    out_specs = [pl.BlockSpec((ls, gw), lambda b, g: (rb + b, g))]
    out_shape = [jax.ShapeDtypeStruct((m, SSD_WIDTH), BF16)]
    if emit_state:
        out_specs += [state_spec] * 2
        out_shape += [jax.ShapeDtypeStruct((nseq, depth, SSD_WIDTH, D_STATE), F32)] * 2
    scratch = [pltpu.VMEM((per_step,) + shape, dt) for shape, dt in seq_scratch]
    kern = functools.partial(_ssd_kernel, L=L, per_step=per_step, has_init=has_init,
                             emit_state=emit_state, n_alias=n_alias)
    return pl.pallas_call(
        kern,
        grid=(nseq // per_step, N_SSD_GROUPS),
        in_specs=in_specs,
        out_specs=out_specs,
        out_shape=out_shape,
        scratch_shapes=scratch,
        input_output_aliases=aliases,
        compiler_params=_cparams(("arbitrary", "arbitrary")),
        name="ssd_latent" if has_init else "ssd_context",
    )(*args)


ATTN_SCALE = HEAD_DIM ** -0.5


def _ctx_attn_kernel(sink_ref, q_ref, k_ref, v_ref, z_ref, o_ref, *, L):
    g = pl.program_id(1)
    for sq in range(q_ref.shape[0] // L):
        rows = slice(sq * L, (sq + 1) * L)
        k = k_ref[rows, :]
        v = v_ref[rows, :]
        for r in range(Q_PER_KV):
            ls = slice(r * HEAD_DIM, (r + 1) * HEAD_DIM)
            sink = sink_ref[g * Q_PER_KV + r]
            s = _dot_nt(q_ref[rows, ls], k) * ATTN_SCALE
            m = jnp.maximum(jnp.max(s, axis=-1, keepdims=True), sink)
            p = jnp.exp(s - m)
            denom = jnp.sum(p, axis=-1, keepdims=True) + jnp.exp(sink - m)
            o = _dot(p.astype(BF16), v) / denom
            o_ref[rows, ls] = (o * z_ref[rows, ls].astype(F32)).astype(BF16)


def _ctx_attention(act, sink, *, L, nseq):
    m = act.shape[0]
    gw = Q_PER_KV * HEAD_DIM
    per_step = math.gcd(nseq, CTX_ATTN_SEQS)
    rows = per_step * L
    return pl.pallas_call(
        functools.partial(_ctx_attn_kernel, L=L),
        grid=(nseq // per_step, N_KV_HEADS),
        in_specs=[
            pl.BlockSpec(memory_space=pltpu.SMEM),
            pl.BlockSpec((rows, gw), lambda b, g: (b, COL_Q // gw + g)),
            pl.BlockSpec((rows, HEAD_DIM), lambda b, g: (b, COL_K // HEAD_DIM + g)),
            pl.BlockSpec((rows, HEAD_DIM), lambda b, g: (b, COL_V // HEAD_DIM + g)),
            pl.BlockSpec((rows, gw), lambda b, g: (b, COL_ZA // gw + g)),
        ],
        out_specs=pl.BlockSpec((rows, gw), lambda b, g: (b, g)),
        out_shape=jax.ShapeDtypeStruct((m, ATTN_WIDTH), BF16),
        compiler_params=_cparams(("arbitrary", "arbitrary")),
        name="attn_context",
    )(sink, act, act, act, act)


def _rope_tables(length):
    sec = HEAD_DIM // 2
    half = sec // 2
    d = np.arange(HEAD_DIM)
    e = d % sec
    freqs = ROPE_BASE ** (-(e % half).astype(np.float64) / half)
    t = np.arange(length)
    pos = np.where((d // sec)[None, :] == 0, (t // GRID_W)[:, None], (t % GRID_W)[:, None])
    ang = pos.astype(np.float64) * freqs[None, :]
    sign = np.where(e < half, -1.0, 1.0)[None, :]
    return (jnp.asarray(np.cos(ang), F32), jnp.asarray(np.sin(ang) * sign, F32))


def _rope(x, cos, sin_signed, first_half):
    partner = jnp.where(first_half, pltpu.roll(x, LANES - HEAD_DIM // 4, 1),
                        pltpu.roll(x, HEAD_DIM // 4, 1))
    return x * cos + partner * sin_signed


def _lat_attn_kernel(sink_ref, q_ref, k_ref, v_ref, z_ref, kc_ref, vc_ref, cos_ref, sin_ref,
                     _oprev_ref, o_ref, kctx_s, vctx_s, keys_s, vals_s, *, L, lc):
    g = pl.program_id(1)
    nb = L // CHUNK
    win = 3 * CHUNK
    rows4 = Q_PER_KV * CHUNK
    lane = lax.broadcasted_iota(jnp.int32, (CHUNK, HEAD_DIM), 1)
    first_half = (lane % (HEAD_DIM // 2)) < (HEAD_DIM // 4)

    kctx_s[...] = kc_ref[...].astype(BF16)
    vctx_s[...] = vc_ref[...].astype(BF16)
    zero_blk = jnp.zeros((CHUNK, HEAD_DIM), BF16)
    for dst in (keys_s, vals_s):
        dst[0:CHUNK, :] = zero_blk
        dst[CHUNK + L:2 * CHUNK + L, :] = zero_blk
    for n in range(nb):
        rows = slice(n * CHUNK, (n + 1) * CHUNK)
        kr = _rope(k_ref[rows, :].astype(F32), cos_ref[rows, :], sin_ref[rows, :], first_half)
        keys_s[CHUNK + n * CHUNK:CHUNK + (n + 1) * CHUNK, :] = kr.astype(BF16)
        vals_s[CHUNK + n * CHUNK:CHUNK + (n + 1) * CHUNK, :] = v_ref[rows, :]

    qi = lax.broadcasted_iota(jnp.int32, (rows4, win), 0) % CHUNK
    wi = lax.broadcasted_iota(jnp.int32, (rows4, win), 1)
    band = jnp.abs(qi - wi + CHUNK) <= WINDOW
    head = lax.broadcasted_iota(jnp.int32, (rows4, 1), 0) // CHUNK
    sink = jnp.zeros((rows4, 1), F32)
    for r in range(Q_PER_KV):
        sink = jnp.where(head == r, sink_ref[g * Q_PER_KV + r], sink)

    def block(n, carry):
        r0 = pl.multiple_of(n * CHUNK, CHUNK)
        rows = pl.ds(r0, CHUNK)
        cos = cos_ref[rows, :]
        sin = sin_ref[rows, :]
        q = jnp.concatenate(
            [_rope(q_ref[rows, r * HEAD_DIM:(r + 1) * HEAD_DIM].astype(F32), cos, sin,
                   first_half).astype(BF16) for r in range(Q_PER_KV)], axis=0)
        in_seq = (wi >= CHUNK - r0) & (wi < L + CHUNK - r0)
        s_ctx = _dot_nt(q, kctx_s[...]) * ATTN_SCALE
        s_lat = _dot_nt(q, keys_s[pl.ds(r0, win), :]) * ATTN_SCALE
        s_lat = jnp.where(in_seq, jnp.where(band, s_lat, -jnp.inf), -jnp.inf)
        m = jnp.maximum(jnp.maximum(jnp.max(s_ctx, axis=-1, keepdims=True),
                                    jnp.max(s_lat, axis=-1, keepdims=True)), sink)
        p_ctx = jnp.exp(s_ctx - m)
        p_lat = jnp.exp(s_lat - m)
        denom = (jnp.sum(p_ctx, axis=-1, keepdims=True) + jnp.sum(p_lat, axis=-1, keepdims=True)
                 + jnp.exp(sink - m))
        o = (_dot(p_ctx.astype(BF16), vctx_s[...])
             + _dot(p_lat.astype(BF16), vals_s[pl.ds(r0, win), :])) / denom
        for r in range(Q_PER_KV):
            ls = slice(r * HEAD_DIM, (r + 1) * HEAD_DIM)
            o_ref[rows, ls] = (o[r * CHUNK:(r + 1) * CHUNK, :]
                               * z_ref[rows, ls].astype(F32)).astype(BF16)
        return carry

    lax.fori_loop(0, nb, block, 0, unroll=LAT_ATTN_UNROLL)


def _lat_attention(act, sink, cache_k, cache_v, o_prev, *, L, nseq, row_block0, layer):
    m = act.shape[0]
    gw = Q_PER_KV * HEAD_DIM
    lc = cache_k.shape[2]
    rb = row_block0
    cos, sin = _rope_tables(L)
    kc = cache_k.reshape(cache_k.shape[0], cache_k.shape[1], lc, KV_WIDTH)
    vc = cache_v.reshape(cache_v.shape[0], cache_v.shape[1], lc, KV_WIDTH)
    kern = functools.partial(_lat_attn_kernel, L=L, lc=lc)
    tab = lambda b, g: (0, 0)
    return pl.pallas_call(
        kern,
        grid=(nseq, N_KV_HEADS),
        in_specs=[
            pl.BlockSpec(memory_space=pltpu.SMEM),
            pl.BlockSpec((L, gw), lambda b, g: (rb + b, COL_Q // gw + g)),
            pl.BlockSpec((L, HEAD_DIM), lambda b, g: (rb + b, COL_K // HEAD_DIM + g)),
            pl.BlockSpec((L, HEAD_DIM), lambda b, g: (rb + b, COL_V // HEAD_DIM + g)),
            pl.BlockSpec((L, gw), lambda b, g: (rb + b, COL_ZA // gw + g)),
            pl.BlockSpec((None, None, lc, HEAD_DIM), lambda b, g: (b, layer, 0, g)),
            pl.BlockSpec((None, None, lc, HEAD_DIM), lambda b, g: (b, layer, 0, g)),
            pl.BlockSpec((L, HEAD_DIM), tab),
            pl.BlockSpec((L, HEAD_DIM), tab),
            pl.BlockSpec(memory_space=pl.ANY),
        ],
        out_specs=pl.BlockSpec((L, gw), lambda b, g: (rb + b, g)),
        out_shape=jax.ShapeDtypeStruct((m, ATTN_WIDTH), BF16),
        scratch_shapes=[pltpu.VMEM((lc, HEAD_DIM), BF16), pltpu.VMEM((lc, HEAD_DIM), BF16),
                        pltpu.VMEM((L + 2 * CHUNK, HEAD_DIM), BF16),
                        pltpu.VMEM((L + 2 * CHUNK, HEAD_DIM), BF16)],
        input_output_aliases={9: 0},
        compiler_params=_cparams(("arbitrary", "arbitrary")),
        name="attn_latent",
    )(sink, act, act, act, act, kc, vc, cos, sin, o_prev)


def _branch_kernel(oa_ref, ys_ref, wpa_ref, wps_ref, ga_ref, gs_ref, o_ref):
    a = _dot(oa_ref[...], wpa_ref[...])
    s = _dot(ys_ref[...], wps_ref[...])
    merged = _sigmoid(ga_ref[...].astype(F32)) * a + _sigmoid(gs_ref[...].astype(F32)) * s
    o_ref[...] = merged.astype(BF16)


def _branches(oa, ys, w_pa, w_ps, act, *, layer, tm):
    m = oa.shape[0]
    tn = BRANCH_TN
    l = layer
    return pl.pallas_call(
        _branch_kernel,
        grid=(m // tm, D_MODEL // tn),
        in_specs=[
            pl.BlockSpec((tm, ATTN_WIDTH), lambda i, j: (i, 0)),
            pl.BlockSpec((tm, SSD_WIDTH), lambda i, j: (i, 0)),
            pl.BlockSpec((None, ATTN_WIDTH, tn), lambda i, j: (l, 0, j)),
            pl.BlockSpec((None, SSD_WIDTH, tn), lambda i, j: (l, 0, j)),
            pl.BlockSpec((tm, tn), lambda i, j: (i, COL_GA // tn + j)),
            pl.BlockSpec((tm, tn), lambda i, j: (i, COL_GS // tn + j)),
        ],
        out_specs=pl.BlockSpec((tm, tn), lambda i, j: (i, j)),
        out_shape=jax.ShapeDtypeStruct((m, D_MODEL), BF16),
        compiler_params=_cparams(("arbitrary", "arbitrary")),
        name="branches",
    )(oa, ys, w_pa, w_ps, act, act)


def _out_kernel(*refs, final, na):
    if final:
        mg_ref, w_ref, xa_ref, xb_ref, gate_ref, fg_ref, ya_ref, yb_ref = refs
    else:
        mg_ref, w_ref, xa_ref, xb_ref, gate_ref, o_ref = refs
    is_ctx = pl.program_id(0) < na
    x = jnp.where(is_ctx, xa_ref[...], xb_ref[...])
    y = x + gate_ref[...] * _dot(mg_ref[...], w_ref[...])
    if not final:
        o_ref[...] = y
        return
    ms = jnp.mean(y * y, axis=-1, keepdims=True)
    y = y * lax.rsqrt(ms + EPS) * fg_ref[...]

    @pl.when(is_ctx)
    def _():
        ya_ref[...] = y

    @pl.when(jnp.logical_not(is_ctx))
    def _():
        yb_ref[...] = y


def _out_proj(merged, w_out, xa, xb, xb_offset, mod, final_g, *, layer, tm, na, group_of):
    m = merged.shape[0]
    l = layer
    final = final_g is not None
    in_specs = [
        pl.BlockSpec((tm, D_MODEL), lambda i: (i, 0)),
        pl.BlockSpec((None, D_MODEL, D_MODEL), lambda i: (l, 0, 0)),
    ] + _row_split_specs(tm, na, xb_offset) + [
        pl.BlockSpec((None, None, 1, D_MODEL), lambda i: (l, group_of(i, tm), 0, 2)),
    ]
    args = [merged, w_out, xa, xb, mod]
    if final:
        in_specs.append(pl.BlockSpec((1, D_MODEL), lambda i: (0, 0)))
        args.append(final_g.reshape(1, D_MODEL))
        out_specs = [
            pl.BlockSpec((tm, D_MODEL), lambda i: (jnp.minimum(i, na - 1), 0)),
            pl.BlockSpec((tm, D_MODEL), lambda i: (jnp.maximum(i - na, 0), 0)),
        ]
        out_shape = [jax.ShapeDtypeStruct((na * tm, D_MODEL), F32),
                     jax.ShapeDtypeStruct((m - na * tm, D_MODEL), F32)]
    else:
        out_specs = pl.BlockSpec((tm, D_MODEL), lambda i: (i, 0))
        out_shape = jax.ShapeDtypeStruct((m, D_MODEL), F32)
    return pl.pallas_call(
        functools.partial(_out_kernel, final=final, na=na),
        grid=(m // tm,),
        in_specs=in_specs,
        out_specs=out_specs,
        out_shape=out_shape,
        compiler_params=_cparams(("arbitrary",)),
        name="out_proj_final" if final else "out_proj",
    )(*args)


def _dt_permutation():
    perm = np.zeros(DT_WIDTH, np.int32)
    for g in range(N_SSD_GROUPS):
        for d in range(2):
            for r in range(HEADS_PER_GROUP):
                perm[g * 16 + d * HEADS_PER_GROUP + r] = d * N_SSD_HEADS + g * HEADS_PER_GROUP + r
    return perm


def kernel(x_prompt, x_sample, c, cache_k, cache_v, state_ssm_fwd, state_ssm_bwd, c_ctx, norm_g, w_mod, b_mod, w_in, conv_w, conv_b, attn_sink, a_log_fwd, a_log_bwd, dt_bias_fwd, dt_bias_bwd, d_skip, ssd_norm_g, w_pa, w_ps, w_out, final_norm_g):
    bc, lc, _ = x_prompt.shape
    bl, ll, _ = x_sample.shape
    depth = w_in.shape[0]
    n_ctx = bc * lc
    m = n_ctx + bl * ll
    assert n_ctx % ll == 0 and ll % lc == 0 and lc % CHUNK == 0
    assert 1 + bl <= COND_ROWS

    def group_of(i, tm):
        return jnp.maximum(i * tm - n_ctx + ll, 0) // ll

    tm_big = math.gcd(ROWS_BIG, math.gcd(n_ctx, ll))
    tm_small = math.gcd(ROWS_SMALL, tm_big)

    cond = jnp.zeros((COND_ROWS, D_MODEL), F32).at[0].set(c_ctx).at[1:1 + bl].set(c)
    mod = _modulation(cond, w_mod, b_mod)

    perm = _dt_permutation()
    consts = _ssd_constants()
    wdt = w_in[:, :, W_IN_DT:W_IN_DT + DT_WIDTH][:, :, perm]
    wdtt = jnp.swapaxes(wdt, 1, 2)
    bias = jnp.concatenate([dt_bias_fwd, dt_bias_bwd], axis=1)[:, perm]
    alog = jnp.concatenate([a_log_fwd, a_log_bwd], axis=1)[:, perm]
    norm_g3 = norm_g.reshape(depth, 1, D_MODEL)
    conv_b3 = conv_b.reshape(depth, 1, CONV_WIDTH)
    dsk = jnp.repeat(d_skip, SSD_HEADDIM, axis=1).reshape(depth, 1, SSD_WIDTH)
    ng = ssd_norm_g.reshape(depth, 1, SSD_WIDTH)
    w_pa_bf = w_pa.astype(BF16)
    w_ps_bf = w_ps.astype(BF16)
    w_out_bf = w_out.astype(BF16)
    s0f = state_ssm_fwd.reshape(bl, depth, SSD_WIDTH, D_STATE)
    s0b = state_ssm_bwd.reshape(bl, depth, SSD_WIDTH, D_STATE)

    na_small = n_ctx // tm_small
    na_big = n_ctx // tm_big
    xa, xb, xb_off = x_prompt.reshape(n_ctx, D_MODEL), x_sample.reshape(bl * ll, D_MODEL), 0
    k_new = v_new = states = None
    for l in range(depth):
        h, p1, p2, rowp = _prep(xa, xb, xb_off, m, mod, norm_g3, wdtt, bias, alog, consts,
                                layer=l, tm=tm_small, na=na_small, group_of=group_of)
        act, k_new, v_new = _inproj(h, w_in, k_new, v_new, layer=l, tm=tm_big, na=na_big, lc=lc,
                                    bc=bc)

        sink = attn_sink[l]
        oa = _ctx_attention(act, sink, L=lc, nseq=bc)
        oa = _lat_attention(act, sink, cache_k, cache_v, oa, L=ll, nseq=bl,
                            row_block0=n_ctx // ll, layer=l)

        ys, sf, sb = _ssd(act, p1, p2, rowp, consts, conv_w, conv_b3, dsk, ng, layer=l, L=lc,
                          nseq=bc, row_block0=0, state_prev=states)
        states = (sf, sb)
        (ys,) = _ssd(act, p1, p2, rowp, consts, conv_w, conv_b3, dsk, ng, layer=l, L=ll, nseq=bl,
                     row_block0=n_ctx // ll, s0f=s0f, s0b=s0b, y_prev=ys)

        merged = _branches(oa, ys, w_pa_bf, w_ps_bf, act, layer=l, tm=tm_big)
        last = l == depth - 1
        res = _out_proj(merged, w_out_bf, xa, xb, xb_off, mod, final_norm_g if last else None,
                        layer=l, tm=tm_small, na=na_small, group_of=group_of)
        if not last:
            xa, xb, xb_off = res, res, na_small

    y_prompt = res[0].reshape(bc, lc, D_MODEL)
    y_sample = res[1].reshape(bl, ll, D_MODEL)
    shape_kv = (bc, depth, lc, N_KV_HEADS, HEAD_DIM)
    shape_st = (bc, depth, N_SSD_HEADS, SSD_HEADDIM, D_STATE)
    return (y_prompt, y_sample, k_new.reshape(shape_kv), v_new.reshape(shape_kv),
            states[0].reshape(shape_st), states[1].reshape(shape_st))
```
